```python
import math
import jax, jax.numpy as jnp
from jax import lax
import numpy as np

D_MODEL = 2048
BATCH = 4
SEQ = 4096
DEPTH = 4

D_MIX = D_MODEL
W_GROUP = D_MIX // 4
ROPE_THETA = 500000.0
NORM_EPS = 1e-6
Q_BLOCK = 128

MLA_HEADS = 4
MLA_NOPE = 128
MLA_ROPE = 64
MLA_V = W_GROUP // MLA_HEADS
MLA_Q_LORA = 512
MLA_KV_LORA = 256

S5_GROUP = 16
S5_GROUPS = W_GROUP // S5_GROUP
S5_STATE = 64
S5_DT_MIN = 0.001
S5_DT_MAX = 0.1

RWKV_HEAD = 64
RWKV_HEADS = W_GROUP // RWKV_HEAD
RWKV_DECAY_LORA = 64
RWKV_ICL_LORA = 64
RWKV_GN_EPS = 64e-5

DIFF_HEADS = 4
DIFF_QK = W_GROUP // (2 * DIFF_HEADS)
DIFF_V = 2 * DIFF_QK
DIFF_ROT = DIFF_QK // 4
DIFF_SUBLN_EPS = 1e-5

N_RWKV = 3 * W_GROUP + RWKV_DECAY_LORA + RWKV_ICL_LORA
PROJ_SIZES = (MLA_Q_LORA, MLA_KV_LORA, MLA_ROPE,
              W_GROUP,
              N_RWKV,
              W_GROUP, W_GROUP, W_GROUP,
              D_MIX)
RWKV_SIZES = (W_GROUP, W_GROUP, W_GROUP, RWKV_DECAY_LORA, RWKV_ICL_LORA)
N_IN = sum(PROJ_SIZES)

kernel_name = 'hymba_style_mla_s5_rwkv7_diffattn'


def _split_points(sizes):
    return [int(v) for v in np.cumsum(sizes)[:-1]]


def rms_norm(x, g, eps=NORM_EPS):
    xf = x.astype(jnp.float32)
    y = xf * lax.rsqrt(jnp.mean(xf * xf, axis=-1, keepdims=True) + eps)
    return (y * g.astype(jnp.float32)).astype(x.dtype)


def rope_tables(positions, rot):
    inv = ROPE_THETA ** (-jnp.arange(0, rot, 2, dtype=jnp.float32) / rot)
    ang = positions.astype(jnp.float32)[..., None] * inv
    return jnp.cos(ang)[:, :, None, :], jnp.sin(ang)[:, :, None, :]


def apply_rope(x, cos, sin):
    rot = 2 * cos.shape[-1]
    xr = x[..., :rot].astype(jnp.float32)
    x1, x2 = xr[..., :rot // 2], xr[..., rot // 2:]
    y = jnp.concatenate([x1 * cos - x2 * sin, x2 * cos + x1 * sin], axis=-1).astype(x.dtype)
    return jnp.concatenate([y, x[..., rot:]], axis=-1)


def _to_blocks(t):
    b, s = t.shape[:2]
    t = t.reshape((b, s // Q_BLOCK, Q_BLOCK) + t.shape[2:])
    return jnp.moveaxis(t, 1, 0)


def _from_blocks(t):
    t = jnp.moveaxis(t, 0, 1)
    return t.reshape((t.shape[0], -1) + t.shape[3:])


def _causal_mask(blk, seq):
    qpos = blk * Q_BLOCK + jnp.arange(Q_BLOCK)
    return qpos[:, None] >= jnp.arange(seq)[None, :]


def causal_attention(q, k, v, scale):
    seq = k.shape[1]
    nb = q.shape[1] // Q_BLOCK

    def one_block(args):
        qb, blk = args
        s = jnp.einsum('bqhd,bkhd->bhqk', qb, k, preferred_element_type=jnp.float32) * scale
        s = jnp.where(_causal_mask(blk, seq), s, -jnp.inf)
        p = jax.nn.softmax(s, axis=-1)
        return jnp.einsum('bhqk,bkhd->bqhd', p.astype(v.dtype), v)

    return _from_blocks(lax.map(one_block, (_to_blocks(q), jnp.arange(nb))))


def causal_diff_attention(q, k, v, lam, scale):
    seq = k.shape[1]
    nb = q.shape[1] // Q_BLOCK

    def one_block(args):
        qb, blk = args
        s = jnp.einsum('bqhmd,bkhmd->bhmqk', qb, k, preferred_element_type=jnp.float32) * scale
        s = jnp.where(_causal_mask(blk, seq), s, -jnp.inf)
        p = jax.nn.softmax(s, axis=-1)
        pd = p[:, :, 0] - lam * p[:, :, 1]
        return jnp.einsum('bhqk,bkhd->bqhd', pd.astype(v.dtype), v)

    return _from_blocks(lax.map(one_block, (_to_blocks(q), jnp.arange(nb))))


def mla_branch(c_q, c_kv, k_rope, q_norm_g, kv_norm_g, w_uq, w_ukv, cos, sin):
    b, s, _ = c_q.shape
    q = (rms_norm(c_q, q_norm_g) @ w_uq).reshape(b, s, MLA_HEADS, MLA_NOPE + MLA_ROPE)
    kv = (rms_norm(c_kv, kv_norm_g) @ w_ukv).reshape(b, s, MLA_HEADS, MLA_NOPE + MLA_V)
    q = jnp.concatenate([q[..., :MLA_NOPE], apply_rope(q[..., MLA_NOPE:], cos, sin)], axis=-1)
    k_pe = apply_rope(k_rope[:, :, None, :], cos, sin)
    k = jnp.concatenate([kv[..., :MLA_NOPE],
                         jnp.broadcast_to(k_pe, (b, s, MLA_HEADS, MLA_ROPE))], axis=-1)
    v = kv[..., MLA_NOPE:]
    o = causal_attention(q, k, v, (MLA_NOPE + MLA_ROPE) ** -0.5)
    return o.reshape(b, s, MLA_HEADS * MLA_V)


def s5_branch(u, a_re, a_im, log_dt, b_re, b_im, c_re, c_im, d, w_glu, b_glu):
    f32 = jnp.float32
    bsz, s, _ = u.shape
    uf = u.astype(f32)
    ug = uf.reshape(bsz, s, S5_GROUPS, S5_GROUP)
    lr = jnp.minimum(a_re.astype(f32), -1e-4)
    li = a_im.astype(f32)
    dt = jnp.exp(log_dt.astype(f32))[:, None]
    mag = jnp.exp(dt * lr)
    ab_re, ab_im = mag * jnp.cos(dt * li), mag * jnp.sin(dt * li)
    den = lr * lr + li * li
    nr, ni = ab_re - 1.0, ab_im
    f_re = (nr * lr + ni * li) / den
    f_im = (ni * lr - nr * li) / den
    br, bi = b_re.astype(f32), b_im.astype(f32)
    bb_re = f_re[..., None] * br - f_im[..., None] * bi
    bb_im = f_re[..., None] * bi + f_im[..., None] * br
    bu_re = jnp.einsum('bsgc,gpc->bsgp', ug, bb_re)
    bu_im = jnp.einsum('bsgc,gpc->bsgp', ug, bb_im)
    shape = bu_re.shape
    elems = (jnp.broadcast_to(ab_re, shape), jnp.broadcast_to(ab_im, shape), bu_re, bu_im)

    def combine(e1, e2):
        a1r, a1i, b1r, b1i = e1
        a2r, a2i, b2r, b2i = e2
        return (a2r * a1r - a2i * a1i, a2r * a1i + a2i * a1r,
                a2r * b1r - a2i * b1i + b2r, a2r * b1i + a2i * b1r + b2i)

    _, _, h_re, h_im = lax.associative_scan(combine, elems, axis=1)
    y = (jnp.einsum('bsgp,gcp->bsgc', h_re, c_re.astype(f32))
         - jnp.einsum('bsgp,gcp->bsgc', h_im, c_im.astype(f32)))
    y = y.reshape(bsz, s, W_GROUP) + d.astype(f32) * uf
    g = jax.nn.gelu(y)
    out = g * jax.nn.sigmoid(g @ w_glu.astype(f32) + b_glu.astype(f32))
    return out.astype(u.dtype)


def rwkv7_branch(z, mu, w0, w2, a0, a2, k_k, k_a, r_k, ln_g, ln_b):
    f32 = jnp.float32
    bsz, s, _ = z.shape
    z_prev = jnp.pad(z, ((0, 0), (1, 0), (0, 0)))[:, :-1]
    z = z + (z_prev - z) * mu
    r, k, v, w_lo, a_lo = jnp.split(z, _split_points(RWKV_SIZES), axis=-1)
    w_log = -jax.nn.softplus(-(w0 + jnp.tanh(w_lo) @ w2)) - 0.5
    decay = jnp.exp(-jnp.exp(w_log.astype(f32)))
    a = jax.nn.sigmoid(a0 + a_lo @ a2)

    def heads(t):
        return t.reshape(bsz, s, RWKV_HEADS, RWKV_HEAD).astype(f32)

    kk = heads(k * k_k)
    kk = kk / jnp.maximum(jnp.sqrt(jnp.sum(kk * kk, axis=-1, keepdims=True)), 1e-12)
    k = k * (1.0 + (a - 1.0) * k_a)
    r_h, k_h, v_h, w_h, a_h = heads(r), heads(k), heads(v), heads(decay), heads(a)
    seq_in = tuple(jnp.moveaxis(t, 1, 0) for t in (r_h, w_h, k_h, v_h, -kk, kk * a_h))

    def step(state, inp):
        r_t, w_t, k_t, v_t, a_t, b_t = inp
        sa = jnp.einsum('bhvk,bhk->bhv', state, a_t)
        state = (state * w_t[:, :, None, :] + sa[..., None] * b_t[:, :, None, :]
                 + v_t[..., None] * k_t[:, :, None, :])
        return state, jnp.einsum('bhvk,bhk->bhv', state, r_t)

    state0 = jnp.zeros((bsz, RWKV_HEADS, RWKV_HEAD, RWKV_HEAD), f32)
    _, o = lax.scan(step, state0, seq_in)
    o = jnp.moveaxis(o, 0, 1)
    mean = jnp.mean(o, axis=-1, keepdims=True)
    var = jnp.mean(jnp.square(o - mean), axis=-1, keepdims=True)
    o = ((o - mean) * lax.rsqrt(var + RWKV_GN_EPS)).reshape(bsz, s, W_GROUP)
    o = o * ln_g.astype(f32) + ln_b.astype(f32)
    bonus = jnp.sum(r_h * k_h * r_k.astype(f32), axis=-1, keepdims=True) * v_h
    return (o + bonus.reshape(bsz, s, W_GROUP)).astype(z.dtype)


def diff_branch(q, k, v, lq1, lk1, lq2, lk2, subln_g, lam_init, cos, sin):
    f32 = jnp.float32
    bsz, s, _ = q.shape
    q = apply_rope(q.reshape(bsz, s, 2 * DIFF_HEADS, DIFF_QK), cos, sin)
    k = apply_rope(k.reshape(bsz, s, 2 * DIFF_HEADS, DIFF_QK), cos, sin)
    q = q.reshape(bsz, s, DIFF_HEADS, 2, DIFF_QK)
    k = k.reshape(bsz, s, DIFF_HEADS, 2, DIFF_QK)
    v = v.reshape(bsz, s, DIFF_HEADS, DIFF_V)
    lam = (jnp.exp(jnp.sum(lq1.astype(f32) * lk1.astype(f32)))
           - jnp.exp(jnp.sum(lq2.astype(f32) * lk2.astype(f32))) + lam_init)
    o = causal_diff_attention(q, k, v, lam, DIFF_QK ** -0.5)
    o = rms_norm(o, subln_g, eps=DIFF_SUBLN_EPS) * (1.0 - lam_init)
    return o.reshape(bsz, s, W_GROUP)


def setup_inputs(seed: int = 0) -> dict:
    key = jax.random.key(seed)
    ks = jax.random.split(key, 40)
    L = DEPTH

    def nrm(k, shape, scale):
        return jax.random.normal(k, shape, jnp.float32) * scale

    x = jax.random.normal(ks[0], (BATCH, SEQ, D_MODEL), jnp.float32)
    offset = jax.random.randint(ks[1], (BATCH, 1), 0, 1024, dtype=jnp.int32)
    positions = offset + jnp.arange(SEQ, dtype=jnp.int32)[None, :]
    s5_im0 = math.pi * jnp.arange(S5_STATE, dtype=jnp.float32)
    return {
        'x': x,
        'positions': positions,
        'norm_g': 1.0 + nrm(ks[2], (L, D_MODEL), 0.02),
        'w_in': nrm(ks[3], (L, D_MODEL, N_IN), D_MODEL ** -0.5),
        'w_out': nrm(ks[4], (L, D_MIX, D_MODEL), D_MIX ** -0.5),
        'mla_q_norm_g': 1.0 + nrm(ks[5], (L, MLA_Q_LORA), 0.02),
        'mla_kv_norm_g': 1.0 + nrm(ks[6], (L, MLA_KV_LORA), 0.02),
        'mla_w_uq': nrm(ks[7], (L, MLA_Q_LORA, MLA_HEADS * (MLA_NOPE + MLA_ROPE)), MLA_Q_LORA ** -0.5),
        'mla_w_ukv': nrm(ks[8], (L, MLA_KV_LORA, MLA_HEADS * (MLA_NOPE + MLA_V)), MLA_KV_LORA ** -0.5),
        's5_a_re': -0.5 + nrm(ks[9], (L, S5_GROUPS, S5_STATE), 0.01),
        's5_a_im': s5_im0 + nrm(ks[10], (L, S5_GROUPS, S5_STATE), 0.01),
        's5_log_dt': jax.random.uniform(ks[11], (L, S5_GROUPS), jnp.float32,
                                        math.log(S5_DT_MIN), math.log(S5_DT_MAX)),
        's5_b_re': nrm(ks[12], (L, S5_GROUPS, S5_STATE, S5_GROUP), (2 * S5_GROUP) ** -0.5),
        's5_b_im': nrm(ks[13], (L, S5_GROUPS, S5_STATE, S5_GROUP), (2 * S5_GROUP) ** -0.5),
        's5_c_re': nrm(ks[14], (L, S5_GROUPS, S5_GROUP, S5_STATE), (2 * S5_STATE) ** -0.5),
        's5_c_im': nrm(ks[15], (L, S5_GROUPS, S5_GROUP, S5_STATE), (2 * S5_STATE) ** -0.5),
        's5_d': nrm(ks[16], (L, W_GROUP), 1.0),
        's5_w_glu': nrm(ks[17], (L, W_GROUP, W_GROUP), W_GROUP ** -0.5),
        's5_b_glu': nrm(ks[18], (L, W_GROUP), 0.01),
        'rwkv_mu': jax.random.uniform(ks[19], (L, N_RWKV), jnp.float32),
        'rwkv_w0': jnp.linspace(-6.0, -1.0, W_GROUP, dtype=jnp.float32)[None, :] + nrm(ks[20], (L, W_GROUP), 0.1),
        'rwkv_w2': nrm(ks[21], (L, RWKV_DECAY_LORA, W_GROUP), 0.5 * RWKV_DECAY_LORA ** -0.5),
        'rwkv_a0': nrm(ks[22], (L, W_GROUP), 0.1),
        'rwkv_a2': nrm(ks[23], (L, RWKV_ICL_LORA, W_GROUP), 0.5 * RWKV_ICL_LORA ** -0.5),
        'rwkv_k_k': 0.85 + nrm(ks[24], (L, W_GROUP), 0.05),
        'rwkv_k_a': 1.0 + nrm(ks[25], (L, W_GROUP), 0.05),
        'rwkv_r_k': nrm(ks[26], (L, RWKV_HEADS, RWKV_HEAD), 0.1),
        'rwkv_ln_g': 1.0 + nrm(ks[27], (L, W_GROUP), 0.02),
        'rwkv_ln_b': nrm(ks[28], (L, W_GROUP), 0.01),
        'diff_lq1': nrm(ks[29], (L, DIFF_QK), 0.1),
        'diff_lk1': nrm(ks[30], (L, DIFF_QK), 0.1),
        'diff_lq2': nrm(ks[31], (L, DIFF_QK), 0.1),
        'diff_lk2': nrm(ks[32], (L, DIFF_QK), 0.1),
        'diff_subln_g': 1.0 + nrm(ks[33], (L, DIFF_V), 0.02),
        'final_norm_g': 1.0 + nrm(ks[34], (D_MODEL,), 0.02),
    }


def reference(x, positions, norm_g, w_in, w_out, mla_q_norm_g, mla_kv_norm_g, mla_w_uq, mla_w_ukv,
              s5_a_re, s5_a_im, s5_log_dt, s5_b_re, s5_b_im, s5_c_re, s5_c_im, s5_d, s5_w_glu, s5_b_glu,
              rwkv_mu, rwkv_w0, rwkv_w2, rwkv_a0, rwkv_a2, rwkv_k_k, rwkv_k_a, rwkv_r_k, rwkv_ln_g, rwkv_ln_b,
              diff_lq1, diff_lk1, diff_lq2, diff_lk2, diff_subln_g, final_norm_g):
    cos_a, sin_a = rope_tables(positions, MLA_ROPE)
    cos_d, sin_d = rope_tables(positions, DIFF_ROT)
    points = _split_points(PROJ_SIZES)
    for l in range(DEPTH):
        h = rms_norm(x, norm_g[l])
        proj = h @ w_in[l]
        c_q, c_kv, k_rope, u_s5, z_rwkv, q_d, k_d, v_d, gate = jnp.split(proj, points, axis=-1)
        y_a = mla_branch(c_q, c_kv, k_rope, mla_q_norm_g[l], mla_kv_norm_g[l],
                         mla_w_uq[l], mla_w_ukv[l], cos_a, sin_a)
        y_b = s5_branch(u_s5, s5_a_re[l], s5_a_im[l], s5_log_dt[l], s5_b_re[l], s5_b_im[l],
                        s5_c_re[l], s5_c_im[l], s5_d[l], s5_w_glu[l], s5_b_glu[l])
        y_c = rwkv7_branch(z_rwkv, rwkv_mu[l], rwkv_w0[l], rwkv_w2[l], rwkv_a0[l], rwkv_a2[l],
                           rwkv_k_k[l], rwkv_k_a[l], rwkv_r_k[l], rwkv_ln_g[l], rwkv_ln_b[l])
        lam_init = 0.8 - 0.6 * math.exp(-0.3 * l)
        y_d = diff_branch(q_d, k_d, v_d, diff_lq1[l], diff_lk1[l], diff_lq2[l], diff_lk2[l],
                          diff_subln_g[l], lam_init, cos_d, sin_d)
        mixed = jnp.concatenate([y_a, y_b, y_c, y_d], axis=-1) * jax.nn.silu(gate)
        x = x + (mixed @ w_out[l]).astype(x.dtype)
    return rms_norm(x, final_norm_g)
```

```python
import functools
import math

import jax
import jax.numpy as jnp
import numpy as np
from jax import lax
from jax.experimental import pallas as pl
from jax.experimental.pallas import tpu as pltpu

F32 = jnp.float32
BF16 = jnp.bfloat16

D_MODEL = 2048
W_GROUP = 512
ROPE_THETA = 500000.0
NORM_EPS = 1e-6
MLA_HEADS, MLA_NOPE, MLA_ROPE, MLA_V = 4, 128, 64, 128
MLA_Q_LORA, MLA_KV_LORA = 512, 256
S5_GROUP, S5_GROUPS, S5_STATE = 16, 32, 64
S5_WIDTH = S5_GROUPS * S5_STATE
RWKV_HEAD, RWKV_HEADS = 64, 8
RWKV_LORA = 64
RWKV_GN_EPS = 64e-5
DIFF_HEADS, DIFF_QK, DIFF_V, DIFF_ROT = 4, 64, 128, 16
DIFF_SUBLN_EPS = 1e-5

LANES = 128
SUBLANES = 8
VMEM_LIMIT_BYTES = 56 * 1024 * 1024

COL_GATE = 0
COL_CQ = 2048
COL_US5 = 2560
COL_QD = 3072
COL_KD = 3584
COL_VD = 4096
COL_R = 4608
COL_K = 5120
COL_V = 5632
COL_CKV = 6144
COL_KROPE = 6400
COL_LORA = 6528
N_PROJ = 6656

TM_INPROJ = 1024
TN_INPROJ = 512
TM_OUTPROJ = 256
TM_PREP = 512
TQ = 512
TK = 512
S5_BLOCK = 256
S5_LANE_CHUNK = 512
RWKV_BLOCK = 256
RWKV_CHUNK = 64
NEG_BIG = -1e30


def _cparams(*sem):
    return pltpu.CompilerParams(dimension_semantics=sem, vmem_limit_bytes=VMEM_LIMIT_BYTES)


def _dot(a, b):
    return jnp.dot(a.astype(BF16), b.astype(BF16), preferred_element_type=F32)


def _dot_nt(a, b):
    return lax.dot_general(a.astype(BF16), b.astype(BF16), (((1,), (1,)), ((), ())), preferred_element_type=F32)


def _dot_split(x, w):
    hi = x.astype(BF16)
    lo = (x - hi.astype(F32)).astype(BF16)
    return jnp.dot(hi, w, preferred_element_type=F32) + jnp.dot(lo, w, preferred_element_type=F32)


def _dot_split3(w, x):
    hi = x.astype(BF16)
    r1 = x - hi.astype(F32)
    mid = r1.astype(BF16)
    lo = (r1 - mid.astype(F32)).astype(BF16)
    return (jnp.dot(w, hi, preferred_element_type=F32) + jnp.dot(w, mid, preferred_element_type=F32)
            + jnp.dot(w, lo, preferred_element_type=F32))


def _sigmoid(x):
    return 1.0 / (1.0 + jnp.exp(-x))


def _inproj_kernel(x_ref, g_ref, w_ref, o_ref, h_ref):
    @pl.when(pl.program_id(1) == 0)
    def _():
        x = x_ref[...]
        y = x * lax.rsqrt(jnp.mean(x * x, axis=-1, keepdims=True) + NORM_EPS)
        h_ref[...] = (y * g_ref[...]).astype(BF16)

    o_ref[...] = jnp.dot(h_ref[...], w_ref[...], preferred_element_type=F32)


def _inproj(x, g, w):
    t, d = x.shape
    n = w.shape[1]
    return pl.pallas_call(
        _inproj_kernel,
        grid=(t // TM_INPROJ, n // TN_INPROJ),
        in_specs=[pl.BlockSpec((TM_INPROJ, d), lambda i, j: (i, 0)),
                  pl.BlockSpec((1, d), lambda i, j: (0, 0)),
                  pl.BlockSpec((d, TN_INPROJ), lambda i, j: (0, j))],
        out_specs=pl.BlockSpec((TM_INPROJ, TN_INPROJ), lambda i, j: (i, j)),
        out_shape=jax.ShapeDtypeStruct((t, n), F32),
        scratch_shapes=[pltpu.VMEM((TM_INPROJ, d), BF16)],
        compiler_params=_cparams("parallel", "arbitrary"),
        name="inproj",
    )(x, g, w)


def _outproj_kernel(ya_ref, yb_ref, yc_ref, yd_ref, gate_ref, x_ref, w_ref, fg_ref, o_ref, *, final):
    y = jnp.concatenate([ya_ref[...], yb_ref[...], yc_ref[...], yd_ref[...]], axis=-1)
    gate = gate_ref[...]
    mixed = y * (gate * _sigmoid(gate))
    xn = x_ref[...] + _dot(mixed, w_ref[...])
    if final:
        xn = xn * lax.rsqrt(jnp.mean(xn * xn, axis=-1, keepdims=True) + NORM_EPS) * fg_ref[...]
    o_ref[...] = xn


def _outproj(ya, yb, yc, yd, proj, x, w, fg, final):
    t, d = x.shape
    tm = TM_OUTPROJ
    yspec = pl.BlockSpec((tm, W_GROUP), lambda i: (i, 0))
    return pl.pallas_call(
        functools.partial(_outproj_kernel, final=final),
        grid=(t // tm,),
        in_specs=[yspec, yspec, yspec, yspec,
                  pl.BlockSpec((tm, d), lambda i: (i, COL_GATE // D_MODEL)),
                  pl.BlockSpec((tm, d), lambda i: (i, 0)),
                  pl.BlockSpec((d, d), lambda i: (0, 0)),
                  pl.BlockSpec((1, d), lambda i: (0, 0))],
        out_specs=pl.BlockSpec((tm, d), lambda i: (i, 0)),
        out_shape=jax.ShapeDtypeStruct((t, d), F32),
        compiler_params=_cparams("parallel"),
        name="outproj_final" if final else "outproj",
    )(ya, yb, yc, yd, proj, x, w, fg)


def _rope128(x, c, sa, sb, half):
    return x * c + pltpu.roll(x, LANES - half, axis=1) * sa + pltpu.roll(x, half, axis=1) * sb


def _mla_prep_kernel(cq_ref, ckv_ref, kr_ref, qg_ref, kvg_ref, wq_ref, wkv_ref, c_ref, sa_ref, sb_ref,
                     q_ref, k_ref, v_ref):
    scale = (MLA_NOPE + MLA_ROPE) ** -0.5
    half = MLA_ROPE // 2
    c, sa, sb = c_ref[...], sa_ref[...], sb_ref[...]
    cq = cq_ref[...]
    hq = cq * lax.rsqrt(jnp.mean(cq * cq, axis=-1, keepdims=True) + NORM_EPS) * qg_ref[...]
    q = _dot(hq, wq_ref[...])
    ckv = ckv_ref[...]
    hkv = ckv * lax.rsqrt(jnp.mean(ckv * ckv, axis=-1, keepdims=True) + NORM_EPS) * kvg_ref[...]
    kv = _dot(hkv, wkv_ref[...])
    kpe = _rope128(kr_ref[...], c, sa, sb, half).astype(BF16)
    for h in range(MLA_HEADS):
        base = 2 * LANES * h
        q_ref[:, base:base + LANES] = (q[:, base:base + LANES] * scale).astype(BF16)
        qpe = _rope128(q[:, base + LANES:base + 2 * LANES], c, sa, sb, half) * scale
        q_ref[:, base + LANES:base + 2 * LANES] = qpe.astype(BF16)
        k_ref[:, base:base + LANES] = kv[:, LANES * h:LANES * (h + 1)].astype(BF16)
        k_ref[:, base + LANES:base + 2 * LANES] = kpe
    v_ref[...] = kv[:, MLA_HEADS * MLA_NOPE:].astype(BF16)


def _mla_prep(proj, qg, kvg, wq, wkv, c, sa, sb):
    t = proj.shape[0]
    tm = TM_PREP
    tab = pl.BlockSpec((tm, LANES), lambda i: (i, 0))
    full = lambda shape: pl.BlockSpec(shape, lambda i: (0, 0))
    return pl.pallas_call(
        _mla_prep_kernel,
        grid=(t // tm,),
        in_specs=[pl.BlockSpec((tm, MLA_Q_LORA), lambda i: (i, COL_CQ // MLA_Q_LORA)),
                  pl.BlockSpec((tm, MLA_KV_LORA), lambda i: (i, COL_CKV // MLA_KV_LORA)),
                  pl.BlockSpec((tm, LANES), lambda i: (i, COL_KROPE // LANES)),
                  full((1, MLA_Q_LORA)), full((1, MLA_KV_LORA)),
                  full(wq.shape), full(wkv.shape), tab, tab, tab],
        out_specs=[pl.BlockSpec((tm, MLA_HEADS * 2 * LANES), lambda i: (i, 0)),
                   pl.BlockSpec((tm, MLA_HEADS * 2 * LANES), lambda i: (i, 0)),
                   pl.BlockSpec((tm, MLA_HEADS * MLA_V), lambda i: (i, 0))],
        out_shape=[jax.ShapeDtypeStruct((t, MLA_HEADS * 2 * LANES), BF16),
                   jax.ShapeDtypeStruct((t, MLA_HEADS * 2 * LANES), BF16),
                   jax.ShapeDtypeStruct((t, MLA_HEADS * MLA_V), BF16)],
        compiler_params=_cparams("parallel"),
        name="mla_prep",
    )(proj, proj, proj, qg, kvg, wq, wkv, c, sa, sb)


def _softmax_step(s, v, m_ref, l_ref, acc_ref):
    m_prev = m_ref[...]
    m_new = jnp.maximum(m_prev, jnp.max(s, axis=-1, keepdims=True))
    alpha = jnp.exp(m_prev - m_new)
    p = jnp.exp(s - m_new)
    l_ref[...] = alpha * l_ref[...] + jnp.sum(p, axis=-1, keepdims=True)
    acc_ref[...] = alpha * acc_ref[...] + jnp.dot(p.astype(BF16), v, preferred_element_type=F32)
    m_ref[...] = m_new


def _causal_mask(s):
    row = lax.broadcasted_iota(jnp.int32, s.shape, 0)
    col = lax.broadcasted_iota(jnp.int32, s.shape, 1)
    return jnp.where(row >= col, s, NEG_BIG)


def _flash_kernel(q_ref, k_ref, v_ref, o_ref, m_ref, l_ref, acc_ref):
    i, j = pl.program_id(2), pl.program_id(3)

    @pl.when(j == 0)
    def _():
        m_ref[...] = jnp.full(m_ref.shape, NEG_BIG, F32)
        l_ref[...] = jnp.zeros(l_ref.shape, F32)
        acc_ref[...] = jnp.zeros(acc_ref.shape, F32)

    @pl.when(j < i)
    def _():
        s = lax.dot_general(q_ref[...], k_ref[...], (((1,), (1,)), ((), ())), preferred_element_type=F32)
        _softmax_step(s, v_ref[...], m_ref, l_ref, acc_ref)

    @pl.when(j == i)
    def _():
        s = lax.dot_general(q_ref[...], k_ref[...], (((1,), (1,)), ((), ())), preferred_element_type=F32)
        _softmax_step(_causal_mask(s), v_ref[...], m_ref, l_ref, acc_ref)
        o_ref[...] = acc_ref[...] / l_ref[...]


def _flash(q, k, v, batch, heads, dqk, dv):
    t = q.shape[0]
    nq = t // batch // TQ
    return pl.pallas_call(
        _flash_kernel,
        grid=(batch, heads, nq, nq),
        in_specs=[pl.BlockSpec((TQ, dqk), lambda b, h, i, j: (b * nq + i, h)),
                  pl.BlockSpec((TK, dqk), lambda b, h, i, j: (b * nq + jnp.minimum(i, j), h)),
                  pl.BlockSpec((TK, dv), lambda b, h, i, j: (b * nq + jnp.minimum(i, j), h))],
        out_specs=pl.BlockSpec((TQ, dv), lambda b, h, i, j: (b * nq + i, h)),
        out_shape=jax.ShapeDtypeStruct((t, heads * dv), F32),
        scratch_shapes=[pltpu.VMEM((TQ, 1), F32), pltpu.VMEM((TQ, 1), F32), pltpu.VMEM((TQ, dv), F32)],
        compiler_params=_cparams("parallel", "parallel", "parallel", "arbitrary"),
        name="mla_flash",
    )(q, k, v)


def _diff_prep_kernel(q_ref, k_ref, v_ref, cq_ref, saq_ref, sbq_ref, ck_ref, sak_ref, sbk_ref,
                      q1_ref, q2_ref, ko_ref, vo_ref):
    half = DIFF_ROT // 2
    lane = lax.broadcasted_iota(jnp.int32, (1, LANES), 1)
    m0 = (lane < DIFF_QK).astype(F32)
    m1 = 1.0 - m0
    for h in range(DIFF_HEADS):
        sl = slice(LANES * h, LANES * (h + 1))
        q = _rope128(q_ref[:, sl], cq_ref[...], saq_ref[...], sbq_ref[...], half)
        q1_ref[:, sl] = (q * m0).astype(BF16)
        q2_ref[:, sl] = (q * m1).astype(BF16)
        ko_ref[:, sl] = _rope128(k_ref[:, sl], ck_ref[...], sak_ref[...], sbk_ref[...], half).astype(BF16)
    vo_ref[...] = v_ref[...].astype(BF16)


def _diff_prep(proj, cq, saq, sbq, ck, sak, sbk):
    t = proj.shape[0]
    tm = TM_PREP
    tab = pl.BlockSpec((tm, LANES), lambda i: (i, 0))
    seg = lambda col: pl.BlockSpec((tm, W_GROUP), lambda i: (i, col // W_GROUP))
    out = pl.BlockSpec((tm, W_GROUP), lambda i: (i, 0))
    shp = jax.ShapeDtypeStruct((t, W_GROUP), BF16)
    return pl.pallas_call(
        _diff_prep_kernel,
        grid=(t // tm,),
        in_specs=[seg(COL_QD), seg(COL_KD), seg(COL_VD), tab, tab, tab, tab, tab, tab],
        out_specs=[out, out, out, out],
        out_shape=[shp, shp, shp, shp],
        compiler_params=_cparams("parallel"),
        name="diff_prep",
    )(proj, proj, proj, cq, saq, sbq, ck, sak, sbk)


def _diff_flash_kernel(q1_ref, q2_ref, k_ref, v_ref, lam_ref, g_ref, o_ref,
                       m1_ref, l1_ref, a1_ref, m2_ref, l2_ref, a2_ref, *, out_scale):
    i, j = pl.program_id(2), pl.program_id(3)

    @pl.when(j == 0)
    def _():
        for m_ref, l_ref, a_ref in ((m1_ref, l1_ref, a1_ref), (m2_ref, l2_ref, a2_ref)):
            m_ref[...] = jnp.full(m_ref.shape, NEG_BIG, F32)
            l_ref[...] = jnp.zeros(l_ref.shape, F32)
            a_ref[...] = jnp.zeros(a_ref.shape, F32)

    def scores(q_ref):
        return lax.dot_general(q_ref[...], k_ref[...], (((1,), (1,)), ((), ())), preferred_element_type=F32)

    @pl.when(j < i)
    def _():
        _softmax_step(scores(q1_ref), v_ref[...], m1_ref, l1_ref, a1_ref)
        _softmax_step(scores(q2_ref), v_ref[...], m2_ref, l2_ref, a2_ref)

    @pl.when(j == i)
    def _():
        _softmax_step(_causal_mask(scores(q1_ref)), v_ref[...], m1_ref, l1_ref, a1_ref)
        _softmax_step(_causal_mask(scores(q2_ref)), v_ref[...], m2_ref, l2_ref, a2_ref)
        o = a1_ref[...] / l1_ref[...] - lam_ref[...] * (a2_ref[...] / l2_ref[...])
        o = o * lax.rsqrt(jnp.mean(o * o, axis=-1, keepdims=True) + DIFF_SUBLN_EPS) * g_ref[...]
        o_ref[...] = o * out_scale


def _diff_flash(q1, q2, k, v, lam, g, batch, out_scale):
    t = q1.shape[0]
    nq = t // batch // TQ
    qspec = pl.BlockSpec((TQ, LANES), lambda b, h, i, j: (b * nq + i, h))
    kspec = pl.BlockSpec((TK, LANES), lambda b, h, i, j: (b * nq + jnp.minimum(i, j), h))
    row = pl.BlockSpec((1, LANES), lambda b, h, i, j: (0, 0))
    stat = pltpu.VMEM((TQ, 1), F32)
    acc = pltpu.VMEM((TQ, LANES), F32)
    return pl.pallas_call(
        functools.partial(_diff_flash_kernel, out_scale=out_scale),
        grid=(batch, DIFF_HEADS, nq, nq),
        in_specs=[qspec, qspec, kspec, kspec, row, row],
        out_specs=qspec,
        out_shape=jax.ShapeDtypeStruct((t, W_GROUP), F32),
        scratch_shapes=[stat, stat, acc, stat, stat, acc],
        compiler_params=_cparams("parallel", "parallel", "parallel", "arbitrary"),
        name="diff_flash",
    )(q1, q2, k, v, lam, g)


def _s5_kernel(u_ref, bmat_ref, cre_ref, cim_ref, pwr_ref, pwi_ref, car_ref, cai_ref, d_ref, wglu_ref, bglu_ref,
               o_ref, hr_ref, hi_ref, sr_ref, si_ref):
    @pl.when(pl.program_id(1) == 0)
    def _():
        sr_ref[...] = jnp.zeros(sr_ref.shape, F32)
        si_ref[...] = jnp.zeros(si_ref.shape, F32)

    u = u_ref[...]
    ub = u.astype(BF16)
    hr_ref[...] = jnp.dot(ub, bmat_ref[:, :S5_WIDTH], preferred_element_type=F32)
    hi_ref[...] = jnp.dot(ub, bmat_ref[:, S5_WIDTH:], preferred_element_type=F32)

    n_tiles = S5_BLOCK // SUBLANES
    for c in range(S5_WIDTH // S5_LANE_CHUNK):
        ls = slice(c * S5_LANE_CHUNK, (c + 1) * S5_LANE_CHUNK)
        pw = [(pwr_ref[s, :, ls], pwi_ref[s, :, ls]) for s in range(3)]
        car, cai = car_ref[:, ls], cai_ref[:, ls]

        def tile(t, carry):
            cr, ci = carry
            rows = pl.ds(pl.multiple_of(t * SUBLANES, SUBLANES), SUBLANES)
            xr, xi = hr_ref[rows, ls], hi_ref[rows, ls]
            for s, (pr, pi) in enumerate(pw):
                rr = pltpu.roll(xr, 1 << s, axis=0)
                ri = pltpu.roll(xi, 1 << s, axis=0)
                xr, xi = xr + (pr * rr - pi * ri), xi + (pr * ri + pi * rr)
            xr, xi = xr + (car * cr - cai * ci), xi + (car * ci + cai * cr)
            hr_ref[rows, ls] = xr
            hi_ref[rows, ls] = xi
            return xr[SUBLANES - 1:SUBLANES, :], xi[SUBLANES - 1:SUBLANES, :]

        cr, ci = lax.fori_loop(0, n_tiles, tile, (sr_ref[:, ls], si_ref[:, ls]), unroll=2)
        sr_ref[:, ls] = cr
        si_ref[:, ls] = ci

    y = (jnp.dot(hr_ref[...].astype(BF16), cre_ref[...], preferred_element_type=F32)
         + jnp.dot(hi_ref[...].astype(BF16), cim_ref[...], preferred_element_type=F32))
    y = y + d_ref[...] * u
    g = 0.5 * y * (1.0 + jnp.tanh(math.sqrt(2.0 / math.pi) * (y + 0.044715 * (y * y * y))))
    o_ref[...] = g * _sigmoid(_dot(g, wglu_ref[...]) + bglu_ref[...])


def _s5(proj, batch, bmat, cre, cim, pwr, pwi, car, cai, d, wglu, bglu):
    t = proj.shape[0]
    nb = t // batch // S5_BLOCK
    full2 = lambda a: pl.BlockSpec(a.shape, lambda b, i: (0,) * a.ndim)
    ucol = COL_US5 // W_GROUP
    return pl.pallas_call(
        _s5_kernel,
        grid=(batch, nb),
        in_specs=[pl.BlockSpec((S5_BLOCK, W_GROUP), lambda b, i: (b * nb + i, ucol))]
        + [full2(a) for a in (bmat, cre, cim, pwr, pwi, car, cai, d, wglu, bglu)],
        out_specs=pl.BlockSpec((S5_BLOCK, W_GROUP), lambda b, i: (b * nb + i, 0)),
        out_shape=jax.ShapeDtypeStruct((t, W_GROUP), F32),
        scratch_shapes=[pltpu.VMEM((S5_BLOCK, S5_WIDTH), F32), pltpu.VMEM((S5_BLOCK, S5_WIDTH), F32),
                        pltpu.VMEM((1, S5_WIDTH), F32), pltpu.VMEM((1, S5_WIDTH), F32)],
        compiler_params=_cparams("parallel", "arbitrary"),
        name="s5",
    )(proj, bmat, cre, cim, pwr, pwi, car, cai, d, wglu, bglu)


def _rwkv_kernel(r_ref, k_ref, v_ref, lo_ref, mu_r_ref, mu_k_ref, mu_v_ref, mu_lo_ref, w0_ref, w2_ref, a0_ref,
                 a2_ref, kk_ref, ka_ref, rk_ref, lng_ref, lnb_ref, ones_ref,
                 o_ref,
                 st_ref, pr_ref, pk_ref, pv_ref, plo_ref,
                 rt_ref, at_ref, bt_ref, kt_ref, bg_ref, kg_ref, vv_ref, ge_ref, oo_ref):
    tb, lc = RWKV_BLOCK, RWKV_CHUNK

    @pl.when(pl.program_id(1) == 0)
    def _():
        st_ref[...] = jnp.zeros(st_ref.shape, F32)
        pr_ref[...] = jnp.zeros(pr_ref.shape, F32)
        pk_ref[...] = jnp.zeros(pk_ref.shape, F32)
        pv_ref[...] = jnp.zeros(pv_ref.shape, F32)
        plo_ref[...] = jnp.zeros(plo_ref.shape, F32)

    def shifted(z_ref, prev_ref, mu_ref):
        z = z_ref[...]
        first = lax.broadcasted_iota(jnp.int32, z.shape, 0) == 0
        z_prev = jnp.where(first, prev_ref[...], pltpu.roll(z, 1, axis=0))
        prev_ref[...] = z[tb - 1:tb, :]
        return z + (z_prev - z) * mu_ref[...]

    r = shifted(r_ref, pr_ref, mu_r_ref)
    k = shifted(k_ref, pk_ref, mu_k_ref)
    v = shifted(v_ref, pv_ref, mu_v_ref)
    lo = shifted(lo_ref, plo_ref, mu_lo_ref)

    ones = ones_ref[...]
    wx = -(w0_ref[...] + _dot(jnp.tanh(lo), w2_ref[...]))
    w_log = -(jnp.maximum(wx, 0.0) + jnp.log(1.0 + jnp.exp(-jnp.abs(wx)))) - 0.5
    lw = -jnp.exp(w_log)
    a = _sigmoid(a0_ref[...] + _dot(lo, a2_ref[...]))
    kk = k * kk_ref[...]
    kk = kk / jnp.maximum(jnp.sqrt(_dot_split(kk * kk, ones)), 1e-12)
    k2 = k * (1.0 + (a - 1.0) * ka_ref[...])
    aa = -kk
    bb = kk * a

    row = lax.broadcasted_iota(jnp.int32, (lc, lc), 0)
    col = lax.broadcasted_iota(jnp.int32, (lc, lc), 1)
    tri = (row >= col).astype(BF16)
    for c in range(tb // lc):
        rs = slice(c * lc, (c + 1) * lc)
        lwc = lw[rs]
        cum = _dot_split3(tri, lwc)
        cum_last = cum[lc - 1:lc, :]
        e_neg = jnp.exp(-cum)
        e_end = jnp.exp(cum_last - cum)
        rt_ref[rs, :] = r[rs] * jnp.exp(cum)
        at_ref[rs, :] = aa[rs] * jnp.exp(cum - lwc)
        bt_ref[rs, :] = bb[rs] * e_neg
        kt_ref[rs, :] = k2[rs] * e_neg
        bg_ref[rs, :] = bb[rs] * e_end
        kg_ref[rs, :] = k2[rs] * e_end
        ge_ref[c] = jnp.broadcast_to(jnp.exp(cum_last), (SUBLANES, W_GROUP))
    vv_ref[...] = v

    lane = lax.broadcasted_iota(jnp.int32, (1, LANES), 1)
    m0 = (lane < RWKV_HEAD).astype(F32)
    m1 = 1.0 - m0
    r2 = lax.broadcasted_iota(jnp.int32, (2 * lc, 2 * lc), 0)
    c2 = lax.broadcasted_iota(jnp.int32, (2 * lc, 2 * lc), 1)
    same = jnp.right_shift(r2, 6) == jnp.right_shift(c2, 6)
    assert lc == 64
    strict = jnp.where(same & (r2 > c2), 1.0, 0.0).astype(F32)
    incl = jnp.where(same & (r2 >= c2), 1.0, 0.0).astype(F32)
    eye = jnp.where(r2 == c2, 1.0, 0.0).astype(F32)
    n_double = int(math.log2(lc)) - 1

    def stack(x):
        return jnp.concatenate([x * m0, x * m1], axis=0)

    def chunk(c, _):
        rs = pl.ds(pl.multiple_of(c * lc, lc), lc)
        for p in range(RWKV_HEADS // 2):
            ls = slice(LANES * p, LANES * (p + 1))
            s_vk = st_ref[p]
            ar = jnp.concatenate([stack(at_ref[rs, ls]), stack(rt_ref[rs, ls])], axis=0).astype(BF16)
            bk = jnp.concatenate([stack(bt_ref[rs, ls]), stack(kt_ref[rs, ls])], axis=0).astype(BF16)
            vs = stack(vv_ref[rs, ls]).astype(BF16)
            big = lax.dot_general(ar, bk, (((1,), (1,)), ((), ())), preferred_element_type=F32)
            ars = lax.dot_general(ar, s_vk.astype(BF16), (((1,), (1,)), ((), ())), preferred_element_type=F32)
            a_ab = big[:2 * lc, :2 * lc] * strict
            a_ak = big[:2 * lc, 2 * lc:] * strict
            a_rb = big[2 * lc:, :2 * lc] * incl
            a_rk = big[2 * lc:, 2 * lc:] * incl
            rhs = ars[:2 * lc] + jnp.dot(a_ak.astype(BF16), vs, preferred_element_type=F32)
            nil = a_ab
            inv = eye + nil
            for _i in range(n_double):
                nb = nil.astype(BF16)
                nil = jnp.dot(nb, nb, preferred_element_type=F32)
                inv = inv + _dot(inv, nil)
            us = _dot(inv, rhs)
            uv = jnp.concatenate([us.astype(BF16), vs], axis=0)
            os_ = ars[2 * lc:] + jnp.dot(jnp.concatenate([a_rb, a_rk], axis=1).astype(BF16), uv,
                                         preferred_element_type=F32)
            oo_ref[rs, ls] = os_[:lc] + os_[lc:]
            bkg = jnp.concatenate([stack(bg_ref[rs, ls]), stack(kg_ref[rs, ls])], axis=0).astype(BF16)
            upd = lax.dot_general(uv, bkg, (((0,), (0,)), ((), ())), preferred_element_type=F32)
            st_ref[p] = s_vk * ge_ref[c, 0:1, ls] + upd
        return 0

    lax.fori_loop(0, tb // lc, chunk, 0)

    o = oo_ref[...]
    inv_n = 1.0 / RWKV_HEAD
    mean = _dot_split(o, ones) * inv_n
    oc = o - mean
    var = _dot_split(oc * oc, ones) * inv_n
    o = oc * lax.rsqrt(var + RWKV_GN_EPS) * lng_ref[...] + lnb_ref[...]
    bonus = _dot_split(r * k2 * rk_ref[...], ones) * v
    o_ref[...] = o + bonus


def _rwkv(proj, batch, mu_r, mu_k, mu_v, mu_lo, w0, w2p, a0, a2p, k_k, k_a, r_k, ln_g, ln_b, ones):
    t = proj.shape[0]
    tb = RWKV_BLOCK
    nb = t // batch // tb
    seg = lambda col: pl.BlockSpec((tb, W_GROUP), lambda b, i: (b * nb + i, col // W_GROUP))
    full2 = lambda a: pl.BlockSpec(a.shape, lambda b, i: (0,) * a.ndim)
    params = (mu_r, mu_k, mu_v, mu_lo, w0, w2p, a0, a2p, k_k, k_a, r_k, ln_g, ln_b, ones)
    buf = pltpu.VMEM((tb, W_GROUP), F32)
    return pl.pallas_call(
        _rwkv_kernel,
        grid=(batch, nb),
        in_specs=[seg(COL_R), seg(COL_K), seg(COL_V),
                  pl.BlockSpec((tb, LANES), lambda b, i: (b * nb + i, COL_LORA // LANES))]
        + [full2(a) for a in params],
        out_specs=pl.BlockSpec((tb, W_GROUP), lambda b, i: (b * nb + i, 0)),
        out_shape=jax.ShapeDtypeStruct((t, W_GROUP), F32),
        scratch_shapes=[pltpu.VMEM((RWKV_HEADS // 2, LANES, LANES), F32),
                        pltpu.VMEM((1, W_GROUP), F32), pltpu.VMEM((1, W_GROUP), F32),
                        pltpu.VMEM((1, W_GROUP), F32), pltpu.VMEM((1, LANES), F32),
                        buf, buf, buf, buf, buf, buf, buf,
                        pltpu.VMEM((tb // RWKV_CHUNK, SUBLANES, W_GROUP), F32), buf],
        compiler_params=_cparams("parallel", "arbitrary"),
        name="rwkv",
    )(proj, proj, proj, proj, *params)


def _rope_tables(positions, rot, period, scale):
    half = rot // 2
    inv = ROPE_THETA ** (-jnp.arange(0, rot, 2, dtype=F32) / rot)
    ang = positions.reshape(-1).astype(F32)[:, None] * inv
    cos, sin = jnp.cos(ang), jnp.sin(ang)
    t = ang.shape[0]
    passthrough = period - rot
    c = jnp.concatenate([cos, cos, jnp.ones((t, passthrough), F32)], axis=1)
    sa = jnp.concatenate([-sin, jnp.zeros((t, half + passthrough), F32)], axis=1)
    sb = jnp.concatenate([jnp.zeros((t, half), F32), sin, jnp.zeros((t, passthrough), F32)], axis=1)
    reps = LANES // period
    return tuple(jnp.tile(a, (1, reps)) * scale for a in (c, sa, sb))


def _permute_w_in(w_in):
    d = w_in.shape[0]
    sizes = (MLA_Q_LORA, MLA_KV_LORA, MLA_ROPE, W_GROUP, W_GROUP, W_GROUP, W_GROUP, RWKV_LORA, RWKV_LORA,
             W_GROUP, W_GROUP, W_GROUP, D_MODEL)
    pts = [int(p) for p in np.cumsum(sizes)[:-1]]
    cq, ckv, krope, us5, r, k, v, wlo, alo, qd, kd, vd, gate = jnp.split(w_in, pts, axis=1)
    out = jnp.concatenate([gate, cq, us5, qd, kd, vd, r, k, v, ckv, krope, jnp.zeros((d, LANES - MLA_ROPE), w_in.dtype),
                           wlo, alo], axis=1)
    assert out.shape[1] == N_PROJ
    return out.astype(BF16)


def _s5_tables(a_re, a_im, log_dt, b_re, b_im, c_re, c_im):
    lr = jnp.minimum(a_re, -1e-4)
    li = a_im
    dt = jnp.exp(log_dt)[:, None]
    mag = jnp.exp(dt * lr)
    ab_re, ab_im = mag * jnp.cos(dt * li), mag * jnp.sin(dt * li)
    den = lr * lr + li * li
    nr, ni = ab_re - 1.0, ab_im
    f_re = (nr * lr + ni * li) / den
    f_im = (ni * lr - nr * li) / den
    bb_re = f_re[..., None] * b_re - f_im[..., None] * b_im
    bb_im = f_re[..., None] * b_im + f_im[..., None] * b_re
    eye = jnp.eye(S5_GROUPS, dtype=F32)
    blk_in = lambda m: jnp.einsum('gpc,gh->gchp', m, eye).reshape(W_GROUP, S5_WIDTH)
    blk_out = lambda m: jnp.einsum('gcp,gh->gphc', m, eye).reshape(S5_WIDTH, W_GROUP)
    bmat = jnp.concatenate([blk_in(bb_re), blk_in(bb_im)], axis=1).astype(BF16)
    cre = blk_out(c_re).astype(BF16)
    cim = blk_out(-c_im).astype(BF16)

    def power(n):
        m = jnp.exp(n * dt * lr)
        return m * jnp.cos(n * dt * li), m * jnp.sin(n * dt * li)

    rows = jnp.arange(SUBLANES, dtype=F32)[:, None, None]
    pwr, pwi = [], []
    for s in (1, 2, 4):
        pr, pi = power(jnp.full((1, 1, 1), float(s), F32))
        keep = (rows >= s).astype(F32)
        pwr.append((keep * pr).reshape(SUBLANES, S5_WIDTH))
        pwi.append((keep * pi).reshape(SUBLANES, S5_WIDTH))
    car, cai = power(rows + 1.0)
    return (bmat, cre, cim, jnp.stack(pwr), jnp.stack(pwi),
            car.reshape(SUBLANES, S5_WIDTH), cai.reshape(SUBLANES, S5_WIDTH))


def kernel(x, positions, norm_g, w_in, w_out, mla_q_norm_g, mla_kv_norm_g, mla_w_uq, mla_w_ukv, s5_a_re, s5_a_im, s5_log_dt, s5_b_re, s5_b_im, s5_c_re, s5_c_im, s5_d, s5_w_glu, s5_b_glu, rwkv_mu, rwkv_w0, rwkv_w2, rwkv_a0, rwkv_a2, rwkv_k_k, rwkv_k_a, rwkv_r_k, rwkv_ln_g, rwkv_ln_b, diff_lq1, diff_lk1, diff_lq2, diff_lk2, diff_subln_g, final_norm_g):
    batch, seq, d = x.shape
    depth = w_in.shape[0]
    t = batch * seq
    assert d == D_MODEL and seq % TQ == 0 and t % TM_INPROJ == 0
    assert seq % S5_BLOCK == 0 and seq % RWKV_BLOCK == 0

    mla_scale = 1.0
    ca, saa, sba = _rope_tables(positions, MLA_ROPE, LANES, mla_scale)
    zero_hi = (jnp.arange(LANES) < MLA_ROPE).astype(F32)[None, :]
    ca = ca * zero_hi
    cdq, sadq, sbdq = _rope_tables(positions, DIFF_ROT, DIFF_QK, DIFF_QK ** -0.5)
    cdk, sadk, sbdk = _rope_tables(positions, DIFF_ROT, DIFF_QK, 1.0)

    head_ones = jnp.kron(jnp.eye(RWKV_HEADS, dtype=F32), jnp.ones((RWKV_HEAD, RWKV_HEAD), F32)).astype(BF16)
    row = lambda a: a.reshape(1, -1).astype(F32)

    xf = x.reshape(t, d)
    for l in range(depth):
        proj = _inproj(xf, row(norm_g[l]), _permute_w_in(w_in[l]))

        wq = mla_w_uq[l].reshape(MLA_Q_LORA, MLA_HEADS, MLA_NOPE + MLA_ROPE)
        wq = jnp.pad(wq, ((0, 0), (0, 0), (0, 2 * LANES - MLA_NOPE - MLA_ROPE))).reshape(MLA_Q_LORA, -1).astype(BF16)
        wkv = mla_w_ukv[l].reshape(MLA_KV_LORA, MLA_HEADS, MLA_NOPE + MLA_V)
        wkv = jnp.concatenate([wkv[:, :, :MLA_NOPE].reshape(MLA_KV_LORA, -1),
                               wkv[:, :, MLA_NOPE:].reshape(MLA_KV_LORA, -1)], axis=1).astype(BF16)
        qa, ka, va = _mla_prep(proj, row(mla_q_norm_g[l]), row(mla_kv_norm_g[l]), wq, wkv, ca, saa, sba)
        y_a = _flash(qa, ka, va, batch, MLA_HEADS, 2 * LANES, MLA_V)

        tabs = _s5_tables(s5_a_re[l], s5_a_im[l], s5_log_dt[l], s5_b_re[l], s5_b_im[l], s5_c_re[l], s5_c_im[l])
        y_b = _s5(proj, batch, *tabs, row(s5_d[l]), s5_w_glu[l].astype(BF16), row(s5_b_glu[l]))

        mu = rwkv_mu[l]
        zpad = jnp.zeros((RWKV_LORA, W_GROUP), F32)
        w2p = jnp.concatenate([rwkv_w2[l], zpad], axis=0).astype(BF16)
        a2p = jnp.concatenate([zpad, rwkv_a2[l]], axis=0).astype(BF16)
        y_c = _rwkv(proj, batch, row(mu[:W_GROUP]), row(mu[W_GROUP:2 * W_GROUP]), row(mu[2 * W_GROUP:3 * W_GROUP]),
                    row(mu[3 * W_GROUP:]), row(rwkv_w0[l]), w2p, row(rwkv_a0[l]), a2p, row(rwkv_k_k[l]),
                    row(rwkv_k_a[l]), row(rwkv_r_k[l]), row(rwkv_ln_g[l]), row(rwkv_ln_b[l]), head_ones)

        lam_init = 0.8 - 0.6 * math.exp(-0.3 * l)
        lam = (jnp.exp(jnp.sum(diff_lq1[l] * diff_lk1[l])) - jnp.exp(jnp.sum(diff_lq2[l] * diff_lk2[l])) + lam_init)
        q1, q2, kd, vd = _diff_prep(proj, cdq, sadq, sbdq, cdk, sadk, sbdk)
        y_d = _diff_flash(q1, q2, kd, vd, jnp.full((1, LANES), lam, F32), row(diff_subln_g[l]), batch,
                          1.0 - lam_init)

        xf = _outproj(y_a, y_b, y_c, y_d, proj, xf, w_out[l].astype(BF16), row(final_norm_g), l == depth - 1)
    return xf.reshape(batch, seq, d)
```

```python
import functools
import math

import jax
import jax.numpy as jnp
import numpy as np
from jax import lax
from jax.experimental import pallas as pl
from jax.experimental.pallas import tpu as pltpu

F32 = jnp.float32
BF16 = jnp.bfloat16

D_MODEL = 2048
W_GROUP = 512
ROPE_THETA = 500000.0
NORM_EPS = 1e-6
MLA_HEADS, MLA_NOPE, MLA_ROPE, MLA_V = 4, 128, 64, 128
MLA_Q_LORA, MLA_KV_LORA = 512, 256
S5_GROUP, S5_GROUPS, S5_STATE = 16, 32, 64
S5_WIDTH = S5_GROUPS * S5_STATE
RWKV_HEAD, RWKV_HEADS = 64, 8
RWKV_LORA = 64
RWKV_GN_EPS = 64e-5
DIFF_HEADS, DIFF_QK, DIFF_V, DIFF_ROT = 4, 64, 128, 16
DIFF_SUBLN_EPS = 1e-5

LANES = 128
SUBLANES = 8
VMEM_LIMIT_BYTES = 56 * 1024 * 1024

COL_GATE = 0
COL_CQ = 2048
COL_US5 = 2560
COL_QD = 3072
COL_KD = 3584
COL_VD = 4096
COL_R = 4608
COL_K = 5120
COL_V = 5632
COL_CKV = 6144
COL_KROPE = 6400
COL_LORA = 6528
N_PROJ = 6656

TM_INPROJ = 1024
TN_INPROJ = 512
TM_OUTPROJ = 256
TM_PREP = 512
TQ = 512
KS = 256
S5_BLOCK = 256
S5_LANE_CHUNK = 512
RWKV_BLOCK = 256
RWKV_CHUNK = 64
NEG_BIG = -1e30
LOG2E = math.log2(math.e)


def _cparams(*sem):
    return pltpu.CompilerParams(dimension_semantics=sem, vmem_limit_bytes=VMEM_LIMIT_BYTES)


def _dot(a, b):
    return jnp.dot(a.astype(BF16), b.astype(BF16), preferred_element_type=F32)


def _dot_nt(a, b):
    return lax.dot_general(a.astype(BF16), b.astype(BF16), (((1,), (1,)), ((), ())), preferred_element_type=F32)


def _dot_split(x, w):
    hi = x.astype(BF16)
    lo = (x - hi.astype(F32)).astype(BF16)
    return jnp.dot(hi, w, preferred_element_type=F32) + jnp.dot(lo, w, preferred_element_type=F32)


def _dot_split3(w, x):
    hi = x.astype(BF16)
    r1 = x - hi.astype(F32)
    mid = r1.astype(BF16)
    lo = (r1 - mid.astype(F32)).astype(BF16)
    return (jnp.dot(w, hi, preferred_element_type=F32) + jnp.dot(w, mid, preferred_element_type=F32)
            + jnp.dot(w, lo, preferred_element_type=F32))


def _sigmoid(x):
    return 1.0 / (1.0 + jnp.exp(-x))


def _inproj_kernel(x_ref, g_ref, w_ref, o_ref, h_ref):
    @pl.when(pl.program_id(1) == 0)
    def _():
        x = x_ref[...]
        y = x * lax.rsqrt(jnp.mean(x * x, axis=-1, keepdims=True) + NORM_EPS)
        h_ref[...] = (y * g_ref[...]).astype(BF16)

    o_ref[...] = jnp.dot(h_ref[...], w_ref[...], preferred_element_type=F32)


def _inproj(x, g, w):
    t, d = x.shape
    n = w.shape[1]
    return pl.pallas_call(
        _inproj_kernel,
        grid=(t // TM_INPROJ, n // TN_INPROJ),
        in_specs=[pl.BlockSpec((TM_INPROJ, d), lambda i, j: (i, 0)),
                  pl.BlockSpec((1, d), lambda i, j: (0, 0)),
                  pl.BlockSpec((d, TN_INPROJ), lambda i, j: (0, j))],
        out_specs=pl.BlockSpec((TM_INPROJ, TN_INPROJ), lambda i, j: (i, j)),
        out_shape=jax.ShapeDtypeStruct((t, n), F32),
        scratch_shapes=[pltpu.VMEM((TM_INPROJ, d), BF16)],
        compiler_params=_cparams("parallel", "arbitrary"),
        name="inproj",
    )(x, g, w)


def _outproj_kernel(ya_ref, yb_ref, yc_ref, yd_ref, gate_ref, x_ref, w_ref, fg_ref, o_ref, *, final):
    y = jnp.concatenate([ya_ref[...], yb_ref[...], yc_ref[...], yd_ref[...]], axis=-1)
    gate = gate_ref[...]
    mixed = y * (gate * _sigmoid(gate))
    xn = x_ref[...] + _dot(mixed, w_ref[...])
    if final:
        xn = xn * lax.rsqrt(jnp.mean(xn * xn, axis=-1, keepdims=True) + NORM_EPS) * fg_ref[...]
    o_ref[...] = xn


def _outproj(ya, yb, yc, yd, proj, x, w, fg, final):
    t, d = x.shape
    tm = TM_OUTPROJ
    yspec = pl.BlockSpec((tm, W_GROUP), lambda i: (i, 0))
    return pl.pallas_call(
        functools.partial(_outproj_kernel, final=final),
        grid=(t // tm,),
        in_specs=[yspec, yspec, yspec, yspec,
                  pl.BlockSpec((tm, d), lambda i: (i, COL_GATE // D_MODEL)),
                  pl.BlockSpec((tm, d), lambda i: (i, 0)),
                  pl.BlockSpec((d, d), lambda i: (0, 0)),
                  pl.BlockSpec((1, d), lambda i: (0, 0))],
        out_specs=pl.BlockSpec((tm, d), lambda i: (i, 0)),
        out_shape=jax.ShapeDtypeStruct((t, d), F32),
        compiler_params=_cparams("parallel"),
        name="outproj_final" if final else "outproj",
    )(ya, yb, yc, yd, proj, x, w, fg)


def _rope128(x, c, sa, sb, half):
    return x * c + pltpu.roll(x, LANES - half, axis=1) * sa + pltpu.roll(x, half, axis=1) * sb


def _store_vt_tiles(v, vt_ref):
    for n in range(v.shape[0] // KS):
        vt_ref[n] = v[n * KS:(n + 1) * KS, :].T.astype(BF16)


def _mla_prep_kernel(cq_ref, ckv_ref, kr_ref, qg_ref, kvg_ref, wq_ref, wkv_ref, c_ref, sa_ref, sb_ref,
                     q_ref, k_ref, vt_ref):
    scale = (MLA_NOPE + MLA_ROPE) ** -0.5 * LOG2E
    half = MLA_ROPE // 2
    c, sa, sb = c_ref[...], sa_ref[...], sb_ref[...]
    cq = cq_ref[...]
    hq = cq * lax.rsqrt(jnp.mean(cq * cq, axis=-1, keepdims=True) + NORM_EPS) * qg_ref[...]
    q = _dot(hq, wq_ref[...])
    ckv = ckv_ref[...]
    hkv = ckv * lax.rsqrt(jnp.mean(ckv * ckv, axis=-1, keepdims=True) + NORM_EPS) * kvg_ref[...]
    kv = _dot(hkv, wkv_ref[...])
    kpe = _rope128(kr_ref[...], c, sa, sb, half).astype(BF16)
    for h in range(MLA_HEADS):
        base = 2 * LANES * h
        q_ref[:, base:base + LANES] = (q[:, base:base + LANES] * scale).astype(BF16)
        qpe = _rope128(q[:, base + LANES:base + 2 * LANES], c, sa, sb, half) * scale
        q_ref[:, base + LANES:base + 2 * LANES] = qpe.astype(BF16)
        k_ref[:, base:base + LANES] = kv[:, LANES * h:LANES * (h + 1)].astype(BF16)
        k_ref[:, base + LANES:base + 2 * LANES] = kpe
    _store_vt_tiles(kv[:, MLA_HEADS * MLA_NOPE:], vt_ref)


def _mla_prep(proj, qg, kvg, wq, wkv, c, sa, sb):
    t = proj.shape[0]
    tm = TM_PREP
    tab = pl.BlockSpec((tm, LANES), lambda i: (i, 0))
    full = lambda shape: pl.BlockSpec(shape, lambda i: (0, 0))
    return pl.pallas_call(
        _mla_prep_kernel,
        grid=(t // tm,),
        in_specs=[pl.BlockSpec((tm, MLA_Q_LORA), lambda i: (i, COL_CQ // MLA_Q_LORA)),
                  pl.BlockSpec((tm, MLA_KV_LORA), lambda i: (i, COL_CKV // MLA_KV_LORA)),
                  pl.BlockSpec((tm, LANES), lambda i: (i, COL_KROPE // LANES)),
                  full((1, MLA_Q_LORA)), full((1, MLA_KV_LORA)),
                  full(wq.shape), full(wkv.shape), tab, tab, tab],
        out_specs=[pl.BlockSpec((tm, MLA_HEADS * 2 * LANES), lambda i: (i, 0)),
                   pl.BlockSpec((tm, MLA_HEADS * 2 * LANES), lambda i: (i, 0)),
                   pl.BlockSpec((tm // KS, MLA_HEADS * MLA_V, KS), lambda i: (i, 0, 0))],
        out_shape=[jax.ShapeDtypeStruct((t, MLA_HEADS * 2 * LANES), BF16),
                   jax.ShapeDtypeStruct((t, MLA_HEADS * 2 * LANES), BF16),
                   jax.ShapeDtypeStruct((t // KS, MLA_HEADS * MLA_V, KS), BF16)],
        compiler_params=_cparams("parallel"),
        name="mla_prep",
    )(proj, proj, proj, qg, kvg, wq, wkv, c, sa, sb)


def _attn_kernel(*refs, n_maps, out_scale):
    q_refs = refs[:n_maps]
    k_ref, vt_ref = refs[n_maps:n_maps + 2]
    if n_maps == 2:
        lam_ref, g_ref = refs[n_maps + 2:n_maps + 4]
        o_ref = refs[n_maps + 4]
        scratch = refs[n_maps + 5:]
    else:
        o_ref = refs[n_maps + 2]
        scratch = refs[n_maps + 3:]
    states = [scratch[3 * a:3 * a + 3] for a in range(n_maps)]
    i = pl.program_id(2)
    sub_per_q = TQ // KS

    for m_ref, l_ref, acc_ref in states:
        m_ref[...] = jnp.full(m_ref.shape, NEG_BIG, F32)
        l_ref[...] = jnp.zeros(l_ref.shape, F32)
        acc_ref[...] = jnp.zeros(acc_ref.shape, F32)
    qs = [q_ref[...] for q_ref in q_refs]

    def block(jq, diagonal):
        tiles = [jq * sub_per_q + sub for sub in range(sub_per_q)]
        ks = [k_ref[pl.ds(pl.multiple_of(jb * KS, KS), KS), :] for jb in tiles]
        scores = [[lax.dot_general(k, q, (((1,), (1,)), ((), ())), preferred_element_type=F32) for q in qs]
                  for k in ks]
        for sub, jb in enumerate(tiles):
            vt = vt_ref[jb]
            for s, (m_ref, l_ref, acc_ref) in zip(scores[sub], states):
                if diagonal:
                    key = lax.broadcasted_iota(jnp.int32, s.shape, 0) + sub * KS
                    qry = lax.broadcasted_iota(jnp.int32, s.shape, 1)
                    s = jnp.where(key <= qry, s, NEG_BIG)
                m_prev = m_ref[...]
                m_new = jnp.maximum(m_prev, jnp.max(s, axis=0, keepdims=True))
                alpha = jnp.exp2(m_prev - m_new)
                p = jnp.exp2(s - m_new)
                l_ref[...] = alpha * l_ref[...] + jnp.sum(p.reshape(KS // SUBLANES, SUBLANES, TQ), axis=0)
                acc_ref[...] = alpha * acc_ref[...] + jnp.dot(vt, p.astype(BF16), preferred_element_type=F32)
                m_ref[...] = m_new

    def body(jq, carry):
        block(jq, False)
        return carry

    lax.fori_loop(0, i, body, 0)
    block(i, True)

    outs = []
    for m_ref, l_ref, acc_ref in states:
        inv_l = 1.0 / jnp.sum(l_ref[...], axis=0, keepdims=True)
        outs.append(acc_ref[...] * inv_l)
    if n_maps == 2:
        ot = outs[0] - lam_ref[0, 0] * outs[1]
        ot = ot * lax.rsqrt(jnp.mean(ot * ot, axis=0, keepdims=True) + DIFF_SUBLN_EPS)
        o_ref[...] = ot.T * (g_ref[...] * out_scale)
    else:
        o_ref[...] = outs[0].T


def _attention(qs, k, vt, batch, heads, dqk, dv, extra=(), out_scale=1.0, name="attn"):
    n_maps = len(qs)
    t = k.shape[0]
    seq = t // batch
    nq = seq // TQ
    qspec = pl.BlockSpec((TQ, dqk), lambda b, h, i: (b * nq + i, h))
    in_specs = [qspec] * n_maps + [pl.BlockSpec((seq, dqk), lambda b, h, i: (b, h)),
                                   pl.BlockSpec((seq // KS, dv, KS), lambda b, h, i: (b, h, 0))]
    if n_maps == 2:
        in_specs += [pl.BlockSpec(memory_space=pltpu.SMEM), pl.BlockSpec((1, dv), lambda b, h, i: (0, 0))]
    state = [pltpu.VMEM((1, TQ), F32), pltpu.VMEM((SUBLANES, TQ), F32), pltpu.VMEM((dv, TQ), F32)]
    return pl.pallas_call(
        functools.partial(_attn_kernel, n_maps=n_maps, out_scale=out_scale),
        grid=(batch, heads, nq),
        in_specs=in_specs,
        out_specs=pl.BlockSpec((TQ, dv), lambda b, h, i: (b * nq + i, h)),
        out_shape=jax.ShapeDtypeStruct((t, heads * dv), F32),
        scratch_shapes=state * n_maps,
        compiler_params=_cparams("parallel", "parallel", "arbitrary"),
        name=name,
    )(*qs, k, vt, *extra)


def _diff_prep_kernel(q_ref, k_ref, v_ref, cq_ref, saq_ref, sbq_ref, ck_ref, sak_ref, sbk_ref,
                      q1_ref, q2_ref, ko_ref, vt_ref):
    half = DIFF_ROT // 2
    lane = lax.broadcasted_iota(jnp.int32, (1, LANES), 1)
    m0 = (lane < DIFF_QK).astype(F32)
    m1 = 1.0 - m0
    for h in range(DIFF_HEADS):
        sl = slice(LANES * h, LANES * (h + 1))
        q = _rope128(q_ref[:, sl], cq_ref[...], saq_ref[...], sbq_ref[...], half)
        q1_ref[:, sl] = (q * m0).astype(BF16)
        q2_ref[:, sl] = (q * m1).astype(BF16)
        ko_ref[:, sl] = _rope128(k_ref[:, sl], ck_ref[...], sak_ref[...], sbk_ref[...], half).astype(BF16)
    _store_vt_tiles(v_ref[...], vt_ref)


def _diff_prep(proj, cq, saq, sbq, ck, sak, sbk):
    t = proj.shape[0]
    tm = TM_PREP
    tab = pl.BlockSpec((tm, LANES), lambda i: (i, 0))
    seg = lambda col: pl.BlockSpec((tm, W_GROUP), lambda i: (i, col // W_GROUP))
    out = pl.BlockSpec((tm, W_GROUP), lambda i: (i, 0))
    shp = jax.ShapeDtypeStruct((t, W_GROUP), BF16)
    return pl.pallas_call(
        _diff_prep_kernel,
        grid=(t // tm,),
        in_specs=[seg(COL_QD), seg(COL_KD), seg(COL_VD), tab, tab, tab, tab, tab, tab],
        out_specs=[out, out, out, pl.BlockSpec((tm // KS, W_GROUP, KS), lambda i: (i, 0, 0))],
        out_shape=[shp, shp, shp, jax.ShapeDtypeStruct((t // KS, W_GROUP, KS), BF16)],
        compiler_params=_cparams("parallel"),
        name="diff_prep",
    )(proj, proj, proj, cq, saq, sbq, ck, sak, sbk)


def _s5_kernel(u_ref, bmat_ref, cre_ref, cim_ref, pwr_ref, pwi_ref, car_ref, cai_ref, d_ref, wglu_ref, bglu_ref,
               o_ref, hr_ref, hi_ref, sr_ref, si_ref):
    @pl.when(pl.program_id(1) == 0)
    def _():
        sr_ref[...] = jnp.zeros(sr_ref.shape, F32)
        si_ref[...] = jnp.zeros(si_ref.shape, F32)

    u = u_ref[...]
    ub = u.astype(BF16)
    hr_ref[...] = jnp.dot(ub, bmat_ref[:, :S5_WIDTH], preferred_element_type=F32)
    hi_ref[...] = jnp.dot(ub, bmat_ref[:, S5_WIDTH:], preferred_element_type=F32)

    n_tiles = S5_BLOCK // SUBLANES
    for c in range(S5_WIDTH // S5_LANE_CHUNK):
        ls = slice(c * S5_LANE_CHUNK, (c + 1) * S5_LANE_CHUNK)
        pw = [(pwr_ref[s, :, ls], pwi_ref[s, :, ls]) for s in range(3)]
        car, cai = car_ref[:, ls], cai_ref[:, ls]

        def tile(t, carry):
            cr, ci = carry
            rows = pl.ds(pl.multiple_of(t * SUBLANES, SUBLANES), SUBLANES)
            xr, xi = hr_ref[rows, ls], hi_ref[rows, ls]
            for s, (pr, pi) in enumerate(pw):
                rr = pltpu.roll(xr, 1 << s, axis=0)
                ri = pltpu.roll(xi, 1 << s, axis=0)
                xr, xi = xr + (pr * rr - pi * ri), xi + (pr * ri + pi * rr)
            xr, xi = xr + (car * cr - cai * ci), xi + (car * ci + cai * cr)
            hr_ref[rows, ls] = xr
            hi_ref[rows, ls] = xi
            return xr[SUBLANES - 1:SUBLANES, :], xi[SUBLANES - 1:SUBLANES, :]

        cr, ci = lax.fori_loop(0, n_tiles, tile, (sr_ref[:, ls], si_ref[:, ls]), unroll=2)
        sr_ref[:, ls] = cr
        si_ref[:, ls] = ci

    y = (jnp.dot(hr_ref[...].astype(BF16), cre_ref[...], preferred_element_type=F32)
         + jnp.dot(hi_ref[...].astype(BF16), cim_ref[...], preferred_element_type=F32))
    y = y + d_ref[...] * u
    g = 0.5 * y * (1.0 + jnp.tanh(math.sqrt(2.0 / math.pi) * (y + 0.044715 * (y * y * y))))
    o_ref[...] = g * _sigmoid(_dot(g, wglu_ref[...]) + bglu_ref[...])


def _s5(proj, batch, bmat, cre, cim, pwr, pwi, car, cai, d, wglu, bglu):
    t = proj.shape[0]
    nb = t // batch // S5_BLOCK
    full2 = lambda a: pl.BlockSpec(a.shape, lambda b, i: (0,) * a.ndim)
    ucol = COL_US5 // W_GROUP
    return pl.pallas_call(
        _s5_kernel,
        grid=(batch, nb),
        in_specs=[pl.BlockSpec((S5_BLOCK, W_GROUP), lambda b, i: (b * nb + i, ucol))]
        + [full2(a) for a in (bmat, cre, cim, pwr, pwi, car, cai, d, wglu, bglu)],
        out_specs=pl.BlockSpec((S5_BLOCK, W_GROUP), lambda b, i: (b * nb + i, 0)),
        out_shape=jax.ShapeDtypeStruct((t, W_GROUP), F32),
        scratch_shapes=[pltpu.VMEM((S5_BLOCK, S5_WIDTH), F32), pltpu.VMEM((S5_BLOCK, S5_WIDTH), F32),
                        pltpu.VMEM((1, S5_WIDTH), F32), pltpu.VMEM((1, S5_WIDTH), F32)],
        compiler_params=_cparams("parallel", "arbitrary"),
        name="s5",
    )(proj, bmat, cre, cim, pwr, pwi, car, cai, d, wglu, bglu)


def _rwkv_kernel(r_ref, k_ref, v_ref, lo_ref, mu_r_ref, mu_k_ref, mu_v_ref, mu_lo_ref, w0_ref, w2_ref, a0_ref,
                 a2_ref, kk_ref, ka_ref, rk_ref, lng_ref, lnb_ref, ones_ref,
                 o_ref,
                 st_ref, pr_ref, pk_ref, pv_ref, plo_ref,
                 rt_ref, at_ref, bt_ref, kt_ref, bg_ref, kg_ref, vv_ref, ge_ref, oo_ref,
                 qc_ref, ec_ref, mc_ref, dc_ref):
    tb, lc = RWKV_BLOCK, RWKV_CHUNK

    @pl.when(pl.program_id(1) == 0)
    def _():
        st_ref[...] = jnp.zeros(st_ref.shape, F32)
        pr_ref[...] = jnp.zeros(pr_ref.shape, F32)
        pk_ref[...] = jnp.zeros(pk_ref.shape, F32)
        pv_ref[...] = jnp.zeros(pv_ref.shape, F32)
        plo_ref[...] = jnp.zeros(plo_ref.shape, F32)

    def shifted(z_ref, prev_ref, mu_ref):
        z = z_ref[...]
        first = lax.broadcasted_iota(jnp.int32, z.shape, 0) == 0
        z_prev = jnp.where(first, prev_ref[...], pltpu.roll(z, 1, axis=0))
        prev_ref[...] = z[tb - 1:tb, :]
        return z + (z_prev - z) * mu_ref[...]

    r = shifted(r_ref, pr_ref, mu_r_ref)
    k = shifted(k_ref, pk_ref, mu_k_ref)
    v = shifted(v_ref, pv_ref, mu_v_ref)
    lo = shifted(lo_ref, plo_ref, mu_lo_ref)

    ones = ones_ref[...]
    wx = -(w0_ref[...] + _dot(jnp.tanh(lo), w2_ref[...]))
    w_log = -(jnp.maximum(wx, 0.0) + jnp.log(1.0 + jnp.exp(-jnp.abs(wx)))) - 0.5
    lw = -jnp.exp(w_log)
    a = _sigmoid(a0_ref[...] + _dot(lo, a2_ref[...]))
    kk = k * kk_ref[...]
    kk = kk / jnp.maximum(jnp.sqrt(_dot_split(kk * kk, ones)), 1e-12)
    k2 = k * (1.0 + (a - 1.0) * ka_ref[...])
    aa = -kk
    bb = kk * a

    row = lax.broadcasted_iota(jnp.int32, (lc, lc), 0)
    col = lax.broadcasted_iota(jnp.int32, (lc, lc), 1)
    tri = (row >= col).astype(BF16)
    for c in range(tb // lc):
        rs = slice(c * lc, (c + 1) * lc)
        lwc = lw[rs]
        cum = _dot_split3(tri, lwc)
        cum_last = cum[lc - 1:lc, :]
        e_neg = jnp.exp(-cum)
        e_end = jnp.exp(cum_last - cum)
        rt_ref[rs, :] = r[rs] * jnp.exp(cum)
        at_ref[rs, :] = aa[rs] * jnp.exp(cum - lwc)
        bt_ref[rs, :] = bb[rs] * e_neg
        kt_ref[rs, :] = k2[rs] * e_neg
        bg_ref[rs, :] = bb[rs] * e_end
        kg_ref[rs, :] = k2[rs] * e_end
        ge_ref[c] = jnp.broadcast_to(jnp.exp(cum_last), (SUBLANES, W_GROUP))
    vv_ref[...] = v

    lane = lax.broadcasted_iota(jnp.int32, (1, LANES), 1)
    m0 = (lane < RWKV_HEAD).astype(F32)
    m1 = 1.0 - m0
    r2 = lax.broadcasted_iota(jnp.int32, (2 * lc, 2 * lc), 0)
    c2 = lax.broadcasted_iota(jnp.int32, (2 * lc, 2 * lc), 1)
    same = jnp.right_shift(r2, 6) == jnp.right_shift(c2, 6)
    assert lc == 64
    strict = jnp.where(same & (r2 > c2), 1.0, 0.0).astype(F32)
    incl = jnp.where(same & (r2 >= c2), 1.0, 0.0).astype(F32)
    eye = jnp.where(r2 == c2, 1.0, 0.0).astype(F32)
    n_double = int(math.log2(lc)) - 1

    def stack(x):
        return jnp.concatenate([x * m0, x * m1], axis=0)

    pairs = range(RWKV_HEADS // 2)
    lsl = [slice(LANES * p, LANES * (p + 1)) for p in pairs]
    n_chunks = tb // lc
    dot_nt = lambda x, y: lax.dot_general(x, y, (((1,), (1,)), ((), ())), preferred_element_type=F32)
    dot_tn = lambda x, y: lax.dot_general(x, y, (((0,), (0,)), ((), ())), preferred_element_type=F32)
    dot_nn = lambda x, y: jnp.dot(x, y, preferred_element_type=F32)
    bf = lambda xs: [x.astype(BF16) for x in xs]

    def chunk_terms(c, _):
        rs = pl.ds(pl.multiple_of(c * lc, lc), lc)
        a_s = bf([stack(at_ref[rs, ls]) for ls in lsl])
        r_f = [stack(rt_ref[rs, ls]) for ls in lsl]
        b_s = bf([stack(bt_ref[rs, ls]) for ls in lsl])
        k_s = bf([stack(kt_ref[rs, ls]) for ls in lsl])
        v_s = bf([stack(vv_ref[rs, ls]) for ls in lsl])
        bkg = [jnp.concatenate([stack(bg_ref[rs, ls]), stack(kg_ref[rs, ls])], axis=0).astype(BF16) for ls in lsl]
        big = [dot_nt(jnp.concatenate([a_s[p], r_f[p].astype(BF16)], axis=0),
                      jnp.concatenate([b_s[p], k_s[p]], axis=0)) for p in pairs]
        nil = [big[p][:2 * lc, :2 * lc] * strict for p in pairs]
        a_ak = bf([big[p][:2 * lc, 2 * lc:] * strict for p in pairs])
        a_r = bf([jnp.concatenate([big[p][2 * lc:, :2 * lc] * incl, big[p][2 * lc:, 2 * lc:] * incl], axis=1)
                  for p in pairs])
        akv = [dot_nn(a_ak[p], v_s[p]) for p in pairs]
        inv = [eye + nil[p] for p in pairs]
        for _i in range(n_double):
            nb = bf(nil)
            nil = [dot_nn(nb[p], nb[p]) for p in pairs]
            nb = bf(nil)
            ib = bf(inv)
            inv = [inv[p] + dot_nn(ib[p], nb[p]) for p in pairs]
        ib = bf(inv)
        ta = bf([dot_nn(ib[p], a_s[p]) for p in pairs])
        akv = bf(akv)
        tvv = [jnp.concatenate([dot_nn(ib[p], akv[p]).astype(BF16), v_s[p]], axis=0) for p in pairs]
        qc = [r_f[p] + dot_nn(a_r[p][:, :2 * lc], ta[p]) for p in pairs]
        ec = [dot_nn(a_r[p], tvv[p]) for p in pairs]
        mc = [dot_tn(ta[p], bkg[p][:2 * lc]) for p in pairs]
        dc = [dot_tn(tvv[p], bkg[p]) for p in pairs]
        for p in pairs:
            idx = c * len(pairs) + p
            qc_ref[idx] = qc[p].astype(BF16)
            ec_ref[idx] = ec[p]
            mc_ref[idx] = mc[p].astype(BF16)
            dc_ref[idx] = dc[p]
        return 0

    lax.fori_loop(0, n_chunks, chunk_terms, 0)

    state = [st_ref[p] for p in pairs]
    for c in range(n_chunks):
        sb = bf(state)
        os_ = [dot_nt(qc_ref[c * len(pairs) + p], sb[p]) + ec_ref[c * len(pairs) + p] for p in pairs]
        state = [state[p] * ge_ref[c, 0:1, lsl[p]] + dot_nn(sb[p], mc_ref[c * len(pairs) + p])
                 + dc_ref[c * len(pairs) + p] for p in pairs]
        for p in pairs:
            oo_ref[c * lc:(c + 1) * lc, lsl[p]] = os_[p][:lc] + os_[p][lc:]
    for p in pairs:
        st_ref[p] = state[p]

    o = oo_ref[...]
    inv_n = 1.0 / RWKV_HEAD
    mean = _dot_split(o, ones) * inv_n
    oc = o - mean
    var = _dot_split(oc * oc, ones) * inv_n
    o = oc * lax.rsqrt(var + RWKV_GN_EPS) * lng_ref[...] + lnb_ref[...]
    bonus = _dot_split(r * k2 * rk_ref[...], ones) * v
    o_ref[...] = o + bonus


def _rwkv(proj, batch, mu_r, mu_k, mu_v, mu_lo, w0, w2p, a0, a2p, k_k, k_a, r_k, ln_g, ln_b, ones):
    t = proj.shape[0]
    tb = RWKV_BLOCK
    nb = t // batch // tb
    seg = lambda col: pl.BlockSpec((tb, W_GROUP), lambda b, i: (b * nb + i, col // W_GROUP))
    full2 = lambda a: pl.BlockSpec(a.shape, lambda b, i: (0,) * a.ndim)
    params = (mu_r, mu_k, mu_v, mu_lo, w0, w2p, a0, a2p, k_k, k_a, r_k, ln_g, ln_b, ones)
    buf = pltpu.VMEM((tb, W_GROUP), F32)
    term = (tb // RWKV_CHUNK * (RWKV_HEADS // 2), LANES, LANES)
    return pl.pallas_call(
        _rwkv_kernel,
        grid=(batch, nb),
        in_specs=[seg(COL_R), seg(COL_K), seg(COL_V),
                  pl.BlockSpec((tb, LANES), lambda b, i: (b * nb + i, COL_LORA // LANES))]
        + [full2(a) for a in params],
        out_specs=pl.BlockSpec((tb, W_GROUP), lambda b, i: (b * nb + i, 0)),
        out_shape=jax.ShapeDtypeStruct((t, W_GROUP), F32),
        scratch_shapes=[pltpu.VMEM((RWKV_HEADS // 2, LANES, LANES), F32),
                        pltpu.VMEM((1, W_GROUP), F32), pltpu.VMEM((1, W_GROUP), F32),
                        pltpu.VMEM((1, W_GROUP), F32), pltpu.VMEM((1, LANES), F32),
                        buf, buf, buf, buf, buf, buf, buf,
                        pltpu.VMEM((tb // RWKV_CHUNK, SUBLANES, W_GROUP), F32), buf,
                        pltpu.VMEM(term, BF16), pltpu.VMEM(term, F32), pltpu.VMEM(term, BF16), pltpu.VMEM(term, F32)],
        compiler_params=_cparams("parallel", "arbitrary"),
        name="rwkv",
    )(proj, proj, proj, proj, *params)


def _rope_tables(positions, rot, period, scale):
    half = rot // 2
    inv = ROPE_THETA ** (-jnp.arange(0, rot, 2, dtype=F32) / rot)
    ang = positions.reshape(-1).astype(F32)[:, None] * inv
    cos, sin = jnp.cos(ang), jnp.sin(ang)
    t = ang.shape[0]
    passthrough = period - rot
    c = jnp.concatenate([cos, cos, jnp.ones((t, passthrough), F32)], axis=1)
    sa = jnp.concatenate([-sin, jnp.zeros((t, half + passthrough), F32)], axis=1)
    sb = jnp.concatenate([jnp.zeros((t, half), F32), sin, jnp.zeros((t, passthrough), F32)], axis=1)
    reps = LANES // period
    return tuple(jnp.tile(a, (1, reps)) * scale for a in (c, sa, sb))


def _permute_w_in(w_in):
    d = w_in.shape[0]
    sizes = (MLA_Q_LORA, MLA_KV_LORA, MLA_ROPE, W_GROUP, W_GROUP, W_GROUP, W_GROUP, RWKV_LORA, RWKV_LORA,
             W_GROUP, W_GROUP, W_GROUP, D_MODEL)
    pts = [int(p) for p in np.cumsum(sizes)[:-1]]
    cq, ckv, krope, us5, r, k, v, wlo, alo, qd, kd, vd, gate = jnp.split(w_in, pts, axis=1)
    out = jnp.concatenate([gate, cq, us5, qd, kd, vd, r, k, v, ckv, krope, jnp.zeros((d, LANES - MLA_ROPE), w_in.dtype),
                           wlo, alo], axis=1)
    assert out.shape[1] == N_PROJ
    return out.astype(BF16)


def _s5_tables(a_re, a_im, log_dt, b_re, b_im, c_re, c_im):
    lr = jnp.minimum(a_re, -1e-4)
    li = a_im
    dt = jnp.exp(log_dt)[:, None]
    mag = jnp.exp(dt * lr)
    ab_re, ab_im = mag * jnp.cos(dt * li), mag * jnp.sin(dt * li)
    den = lr * lr + li * li
    nr, ni = ab_re - 1.0, ab_im
    f_re = (nr * lr + ni * li) / den
    f_im = (ni * lr - nr * li) / den
    bb_re = f_re[..., None] * b_re - f_im[..., None] * b_im
    bb_im = f_re[..., None] * b_im + f_im[..., None] * b_re
    eye = jnp.eye(S5_GROUPS, dtype=F32)
    blk_in = lambda m: jnp.einsum('gpc,gh->gchp', m, eye).reshape(W_GROUP, S5_WIDTH)
    blk_out = lambda m: jnp.einsum('gcp,gh->gphc', m, eye).reshape(S5_WIDTH, W_GROUP)
    bmat = jnp.concatenate([blk_in(bb_re), blk_in(bb_im)], axis=1).astype(BF16)
    cre = blk_out(c_re).astype(BF16)
    cim = blk_out(-c_im).astype(BF16)

    def power(n):
        m = jnp.exp(n * dt * lr)
        return m * jnp.cos(n * dt * li), m * jnp.sin(n * dt * li)

    rows = jnp.arange(SUBLANES, dtype=F32)[:, None, None]
    pwr, pwi = [], []
    for s in (1, 2, 4):
        pr, pi = power(jnp.full((1, 1, 1), float(s), F32))
        keep = (rows >= s).astype(F32)
        pwr.append((keep * pr).reshape(SUBLANES, S5_WIDTH))
        pwi.append((keep * pi).reshape(SUBLANES, S5_WIDTH))
    car, cai = power(rows + 1.0)
    return (bmat, cre, cim, jnp.stack(pwr), jnp.stack(pwi),
            car.reshape(SUBLANES, S5_WIDTH), cai.reshape(SUBLANES, S5_WIDTH))


def kernel(x, positions, norm_g, w_in, w_out, mla_q_norm_g, mla_kv_norm_g, mla_w_uq, mla_w_ukv, s5_a_re, s5_a_im, s5_log_dt, s5_b_re, s5_b_im, s5_c_re, s5_c_im, s5_d, s5_w_glu, s5_b_glu, rwkv_mu, rwkv_w0, rwkv_w2, rwkv_a0, rwkv_a2, rwkv_k_k, rwkv_k_a, rwkv_r_k, rwkv_ln_g, rwkv_ln_b, diff_lq1, diff_lk1, diff_lq2, diff_lk2, diff_subln_g, final_norm_g):
    batch, seq, d = x.shape
    depth = w_in.shape[0]
    t = batch * seq
    assert d == D_MODEL and seq % TQ == 0 and t % TM_INPROJ == 0
    assert seq % S5_BLOCK == 0 and seq % RWKV_BLOCK == 0

    mla_scale = 1.0
    ca, saa, sba = _rope_tables(positions, MLA_ROPE, LANES, mla_scale)
    zero_hi = (jnp.arange(LANES) < MLA_ROPE).astype(F32)[None, :]
    ca = ca * zero_hi
    cdq, sadq, sbdq = _rope_tables(positions, DIFF_ROT, DIFF_QK, DIFF_QK ** -0.5 * LOG2E)
    cdk, sadk, sbdk = _rope_tables(positions, DIFF_ROT, DIFF_QK, 1.0)

    head_ones = jnp.kron(jnp.eye(RWKV_HEADS, dtype=F32), jnp.ones((RWKV_HEAD, RWKV_HEAD), F32)).astype(BF16)
    row = lambda a: a.reshape(1, -1).astype(F32)

    xf = x.reshape(t, d)
    for l in range(depth):
        proj = _inproj(xf, row(norm_g[l]), _permute_w_in(w_in[l]))

        wq = mla_w_uq[l].reshape(MLA_Q_LORA, MLA_HEADS, MLA_NOPE + MLA_ROPE)
        wq = jnp.pad(wq, ((0, 0), (0, 0), (0, 2 * LANES - MLA_NOPE - MLA_ROPE))).reshape(MLA_Q_LORA, -1).astype(BF16)
        wkv = mla_w_ukv[l].reshape(MLA_KV_LORA, MLA_HEADS, MLA_NOPE + MLA_V)
        wkv = jnp.concatenate([wkv[:, :, :MLA_NOPE].reshape(MLA_KV_LORA, -1),
                               wkv[:, :, MLA_NOPE:].reshape(MLA_KV_LORA, -1)], axis=1).astype(BF16)
        qa, ka, va = _mla_prep(proj, row(mla_q_norm_g[l]), row(mla_kv_norm_g[l]), wq, wkv, ca, saa, sba)
        y_a = _attention([qa], ka, va, batch, MLA_HEADS, 2 * LANES, MLA_V, name="mla_attn")

        tabs = _s5_tables(s5_a_re[l], s5_a_im[l], s5_log_dt[l], s5_b_re[l], s5_b_im[l], s5_c_re[l], s5_c_im[l])
        y_b = _s5(proj, batch, *tabs, row(s5_d[l]), s5_w_glu[l].astype(BF16), row(s5_b_glu[l]))

        mu = rwkv_mu[l]
        zpad = jnp.zeros((RWKV_LORA, W_GROUP), F32)
        w2p = jnp.concatenate([rwkv_w2[l], zpad], axis=0).astype(BF16)
        a2p = jnp.concatenate([zpad, rwkv_a2[l]], axis=0).astype(BF16)
        y_c = _rwkv(proj, batch, row(mu[:W_GROUP]), row(mu[W_GROUP:2 * W_GROUP]), row(mu[2 * W_GROUP:3 * W_GROUP]),
                    row(mu[3 * W_GROUP:]), row(rwkv_w0[l]), w2p, row(rwkv_a0[l]), a2p, row(rwkv_k_k[l]),
                    row(rwkv_k_a[l]), row(rwkv_r_k[l]), row(rwkv_ln_g[l]), row(rwkv_ln_b[l]), head_ones)

        lam_init = 0.8 - 0.6 * math.exp(-0.3 * l)
        lam = (jnp.exp(jnp.sum(diff_lq1[l] * diff_lk1[l])) - jnp.exp(jnp.sum(diff_lq2[l] * diff_lk2[l])) + lam_init)
        q1, q2, kd, vd = _diff_prep(proj, cdq, sadq, sbdq, cdk, sadk, sbdk)
        y_d = _attention([q1, q2], kd, vd, batch, DIFF_HEADS, LANES, DIFF_V,
                         extra=(lam.reshape(1, 1).astype(F32), row(diff_subln_g[l])),
                         out_scale=1.0 - lam_init, name="diff_attn")

        xf = _outproj(y_a, y_b, y_c, y_d, proj, xf, w_out[l].astype(BF16), row(final_norm_g), l == depth - 1)
    return xf.reshape(batch, seq, d)
```

```python
import functools
import math

import jax
import jax.numpy as jnp
import numpy as np
from jax import lax
from jax.experimental import pallas as pl
from jax.experimental.pallas import tpu as pltpu

F32 = jnp.float32
BF16 = jnp.bfloat16

D_MODEL = 2048
W_GROUP = 512
ROPE_THETA = 500000.0
NORM_EPS = 1e-6
MLA_HEADS, MLA_NOPE, MLA_ROPE, MLA_V = 4, 128, 64, 128
MLA_Q_LORA, MLA_KV_LORA = 512, 256
S5_GROUP, S5_GROUPS, S5_STATE = 16, 32, 64
S5_WIDTH = S5_GROUPS * S5_STATE
RWKV_HEAD, RWKV_HEADS = 64, 8
RWKV_LORA = 64
RWKV_GN_EPS = 64e-5
DIFF_HEADS, DIFF_QK, DIFF_V, DIFF_ROT = 4, 64, 128, 16
DIFF_SUBLN_EPS = 1e-5

LANES = 128
SUBLANES = 8
VMEM_LIMIT_BYTES = 56 * 1024 * 1024

COL_GATE = 0
COL_CQ = 2048
COL_US5 = 2560
COL_QD = 3072
COL_KD = 3584
COL_VD = 4096
COL_R = 4608
COL_K = 5120
COL_V = 5632
COL_CKV = 6144
COL_KROPE = 6400
COL_LORA = 6528
N_PROJ = 6656

TM_INPROJ = 1024
TN_INPROJ = 1664
TM_OUTPROJ = 256
TM_PREP = 512
TQ = 512
KS = 256
S5_BLOCK = 256
S5_LANE_CHUNK = 512
RWKV_BLOCK = 256
RWKV_CHUNK = 64
RWKV_CHUNKS_INTERLEAVED = 2
NEG_BIG = -1e30
LOG2E = math.log2(math.e)


def _cparams(*sem):
    return pltpu.CompilerParams(dimension_semantics=sem, vmem_limit_bytes=VMEM_LIMIT_BYTES)


def _dot(a, b):
    return jnp.dot(a.astype(BF16), b.astype(BF16), preferred_element_type=F32)


def _dot_nt(a, b):
    return lax.dot_general(a.astype(BF16), b.astype(BF16), (((1,), (1,)), ((), ())), preferred_element_type=F32)


def _dot_split(x, w):
    hi = x.astype(BF16)
    lo = (x - hi.astype(F32)).astype(BF16)
    return jnp.dot(hi, w, preferred_element_type=F32) + jnp.dot(lo, w, preferred_element_type=F32)


def _dot_split3(w, x):
    hi = x.astype(BF16)
    r1 = x - hi.astype(F32)
    mid = r1.astype(BF16)
    lo = (r1 - mid.astype(F32)).astype(BF16)
    return (jnp.dot(w, hi, preferred_element_type=F32) + jnp.dot(w, mid, preferred_element_type=F32)
            + jnp.dot(w, lo, preferred_element_type=F32))


def _sigmoid(x):
    return 1.0 / (1.0 + jnp.exp(-x))


def _inproj_kernel(x_ref, g_ref, w_ref, o_ref, h_ref):
    @pl.when(pl.program_id(1) == 0)
    def _():
        x = x_ref[...]
        y = x * lax.rsqrt(jnp.mean(x * x, axis=-1, keepdims=True) + NORM_EPS)
        h_ref[...] = (y * g_ref[...]).astype(BF16)

    o_ref[...] = jnp.dot(h_ref[...], w_ref[...], preferred_element_type=F32)


def _inproj(x, g, w):
    t, d = x.shape
    n = w.shape[1]
    return pl.pallas_call(
        _inproj_kernel,
        grid=(t // TM_INPROJ, n // TN_INPROJ),
        in_specs=[pl.BlockSpec((TM_INPROJ, d), lambda i, j: (i, 0)),
                  pl.BlockSpec((1, d), lambda i, j: (0, 0)),
                  pl.BlockSpec((d, TN_INPROJ), lambda i, j: (0, j))],
        out_specs=pl.BlockSpec((TM_INPROJ, TN_INPROJ), lambda i, j: (i, j)),
        out_shape=jax.ShapeDtypeStruct((t, n), F32),
        scratch_shapes=[pltpu.VMEM((TM_INPROJ, d), BF16)],
        compiler_params=_cparams("parallel", "arbitrary"),
        name="inproj",
    )(x, g, w)


def _outproj_kernel(ya_ref, yb_ref, yc_ref, yd_ref, gate_ref, x_ref, w_ref, fg_ref, o_ref, *, final):
    y = jnp.concatenate([ya_ref[...], yb_ref[...], yc_ref[...], yd_ref[...]], axis=-1)
    gate = gate_ref[...]
    mixed = y * (gate * _sigmoid(gate))
    xn = x_ref[...] + _dot(mixed, w_ref[...])
    if final:
        xn = xn * lax.rsqrt(jnp.mean(xn * xn, axis=-1, keepdims=True) + NORM_EPS) * fg_ref[...]
    o_ref[...] = xn


def _outproj(ya, yb, yc, yd, proj, x, w, fg, final):
    t, d = x.shape
    tm = TM_OUTPROJ
    yspec = pl.BlockSpec((tm, W_GROUP), lambda i: (i, 0))
    return pl.pallas_call(
        functools.partial(_outproj_kernel, final=final),
        grid=(t // tm,),
        in_specs=[yspec, yspec, yspec, yspec,
                  pl.BlockSpec((tm, d), lambda i: (i, COL_GATE // D_MODEL)),
                  pl.BlockSpec((tm, d), lambda i: (i, 0)),
                  pl.BlockSpec((d, d), lambda i: (0, 0)),
                  pl.BlockSpec((1, d), lambda i: (0, 0))],
        out_specs=pl.BlockSpec((tm, d), lambda i: (i, 0)),
        out_shape=jax.ShapeDtypeStruct((t, d), F32),
        compiler_params=_cparams("parallel"),
        name="outproj_final" if final else "outproj",
    )(ya, yb, yc, yd, proj, x, w, fg)


def _rope128(x, c, sa, sb, half):
    return x * c + pltpu.roll(x, LANES - half, axis=1) * sa + pltpu.roll(x, half, axis=1) * sb


def _store_vt_tiles(v, vt_ref):
    for n in range(v.shape[0] // KS):
        vt_ref[n] = v[n * KS:(n + 1) * KS, :].T.astype(BF16)


def _mla_prep_kernel(cq_ref, ckv_ref, kr_ref, qg_ref, kvg_ref, wq_ref, wkv_ref, c_ref, sa_ref, sb_ref,
                     q_ref, k_ref, vt_ref):
    scale = (MLA_NOPE + MLA_ROPE) ** -0.5 * LOG2E
    half = MLA_ROPE // 2
    c, sa, sb = c_ref[...], sa_ref[...], sb_ref[...]
    cq = cq_ref[...]
    hq = cq * lax.rsqrt(jnp.mean(cq * cq, axis=-1, keepdims=True) + NORM_EPS) * qg_ref[...]
    q = _dot(hq, wq_ref[...])
    ckv = ckv_ref[...]
    hkv = ckv * lax.rsqrt(jnp.mean(ckv * ckv, axis=-1, keepdims=True) + NORM_EPS) * kvg_ref[...]
    kv = _dot(hkv, wkv_ref[...])
    kpe = _rope128(kr_ref[...], c, sa, sb, half).astype(BF16)
    for h in range(MLA_HEADS):
        base = 2 * LANES * h
        q_ref[:, base:base + LANES] = (q[:, base:base + LANES] * scale).astype(BF16)
        qpe = _rope128(q[:, base + LANES:base + 2 * LANES], c, sa, sb, half) * scale
        q_ref[:, base + LANES:base + 2 * LANES] = qpe.astype(BF16)
        k_ref[:, base:base + LANES] = kv[:, LANES * h:LANES * (h + 1)].astype(BF16)
        k_ref[:, base + LANES:base + 2 * LANES] = kpe
    _store_vt_tiles(kv[:, MLA_HEADS * MLA_NOPE:], vt_ref)


def _mla_prep(proj, qg, kvg, wq, wkv, c, sa, sb):
    t = proj.shape[0]
    tm = TM_PREP
    tab = pl.BlockSpec((tm, LANES), lambda i: (i, 0))
    full = lambda shape: pl.BlockSpec(shape, lambda i: (0, 0))
    return pl.pallas_call(
        _mla_prep_kernel,
        grid=(t // tm,),
        in_specs=[pl.BlockSpec((tm, MLA_Q_LORA), lambda i: (i, COL_CQ // MLA_Q_LORA)),
                  pl.BlockSpec((tm, MLA_KV_LORA), lambda i: (i, COL_CKV // MLA_KV_LORA)),
                  pl.BlockSpec((tm, LANES), lambda i: (i, COL_KROPE // LANES)),
                  full((1, MLA_Q_LORA)), full((1, MLA_KV_LORA)),
                  full(wq.shape), full(wkv.shape), tab, tab, tab],
        out_specs=[pl.BlockSpec((tm, MLA_HEADS * 2 * LANES), lambda i: (i, 0)),
                   pl.BlockSpec((tm, MLA_HEADS * 2 * LANES), lambda i: (i, 0)),
                   pl.BlockSpec((tm // KS, MLA_HEADS * MLA_V, KS), lambda i: (i, 0, 0))],
        out_shape=[jax.ShapeDtypeStruct((t, MLA_HEADS * 2 * LANES), BF16),
                   jax.ShapeDtypeStruct((t, MLA_HEADS * 2 * LANES), BF16),
                   jax.ShapeDtypeStruct((t // KS, MLA_HEADS * MLA_V, KS), BF16)],
        compiler_params=_cparams("parallel"),
        name="mla_prep",
    )(proj, proj, proj, qg, kvg, wq, wkv, c, sa, sb)


def _attn_kernel(*refs, n_maps, out_scale):
    q_refs = refs[:n_maps]
    k_ref, vt_ref = refs[n_maps:n_maps + 2]
    if n_maps == 2:
        lam_ref, g_ref = refs[n_maps + 2:n_maps + 4]
        o_ref = refs[n_maps + 4]
        scratch = refs[n_maps + 5:]
    else:
        o_ref = refs[n_maps + 2]
        scratch = refs[n_maps + 3:]
    states = [scratch[3 * a:3 * a + 3] for a in range(n_maps)]
    s_slots = scratch[3 * n_maps:3 * n_maps + 2]
    nq = k_ref.shape[0] // TQ
    sub_per_q = TQ // KS
    i = pl.program_id(2)
    q_blocks = (i, nq - 1 - i)

    for m_ref, l_ref, acc_ref in states:
        m_ref[...] = jnp.full(m_ref.shape, NEG_BIG, F32)
        l_ref[...] = jnp.zeros(l_ref.shape, F32)
        acc_ref[...] = jnp.zeros(acc_ref.shape, F32)

    items = [(0, q_blocks[0], True), (1, q_blocks[1], True)]
    for t in range(nq - 1):
        first = t < i
        items.append((jnp.where(first, 0, 1), jnp.where(first, t, t - i), False))

    def rows(blk, size):
        return pl.ds(pl.multiple_of(blk * size, size), size)

    def scores_into(item, s_ref):
        sel, jq, _ = item
        qblk = q_blocks[sel] if isinstance(sel, int) else jnp.where(sel == 0, q_blocks[0], q_blocks[1])
        for sub in range(sub_per_q):
            k = k_ref[rows(jq * sub_per_q + sub, KS), :]
            for a, q_ref in enumerate(q_refs):
                s_ref[a, sub] = lax.dot_general(k, q_ref[rows(qblk, TQ), :], (((1,), (1,)), ((), ())),
                                                preferred_element_type=F32)

    def consume(item, s_ref):
        sel, jq, diagonal = item
        for sub in range(sub_per_q):
            vt = vt_ref[jq * sub_per_q + sub]
            for a, (m_ref, l_ref, acc_ref) in enumerate(states):
                s = s_ref[a, sub]
                if diagonal:
                    key = lax.broadcasted_iota(jnp.int32, s.shape, 0) + sub * KS
                    qry = lax.broadcasted_iota(jnp.int32, s.shape, 1)
                    s = jnp.where(key <= qry, s, NEG_BIG)
                m_prev = m_ref[sel]
                m_new = jnp.maximum(m_prev, jnp.max(s, axis=0, keepdims=True))
                alpha = jnp.exp2(m_prev - m_new)
                p = jnp.exp2(s - m_new)
                l_ref[sel] = alpha * l_ref[sel] + jnp.sum(p.reshape(KS // SUBLANES, SUBLANES, TQ), axis=0)
                acc_ref[sel] = alpha * acc_ref[sel] + jnp.dot(vt, p.astype(BF16), preferred_element_type=F32)
                m_ref[sel] = m_new

    scores_into(items[0], s_slots[0])
    for n, item in enumerate(items):
        if n + 1 < len(items):
            scores_into(items[n + 1], s_slots[(n + 1) % 2])
        consume(item, s_slots[n % 2])

    for sel in range(2):
        outs = []
        for m_ref, l_ref, acc_ref in states:
            inv_l = 1.0 / jnp.sum(l_ref[sel], axis=0, keepdims=True)
            outs.append(acc_ref[sel] * inv_l)
        if n_maps == 2:
            ot = outs[0] - lam_ref[0, 0] * outs[1]
            ot = ot * lax.rsqrt(jnp.mean(ot * ot, axis=0, keepdims=True) + DIFF_SUBLN_EPS)
            o_ref[rows(q_blocks[sel], TQ), :] = ot.T * (g_ref[...] * out_scale)
        else:
            o_ref[rows(q_blocks[sel], TQ), :] = outs[0].T


def _attention(qs, k, vt, batch, heads, dqk, dv, extra=(), out_scale=1.0, name="attn"):
    n_maps = len(qs)
    t = k.shape[0]
    seq = t // batch
    nq = seq // TQ
    assert nq % 2 == 0
    seq_spec = lambda width: pl.BlockSpec((seq, width), lambda b, h, i: (b, h))
    in_specs = [seq_spec(dqk)] * (n_maps + 1) + [pl.BlockSpec((seq // KS, dv, KS), lambda b, h, i: (b, h, 0))]
    if n_maps == 2:
        in_specs += [pl.BlockSpec(memory_space=pltpu.SMEM), pl.BlockSpec((1, dv), lambda b, h, i: (0, 0))]
    state = [pltpu.VMEM((2, 1, TQ), F32), pltpu.VMEM((2, SUBLANES, TQ), F32), pltpu.VMEM((2, dv, TQ), F32)]
    s_slot = pltpu.VMEM((n_maps, TQ // KS, KS, TQ), F32)
    return pl.pallas_call(
        functools.partial(_attn_kernel, n_maps=n_maps, out_scale=out_scale),
        grid=(batch, heads, nq // 2),
        in_specs=in_specs,
        out_specs=seq_spec(dv),
        out_shape=jax.ShapeDtypeStruct((t, heads * dv), F32),
        scratch_shapes=state * n_maps + [s_slot, s_slot],
        compiler_params=_cparams("parallel", "parallel", "arbitrary"),
        name=name,
    )(*qs, k, vt, *extra)


def _diff_prep_kernel(q_ref, k_ref, v_ref, cq_ref, saq_ref, sbq_ref, ck_ref, sak_ref, sbk_ref,
                      q1_ref, q2_ref, ko_ref, vt_ref):
    half = DIFF_ROT // 2
    lane = lax.broadcasted_iota(jnp.int32, (1, LANES), 1)
    m0 = (lane < DIFF_QK).astype(F32)
    m1 = 1.0 - m0
    for h in range(DIFF_HEADS):
        sl = slice(LANES * h, LANES * (h + 1))
        q = _rope128(q_ref[:, sl], cq_ref[...], saq_ref[...], sbq_ref[...], half)
        q1_ref[:, sl] = (q * m0).astype(BF16)
        q2_ref[:, sl] = (q * m1).astype(BF16)
        ko_ref[:, sl] = _rope128(k_ref[:, sl], ck_ref[...], sak_ref[...], sbk_ref[...], half).astype(BF16)
    _store_vt_tiles(v_ref[...], vt_ref)


def _diff_prep(proj, cq, saq, sbq, ck, sak, sbk):
    t = proj.shape[0]
    tm = TM_PREP
    tab = pl.BlockSpec((tm, LANES), lambda i: (i, 0))
    seg = lambda col: pl.BlockSpec((tm, W_GROUP), lambda i: (i, col // W_GROUP))
    out = pl.BlockSpec((tm, W_GROUP), lambda i: (i, 0))
    shp = jax.ShapeDtypeStruct((t, W_GROUP), BF16)
    return pl.pallas_call(
        _diff_prep_kernel,
        grid=(t // tm,),
        in_specs=[seg(COL_QD), seg(COL_KD), seg(COL_VD), tab, tab, tab, tab, tab, tab],
        out_specs=[out, out, out, pl.BlockSpec((tm // KS, W_GROUP, KS), lambda i: (i, 0, 0))],
        out_shape=[shp, shp, shp, jax.ShapeDtypeStruct((t // KS, W_GROUP, KS), BF16)],
        compiler_params=_cparams("parallel"),
        name="diff_prep",
    )(proj, proj, proj, cq, saq, sbq, ck, sak, sbk)


def _s5_kernel(u_ref, bmat_ref, cre_ref, cim_ref, pwr_ref, pwi_ref, car_ref, cai_ref, d_ref, wglu_ref, bglu_ref,
               o_ref, hr_ref, hi_ref, sr_ref, si_ref):
    @pl.when(pl.program_id(1) == 0)
    def _():
        sr_ref[...] = jnp.zeros(sr_ref.shape, F32)
        si_ref[...] = jnp.zeros(si_ref.shape, F32)

    u = u_ref[...]
    ub = u.astype(BF16)
    n_slabs = W_GROUP // LANES
    sw = S5_WIDTH // n_slabs
    for q in range(n_slabs):
        bu = jnp.dot(ub[:, q * LANES:(q + 1) * LANES], bmat_ref[q], preferred_element_type=F32)
        hr_ref[:, q * sw:(q + 1) * sw] = bu[:, :sw]
        hi_ref[:, q * sw:(q + 1) * sw] = bu[:, sw:]

    n_tiles = S5_BLOCK // SUBLANES
    for c in range(S5_WIDTH // S5_LANE_CHUNK):
        ls = slice(c * S5_LANE_CHUNK, (c + 1) * S5_LANE_CHUNK)
        pw = [(pwr_ref[s, :, ls], pwi_ref[s, :, ls]) for s in range(3)]
        car, cai = car_ref[:, ls], cai_ref[:, ls]

        def tile(t, carry):
            cr, ci = carry
            rows = pl.ds(pl.multiple_of(t * SUBLANES, SUBLANES), SUBLANES)
            xr, xi = hr_ref[rows, ls], hi_ref[rows, ls]
            for s, (pr, pi) in enumerate(pw):
                rr = pltpu.roll(xr, 1 << s, axis=0)
                ri = pltpu.roll(xi, 1 << s, axis=0)
                xr, xi = xr + (pr * rr - pi * ri), xi + (pr * ri + pi * rr)
            xr, xi = xr + (car * cr - cai * ci), xi + (car * ci + cai * cr)
            hr_ref[rows, ls] = xr
            hi_ref[rows, ls] = xi
            return xr[SUBLANES - 1:SUBLANES, :], xi[SUBLANES - 1:SUBLANES, :]

        cr, ci = lax.fori_loop(0, n_tiles, tile, (sr_ref[:, ls], si_ref[:, ls]), unroll=2)
        sr_ref[:, ls] = cr
        si_ref[:, ls] = ci

    y = jnp.concatenate(
        [jnp.dot(hr_ref[:, q * sw:(q + 1) * sw].astype(BF16), cre_ref[q], preferred_element_type=F32)
         + jnp.dot(hi_ref[:, q * sw:(q + 1) * sw].astype(BF16), cim_ref[q], preferred_element_type=F32)
         for q in range(n_slabs)], axis=1)
    y = y + d_ref[...] * u
    g = 0.5 * y * (1.0 + jnp.tanh(math.sqrt(2.0 / math.pi) * (y + 0.044715 * (y * y * y))))
    o_ref[...] = g * _sigmoid(_dot(g, wglu_ref[...]) + bglu_ref[...])


def _s5(proj, batch, bmat, cre, cim, pwr, pwi, car, cai, d, wglu, bglu):
    t = proj.shape[0]
    nb = t // batch // S5_BLOCK
    full2 = lambda a: pl.BlockSpec(a.shape, lambda b, i: (0,) * a.ndim)
    ucol = COL_US5 // W_GROUP
    return pl.pallas_call(
        _s5_kernel,
        grid=(batch, nb),
        in_specs=[pl.BlockSpec((S5_BLOCK, W_GROUP), lambda b, i: (b * nb + i, ucol))]
        + [full2(a) for a in (bmat, cre, cim, pwr, pwi, car, cai, d, wglu, bglu)],
        out_specs=pl.BlockSpec((S5_BLOCK, W_GROUP), lambda b, i: (b * nb + i, 0)),
        out_shape=jax.ShapeDtypeStruct((t, W_GROUP), F32),
        scratch_shapes=[pltpu.VMEM((S5_BLOCK, S5_WIDTH), F32), pltpu.VMEM((S5_BLOCK, S5_WIDTH), F32),
                        pltpu.VMEM((1, S5_WIDTH), F32), pltpu.VMEM((1, S5_WIDTH), F32)],
        compiler_params=_cparams("parallel", "arbitrary"),
        name="s5",
    )(proj, bmat, cre, cim, pwr, pwi, car, cai, d, wglu, bglu)


def _rwkv_kernel(r_ref, k_ref, v_ref, lo_ref, mu_r_ref, mu_k_ref, mu_v_ref, mu_lo_ref, w0_ref, w2_ref, a0_ref,
                 a2_ref, kk_ref, ka_ref, rk_ref, lng_ref, lnb_ref, ones_ref,
                 o_ref,
                 st_ref, pr_ref, pk_ref, pv_ref, plo_ref,
                 rt_ref, at_ref, bt_ref, kt_ref, bg_ref, kg_ref, vv_ref, ge_ref, oo_ref,
                 qc_ref, ec_ref, mc_ref, dc_ref):
    tb, lc = RWKV_BLOCK, RWKV_CHUNK

    @pl.when(pl.program_id(1) == 0)
    def _():
        st_ref[...] = jnp.zeros(st_ref.shape, F32)
        pr_ref[...] = jnp.zeros(pr_ref.shape, F32)
        pk_ref[...] = jnp.zeros(pk_ref.shape, F32)
        pv_ref[...] = jnp.zeros(pv_ref.shape, F32)
        plo_ref[...] = jnp.zeros(plo_ref.shape, F32)

    def shifted(z_ref, prev_ref, mu_ref):
        z = z_ref[...]
        first = lax.broadcasted_iota(jnp.int32, z.shape, 0) == 0
        z_prev = jnp.where(first, prev_ref[...], pltpu.roll(z, 1, axis=0))
        prev_ref[...] = z[tb - 1:tb, :]
        return z + (z_prev - z) * mu_ref[...]

    r = shifted(r_ref, pr_ref, mu_r_ref)
    k = shifted(k_ref, pk_ref, mu_k_ref)
    v = shifted(v_ref, pv_ref, mu_v_ref)
    lo = shifted(lo_ref, plo_ref, mu_lo_ref)

    ones = ones_ref[...]
    wx = -(w0_ref[...] + _dot(jnp.tanh(lo), w2_ref[...]))
    w_log = -(jnp.maximum(wx, 0.0) + jnp.log(1.0 + jnp.exp(-jnp.abs(wx)))) - 0.5
    lw = -jnp.exp(w_log)
    a = _sigmoid(a0_ref[...] + _dot(lo, a2_ref[...]))
    kk = k * kk_ref[...]
    kk = kk / jnp.maximum(jnp.sqrt(_dot_split(kk * kk, ones)), 1e-12)
    k2 = k * (1.0 + (a - 1.0) * ka_ref[...])
    aa = -kk
    bb = kk * a

    row = lax.broadcasted_iota(jnp.int32, (lc, lc), 0)
    col = lax.broadcasted_iota(jnp.int32, (lc, lc), 1)
    tri = (row >= col).astype(BF16)
    for c in range(tb // lc):
        rs = slice(c * lc, (c + 1) * lc)
        lwc = lw[rs]
        cum = _dot_split3(tri, lwc)
        cum_last = cum[lc - 1:lc, :]
        e_neg = jnp.exp(-cum)
        e_end = jnp.exp(cum_last - cum)
        rt_ref[rs, :] = r[rs] * jnp.exp(cum)
        at_ref[rs, :] = aa[rs] * jnp.exp(cum - lwc)
        bt_ref[rs, :] = bb[rs] * e_neg
        kt_ref[rs, :] = k2[rs] * e_neg
        bg_ref[rs, :] = bb[rs] * e_end
        kg_ref[rs, :] = k2[rs] * e_end
        ge_ref[c] = jnp.broadcast_to(jnp.exp(cum_last), (SUBLANES, W_GROUP))
    vv_ref[...] = v

    lane = lax.broadcasted_iota(jnp.int32, (1, LANES), 1)
    m0 = (lane < RWKV_HEAD).astype(F32)
    m1 = 1.0 - m0
    r2 = lax.broadcasted_iota(jnp.int32, (2 * lc, 2 * lc), 0)
    c2 = lax.broadcasted_iota(jnp.int32, (2 * lc, 2 * lc), 1)
    same = jnp.right_shift(r2, 6) == jnp.right_shift(c2, 6)
    assert lc == 64
    strict = jnp.where(same & (r2 > c2), 1.0, 0.0).astype(F32)
    incl = jnp.where(same & (r2 >= c2), 1.0, 0.0).astype(F32)
    eye = jnp.where(r2 == c2, 1.0, 0.0).astype(F32)
    n_double = int(math.log2(lc)) - 1

    def stack(x):
        return jnp.concatenate([x * m0, x * m1], axis=0)

    pairs = range(RWKV_HEADS // 2)
    lsl = [slice(LANES * p, LANES * (p + 1)) for p in pairs]
    n_chunks = tb // lc
    dot_nt = lambda x, y: lax.dot_general(x, y, (((1,), (1,)), ((), ())), preferred_element_type=F32)
    dot_tn = lambda x, y: lax.dot_general(x, y, (((0,), (0,)), ((), ())), preferred_element_type=F32)
    dot_nn = lambda x, y: jnp.dot(x, y, preferred_element_type=F32)
    bf = lambda xs: [x.astype(BF16) for x in xs]

    def chunk_terms(chunks):
        chains = [(c, p) for c in chunks for p in range(RWKV_HEADS // 2)]
        pairs = range(len(chains))
        tile = lambda ref: [stack(ref[c * lc:(c + 1) * lc, LANES * p:LANES * (p + 1)]) for c, p in chains]
        a_s = bf(tile(at_ref))
        r_f = tile(rt_ref)
        b_s = bf(tile(bt_ref))
        k_s = bf(tile(kt_ref))
        v_s = bf(tile(vv_ref))
        bg_s = bf(tile(bg_ref))
        kg_s = bf(tile(kg_ref))
        big = [dot_nt(jnp.concatenate([a_s[p], r_f[p].astype(BF16)], axis=0),
                      jnp.concatenate([b_s[p], k_s[p]], axis=0)) for p in pairs]
        nil = [big[p][:2 * lc, :2 * lc] * strict for p in pairs]
        a_ak = bf([big[p][:2 * lc, 2 * lc:] * strict for p in pairs])
        a_rb = bf([big[p][2 * lc:, :2 * lc] * incl for p in pairs])
        a_rk = bf([big[p][2 * lc:, 2 * lc:] * incl for p in pairs])
        akv = bf([dot_nn(a_ak[p], v_s[p]) for p in pairs])
        inv = [eye + nil[p] for p in pairs]
        nb = bf(nil)
        nil = [dot_nn(nb[p], nb[p]) for p in pairs]
        for step in range(1, n_double):
            nb = bf(nil)
            prod = [dot_nn(nb[p], jnp.concatenate([nb[p], inv[p].astype(BF16)], axis=1)) for p in pairs]
            nil = [prod[p][:, :2 * lc] for p in pairs]
            inv = [inv[p] + prod[p][:, 2 * lc:] for p in pairs]
        ib = bf(inv)
        inv = [inv[p] + dot_nn(nil[p].astype(BF16), ib[p]) for p in pairs]
        ib = bf(inv)
        tatv = bf([dot_nn(ib[p], jnp.concatenate([a_s[p], akv[p]], axis=1)) for p in pairs])
        qe = [dot_nn(a_rb[p], tatv[p]) for p in pairs]
        qc = [r_f[p] + qe[p][:, :2 * lc] for p in pairs]
        ec = [qe[p][:, 2 * lc:] + dot_nn(a_rk[p], v_s[p]) for p in pairs]
        md = [dot_tn(tatv[p], bg_s[p]) for p in pairs]
        mc = [md[p][:2 * lc] for p in pairs]
        dc = [md[p][2 * lc:] + dot_tn(v_s[p], kg_s[p]) for p in pairs]
        for j, (c, p) in enumerate(chains):
            idx = c * (RWKV_HEADS // 2) + p
            qc_ref[idx] = qc[j].astype(BF16)
            ec_ref[idx] = ec[j]
            mc_ref[idx] = mc[j].astype(BF16)
            dc_ref[idx] = dc[j]

    for c0 in range(0, n_chunks, RWKV_CHUNKS_INTERLEAVED):
        chunk_terms(range(c0, c0 + RWKV_CHUNKS_INTERLEAVED))

    state = [st_ref[p] for p in pairs]
    for c in range(n_chunks):
        sb = bf(state)
        os_ = [dot_nt(qc_ref[c * len(pairs) + p], sb[p]) + ec_ref[c * len(pairs) + p] for p in pairs]
        state = [state[p] * ge_ref[c, 0:1, lsl[p]] + dot_nn(sb[p], mc_ref[c * len(pairs) + p])
                 + dc_ref[c * len(pairs) + p] for p in pairs]
        for p in pairs:
            oo_ref[c * lc:(c + 1) * lc, lsl[p]] = os_[p][:lc] + os_[p][lc:]
    for p in pairs:
        st_ref[p] = state[p]

    o = oo_ref[...]
    inv_n = 1.0 / RWKV_HEAD
    mean = _dot_split(o, ones) * inv_n
    oc = o - mean
    var = _dot_split(oc * oc, ones) * inv_n
    o = oc * lax.rsqrt(var + RWKV_GN_EPS) * lng_ref[...] + lnb_ref[...]
    bonus = _dot_split(r * k2 * rk_ref[...], ones) * v
    o_ref[...] = o + bonus


def _rwkv(proj, batch, mu_r, mu_k, mu_v, mu_lo, w0, w2p, a0, a2p, k_k, k_a, r_k, ln_g, ln_b, ones):
    t = proj.shape[0]
    tb = RWKV_BLOCK
    nb = t // batch // tb
    seg = lambda col: pl.BlockSpec((tb, W_GROUP), lambda b, i: (b * nb + i, col // W_GROUP))
    full2 = lambda a: pl.BlockSpec(a.shape, lambda b, i: (0,) * a.ndim)
    params = (mu_r, mu_k, mu_v, mu_lo, w0, w2p, a0, a2p, k_k, k_a, r_k, ln_g, ln_b, ones)
    buf = pltpu.VMEM((tb, W_GROUP), F32)
    term = (tb // RWKV_CHUNK * (RWKV_HEADS // 2), LANES, LANES)
    return pl.pallas_call(
        _rwkv_kernel,
        grid=(batch, nb),
        in_specs=[seg(COL_R), seg(COL_K), seg(COL_V),
                  pl.BlockSpec((tb, LANES), lambda b, i: (b * nb + i, COL_LORA // LANES))]
        + [full2(a) for a in params],
        out_specs=pl.BlockSpec((tb, W_GROUP), lambda b, i: (b * nb + i, 0)),
        out_shape=jax.ShapeDtypeStruct((t, W_GROUP), F32),
        scratch_shapes=[pltpu.VMEM((RWKV_HEADS // 2, LANES, LANES), F32),
                        pltpu.VMEM((1, W_GROUP), F32), pltpu.VMEM((1, W_GROUP), F32),
                        pltpu.VMEM((1, W_GROUP), F32), pltpu.VMEM((1, LANES), F32),
                        buf, buf, buf, buf, buf, buf, buf,
                        pltpu.VMEM((tb // RWKV_CHUNK, SUBLANES, W_GROUP), F32), buf,
                        pltpu.VMEM(term, BF16), pltpu.VMEM(term, F32), pltpu.VMEM(term, BF16), pltpu.VMEM(term, F32)],
        compiler_params=_cparams("parallel", "arbitrary"),
        name="rwkv",
    )(proj, proj, proj, proj, *params)


def _rope_tables(positions, rot, period, scale):
    half = rot // 2
    inv = ROPE_THETA ** (-jnp.arange(0, rot, 2, dtype=F32) / rot)
    ang = positions.reshape(-1).astype(F32)[:, None] * inv
    cos, sin = jnp.cos(ang), jnp.sin(ang)
    t = ang.shape[0]
    passthrough = period - rot
    c = jnp.concatenate([cos, cos, jnp.ones((t, passthrough), F32)], axis=1)
    sa = jnp.concatenate([-sin, jnp.zeros((t, half + passthrough), F32)], axis=1)
    sb = jnp.concatenate([jnp.zeros((t, half), F32), sin, jnp.zeros((t, passthrough), F32)], axis=1)
    reps = LANES // period
    return tuple(jnp.tile(a, (1, reps)) * scale for a in (c, sa, sb))


def _permute_w_in(w_in):
    d = w_in.shape[0]
    sizes = (MLA_Q_LORA, MLA_KV_LORA, MLA_ROPE, W_GROUP, W_GROUP, W_GROUP, W_GROUP, RWKV_LORA, RWKV_LORA,
             W_GROUP, W_GROUP, W_GROUP, D_MODEL)
    pts = [int(p) for p in np.cumsum(sizes)[:-1]]
    cq, ckv, krope, us5, r, k, v, wlo, alo, qd, kd, vd, gate = jnp.split(w_in, pts, axis=1)
    out = jnp.concatenate([gate, cq, us5, qd, kd, vd, r, k, v, ckv, krope, jnp.zeros((d, LANES - MLA_ROPE), w_in.dtype),
                           wlo, alo], axis=1)
    assert out.shape[1] == N_PROJ
    return out.astype(BF16)


def _s5_tables(a_re, a_im, log_dt, b_re, b_im, c_re, c_im):
    lr = jnp.minimum(a_re, -1e-4)
    li = a_im
    dt = jnp.exp(log_dt)[:, None]
    mag = jnp.exp(dt * lr)
    ab_re, ab_im = mag * jnp.cos(dt * li), mag * jnp.sin(dt * li)
    den = lr * lr + li * li
    nr, ni = ab_re - 1.0, ab_im
    f_re = (nr * lr + ni * li) / den
    f_im = (ni * lr - nr * li) / den
    bb_re = f_re[..., None] * b_re - f_im[..., None] * b_im
    bb_im = f_re[..., None] * b_im + f_im[..., None] * b_re
    eye = jnp.eye(S5_GROUPS, dtype=F32)
    blk_in = lambda m: jnp.einsum('gpc,gh->gchp', m, eye).reshape(W_GROUP, S5_WIDTH)
    blk_out = lambda m: jnp.einsum('gcp,gh->gphc', m, eye).reshape(S5_WIDTH, W_GROUP)
    n_slabs = W_GROUP // LANES
    sw = S5_WIDTH // n_slabs
    diag_in = lambda m: jnp.stack([m[q * LANES:(q + 1) * LANES, q * sw:(q + 1) * sw] for q in range(n_slabs)])
    diag_out = lambda m: jnp.stack([m[q * sw:(q + 1) * sw, q * LANES:(q + 1) * LANES] for q in range(n_slabs)])
    bmat = jnp.concatenate([diag_in(blk_in(bb_re)), diag_in(blk_in(bb_im))], axis=2).astype(BF16)
    cre = diag_out(blk_out(c_re)).astype(BF16)
    cim = diag_out(blk_out(-c_im)).astype(BF16)

    def power(n):
        m = jnp.exp(n * dt * lr)
        return m * jnp.cos(n * dt * li), m * jnp.sin(n * dt * li)

    rows = jnp.arange(SUBLANES, dtype=F32)[:, None, None]
    pwr, pwi = [], []
    for s in (1, 2, 4):
        pr, pi = power(jnp.full((1, 1, 1), float(s), F32))
        keep = (rows >= s).astype(F32)
        pwr.append((keep * pr).reshape(SUBLANES, S5_WIDTH))
        pwi.append((keep * pi).reshape(SUBLANES, S5_WIDTH))
    car, cai = power(rows + 1.0)
    return (bmat, cre, cim, jnp.stack(pwr), jnp.stack(pwi),
            car.reshape(SUBLANES, S5_WIDTH), cai.reshape(SUBLANES, S5_WIDTH))


def kernel(x, positions, norm_g, w_in, w_out, mla_q_norm_g, mla_kv_norm_g, mla_w_uq, mla_w_ukv, s5_a_re, s5_a_im, s5_log_dt, s5_b_re, s5_b_im, s5_c_re, s5_c_im, s5_d, s5_w_glu, s5_b_glu, rwkv_mu, rwkv_w0, rwkv_w2, rwkv_a0, rwkv_a2, rwkv_k_k, rwkv_k_a, rwkv_r_k, rwkv_ln_g, rwkv_ln_b, diff_lq1, diff_lk1, diff_lq2, diff_lk2, diff_subln_g, final_norm_g):
    batch, seq, d = x.shape
    depth = w_in.shape[0]
    t = batch * seq
    assert d == D_MODEL and seq % TQ == 0 and t % TM_INPROJ == 0
    assert seq % S5_BLOCK == 0 and seq % RWKV_BLOCK == 0

    mla_scale = 1.0
    ca, saa, sba = _rope_tables(positions, MLA_ROPE, LANES, mla_scale)
    zero_hi = (jnp.arange(LANES) < MLA_ROPE).astype(F32)[None, :]
    ca = ca * zero_hi
    cdq, sadq, sbdq = _rope_tables(positions, DIFF_ROT, DIFF_QK, DIFF_QK ** -0.5 * LOG2E)
    cdk, sadk, sbdk = _rope_tables(positions, DIFF_ROT, DIFF_QK, 1.0)

    head_ones = jnp.kron(jnp.eye(RWKV_HEADS, dtype=F32), jnp.ones((RWKV_HEAD, RWKV_HEAD), F32)).astype(BF16)
    row = lambda a: a.reshape(1, -1).astype(F32)

    xf = x.reshape(t, d)
    for l in range(depth):
        proj = _inproj(xf, row(norm_g[l]), _permute_w_in(w_in[l]))

        wq = mla_w_uq[l].reshape(MLA_Q_LORA, MLA_HEADS, MLA_NOPE + MLA_ROPE)
        wq = jnp.pad(wq, ((0, 0), (0, 0), (0, 2 * LANES - MLA_NOPE - MLA_ROPE))).reshape(MLA_Q_LORA, -1).astype(BF16)
        wkv = mla_w_ukv[l].reshape(MLA_KV_LORA, MLA_HEADS, MLA_NOPE + MLA_V)
        wkv = jnp.concatenate([wkv[:, :, :MLA_NOPE].reshape(MLA_KV_LORA, -1),
                               wkv[:, :, MLA_NOPE:].reshape(MLA_KV_LORA, -1)], axis=1).astype(BF16)
        qa, ka, va = _mla_prep(proj, row(mla_q_norm_g[l]), row(mla_kv_norm_g[l]), wq, wkv, ca, saa, sba)
        y_a = _attention([qa], ka, va, batch, MLA_HEADS, 2 * LANES, MLA_V, name="mla_attn")

        tabs = _s5_tables(s5_a_re[l], s5_a_im[l], s5_log_dt[l], s5_b_re[l], s5_b_im[l], s5_c_re[l], s5_c_im[l])
        y_b = _s5(proj, batch, *tabs, row(s5_d[l]), s5_w_glu[l].astype(BF16), row(s5_b_glu[l]))

        mu = rwkv_mu[l]
        zpad = jnp.zeros((RWKV_LORA, W_GROUP), F32)
        w2p = jnp.concatenate([rwkv_w2[l], zpad], axis=0).astype(BF16)
        a2p = jnp.concatenate([zpad, rwkv_a2[l]], axis=0).astype(BF16)
        y_c = _rwkv(proj, batch, row(mu[:W_GROUP]), row(mu[W_GROUP:2 * W_GROUP]), row(mu[2 * W_GROUP:3 * W_GROUP]),
                    row(mu[3 * W_GROUP:]), row(rwkv_w0[l]), w2p, row(rwkv_a0[l]), a2p, row(rwkv_k_k[l]),
                    row(rwkv_k_a[l]), row(rwkv_r_k[l]), row(rwkv_ln_g[l]), row(rwkv_ln_b[l]), head_ones)

        lam_init = 0.8 - 0.6 * math.exp(-0.3 * l)
        lam = (jnp.exp(jnp.sum(diff_lq1[l] * diff_lk1[l])) - jnp.exp(jnp.sum(diff_lq2[l] * diff_lk2[l])) + lam_init)
        q1, q2, kd, vd = _diff_prep(proj, cdq, sadq, sbdq, cdk, sadk, sbdk)
        y_d = _attention([q1, q2], kd, vd, batch, DIFF_HEADS, LANES, DIFF_V,
                         extra=(lam.reshape(1, 1).astype(F32), row(diff_subln_g[l])),
                         out_scale=1.0 - lam_init, name="diff_attn")

        xf = _outproj(y_a, y_b, y_c, y_d, proj, xf, w_out[l].astype(BF16), row(final_norm_g), l == depth - 1)
    return xf.reshape(batch, seq, d)
```

```python
import functools
import math

import jax
import jax.numpy as jnp
import numpy as np
from jax import lax
from jax.experimental import pallas as pl
from jax.experimental.pallas import tpu as pltpu

F32 = jnp.float32
BF16 = jnp.bfloat16
ACT = jnp.bfloat16

D_MODEL = 2048
W_GROUP = 512
ROPE_THETA = 500000.0
NORM_EPS = 1e-6
MLA_HEADS, MLA_NOPE, MLA_ROPE, MLA_V = 4, 128, 64, 128
MLA_Q_LORA, MLA_KV_LORA = 512, 256
S5_GROUP, S5_GROUPS, S5_STATE = 16, 32, 64
S5_WIDTH = S5_GROUPS * S5_STATE
RWKV_HEAD, RWKV_HEADS = 64, 8
RWKV_LORA = 64
RWKV_GN_EPS = 64e-5
DIFF_HEADS, DIFF_QK, DIFF_V, DIFF_ROT = 4, 64, 128, 16
DIFF_SUBLN_EPS = 1e-5

LANES = 128
SUBLANES = 8
VMEM_LIMIT_BYTES = 56 * 1024 * 1024

COL_GATE = 0
COL_CQ = 2048
COL_US5 = 2560
COL_QD = 3072
COL_KD = 3584
COL_VD = 4096
COL_R = 4608
COL_K = 5120
COL_V = 5632
COL_CKV = 6144
COL_KROPE = 6400
COL_LORA = 6528
N_PROJ = 6656

TM_INPROJ = 1024
TN_INPROJ = 1664
TM_OUTPROJ = 256
TM_PREP = 512
TQ = 512
KS = 256
S5_BLOCK = 256
S5_SEG = S5_BLOCK // 8
S5_LANE_CHUNK = 512
RWKV_BLOCK = 256
RWKV_CHUNK = 64
RWKV_CHUNKS_INTERLEAVED = 2
NEG_BIG = -1e30
LOG2E = math.log2(math.e)


def _cparams(*sem):
    return pltpu.CompilerParams(dimension_semantics=sem, vmem_limit_bytes=VMEM_LIMIT_BYTES)


def _dot(a, b):
    return jnp.dot(a.astype(BF16), b.astype(BF16), preferred_element_type=F32)


def _dot_nt(a, b):
    return lax.dot_general(a.astype(BF16), b.astype(BF16), (((1,), (1,)), ((), ())), preferred_element_type=F32)


def _dot_split(x, w):
    hi = x.astype(BF16)
    lo = (x - hi.astype(F32)).astype(BF16)
    return jnp.dot(hi, w, preferred_element_type=F32) + jnp.dot(lo, w, preferred_element_type=F32)


def _dot_split3(w, x):
    hi = x.astype(BF16)
    r1 = x - hi.astype(F32)
    mid = r1.astype(BF16)
    lo = (r1 - mid.astype(F32)).astype(BF16)
    return (jnp.dot(w, hi, preferred_element_type=F32) + jnp.dot(w, mid, preferred_element_type=F32)
            + jnp.dot(w, lo, preferred_element_type=F32))


def _sigmoid(x):
    return 1.0 / (1.0 + jnp.exp(-x))


def _inproj_kernel(x_ref, g_ref, w_ref, o_ref, h_ref):
    @pl.when(pl.program_id(1) == 0)
    def _():
        x = x_ref[...]
        y = x * lax.rsqrt(jnp.mean(x * x, axis=-1, keepdims=True) + NORM_EPS)
        h_ref[...] = (y * g_ref[...]).astype(BF16)

    o_ref[...] = jnp.dot(h_ref[...], w_ref[...], preferred_element_type=F32).astype(ACT)


def _inproj(x, g, w):
    t, d = x.shape
    n = w.shape[1]
    return pl.pallas_call(
        _inproj_kernel,
        grid=(t // TM_INPROJ, n // TN_INPROJ),
        in_specs=[pl.BlockSpec((TM_INPROJ, d), lambda i, j: (i, 0)),
                  pl.BlockSpec((1, d), lambda i, j: (0, 0)),
                  pl.BlockSpec((d, TN_INPROJ), lambda i, j: (0, j))],
        out_specs=pl.BlockSpec((TM_INPROJ, TN_INPROJ), lambda i, j: (i, j)),
        out_shape=jax.ShapeDtypeStruct((t, n), ACT),
        scratch_shapes=[pltpu.VMEM((TM_INPROJ, d), BF16)],
        compiler_params=_cparams("parallel", "arbitrary"),
        name="inproj",
    )(x, g, w)


def _outproj_kernel(ya_ref, yb_ref, yc_ref, yd_ref, gate_ref, x_ref, w_ref, fg_ref, o_ref, *, final):
    y = jnp.concatenate([ya_ref[...], yb_ref[...], yc_ref[...], yd_ref[...]], axis=-1).astype(F32)
    gate = gate_ref[...].astype(F32)
    mixed = y * (gate * _sigmoid(gate))
    xn = x_ref[...] + _dot(mixed, w_ref[...])
    if final:
        xn = xn * lax.rsqrt(jnp.mean(xn * xn, axis=-1, keepdims=True) + NORM_EPS) * fg_ref[...]
    o_ref[...] = xn


def _outproj(ya, yb, yc, yd, proj, x, w, fg, final):
    t, d = x.shape
    tm = TM_OUTPROJ
    yspec = pl.BlockSpec((tm, W_GROUP), lambda i: (i, 0))
    return pl.pallas_call(
        functools.partial(_outproj_kernel, final=final),
        grid=(t // tm,),
        in_specs=[yspec, yspec, yspec, yspec,
                  pl.BlockSpec((tm, d), lambda i: (i, COL_GATE // D_MODEL)),
                  pl.BlockSpec((tm, d), lambda i: (i, 0)),
                  pl.BlockSpec((d, d), lambda i: (0, 0)),
                  pl.BlockSpec((1, d), lambda i: (0, 0))],
        out_specs=pl.BlockSpec((tm, d), lambda i: (i, 0)),
        out_shape=jax.ShapeDtypeStruct((t, d), F32),
        compiler_params=_cparams("parallel"),
        name="outproj_final" if final else "outproj",
    )(ya, yb, yc, yd, proj, x, w, fg)


def _rope128(x, c, sa, sb, half):
    return x * c + pltpu.roll(x, LANES - half, axis=1) * sa + pltpu.roll(x, half, axis=1) * sb


def _store_vt_tiles(v, vt_ref):
    for n in range(v.shape[0] // KS):
        vt_ref[n] = v[n * KS:(n + 1) * KS, :].T.astype(BF16)


def _mla_prep_kernel(cq_ref, ckv_ref, kr_ref, qg_ref, kvg_ref, wq_ref, wkv_ref, c_ref, sa_ref, sb_ref,
                     q_ref, k_ref, vt_ref):
    scale = (MLA_NOPE + MLA_ROPE) ** -0.5 * LOG2E
    half = MLA_ROPE // 2
    c, sa, sb = c_ref[...], sa_ref[...], sb_ref[...]
    cq = cq_ref[...].astype(F32)
    hq = cq * lax.rsqrt(jnp.mean(cq * cq, axis=-1, keepdims=True) + NORM_EPS) * qg_ref[...]
    q = _dot(hq, wq_ref[...])
    ckv = ckv_ref[...].astype(F32)
    hkv = ckv * lax.rsqrt(jnp.mean(ckv * ckv, axis=-1, keepdims=True) + NORM_EPS) * kvg_ref[...]
    kv = _dot(hkv, wkv_ref[...])
    kpe = _rope128(kr_ref[...].astype(F32), c, sa, sb, half).astype(BF16)
    for h in range(MLA_HEADS):
        base = 2 * LANES * h
        q_ref[:, base:base + LANES] = (q[:, base:base + LANES] * scale).astype(BF16)
        qpe = _rope128(q[:, base + LANES:base + 2 * LANES], c, sa, sb, half) * scale
        q_ref[:, base + LANES:base + 2 * LANES] = qpe.astype(BF16)
        k_ref[:, base:base + LANES] = kv[:, LANES * h:LANES * (h + 1)].astype(BF16)
        k_ref[:, base + LANES:base + 2 * LANES] = kpe
    _store_vt_tiles(kv[:, MLA_HEADS * MLA_NOPE:], vt_ref)


def _mla_prep(proj, qg, kvg, wq, wkv, c, sa, sb):
    t = proj.shape[0]
    tm = TM_PREP
    tab = pl.BlockSpec((tm, LANES), lambda i: (i, 0))
    full = lambda shape: pl.BlockSpec(shape, lambda i: (0, 0))
    return pl.pallas_call(
        _mla_prep_kernel,
        grid=(t // tm,),
        in_specs=[pl.BlockSpec((tm, MLA_Q_LORA), lambda i: (i, COL_CQ // MLA_Q_LORA)),
                  pl.BlockSpec((tm, MLA_KV_LORA), lambda i: (i, COL_CKV // MLA_KV_LORA)),
                  pl.BlockSpec((tm, LANES), lambda i: (i, COL_KROPE // LANES)),
                  full((1, MLA_Q_LORA)), full((1, MLA_KV_LORA)),
                  full(wq.shape), full(wkv.shape), tab, tab, tab],
        out_specs=[pl.BlockSpec((tm, MLA_HEADS * 2 * LANES), lambda i: (i, 0)),
                   pl.BlockSpec((tm, MLA_HEADS * 2 * LANES), lambda i: (i, 0)),
                   pl.BlockSpec((tm // KS, MLA_HEADS * MLA_V, KS), lambda i: (i, 0, 0))],
        out_shape=[jax.ShapeDtypeStruct((t, MLA_HEADS * 2 * LANES), BF16),
                   jax.ShapeDtypeStruct((t, MLA_HEADS * 2 * LANES), BF16),
                   jax.ShapeDtypeStruct((t // KS, MLA_HEADS * MLA_V, KS), BF16)],
        compiler_params=_cparams("parallel"),
        name="mla_prep",
    )(proj, proj, proj, qg, kvg, wq, wkv, c, sa, sb)


def _attn_kernel(*refs, n_maps, out_scale):
    q_refs = refs[:n_maps]
    k_ref, vt_ref = refs[n_maps:n_maps + 2]
    if n_maps == 2:
        lam_ref, g_ref = refs[n_maps + 2:n_maps + 4]
        o_ref = refs[n_maps + 4]
        scratch = refs[n_maps + 5:]
    else:
        o_ref = refs[n_maps + 2]
        scratch = refs[n_maps + 3:]
    states = [scratch[3 * a:3 * a + 3] for a in range(n_maps)]
    s_slots = scratch[3 * n_maps:3 * n_maps + 2]
    nq = k_ref.shape[0] // TQ
    sub_per_q = TQ // KS
    i = pl.program_id(2)
    q_blocks = (i, nq - 1 - i)

    for m_ref, l_ref, acc_ref in states:
        m_ref[...] = jnp.full(m_ref.shape, NEG_BIG, F32)
        l_ref[...] = jnp.zeros(l_ref.shape, F32)
        acc_ref[...] = jnp.zeros(acc_ref.shape, F32)

    items = [(0, q_blocks[0], True), (1, q_blocks[1], True)]
    for t in range(nq - 1):
        first = t < i
        items.append((jnp.where(first, 0, 1), jnp.where(first, t, t - i), False))

    def rows(blk, size):
        return pl.ds(pl.multiple_of(blk * size, size), size)

    def scores_into(item, s_ref):
        sel, jq, _ = item
        qblk = q_blocks[sel] if isinstance(sel, int) else jnp.where(sel == 0, q_blocks[0], q_blocks[1])
        for sub in range(sub_per_q):
            k = k_ref[rows(jq * sub_per_q + sub, KS), :]
            for a, q_ref in enumerate(q_refs):
                s_ref[a, sub] = lax.dot_general(k, q_ref[rows(qblk, TQ), :], (((1,), (1,)), ((), ())),
                                                preferred_element_type=F32)

    def consume(item, s_ref):
        sel, jq, diagonal = item
        for sub in range(sub_per_q):
            vt = vt_ref[jq * sub_per_q + sub]
            for a, (m_ref, l_ref, acc_ref) in enumerate(states):
                s = s_ref[a, sub]
                if diagonal:
                    key = lax.broadcasted_iota(jnp.int32, s.shape, 0) + sub * KS
                    qry = lax.broadcasted_iota(jnp.int32, s.shape, 1)
                    s = jnp.where(key <= qry, s, NEG_BIG)
                m_prev = m_ref[sel]
                m_new = jnp.maximum(m_prev, jnp.max(s, axis=0, keepdims=True))
                alpha = jnp.exp2(m_prev - m_new)
                p = jnp.exp2(s - m_new)
                l_ref[sel] = alpha * l_ref[sel] + jnp.sum(p.reshape(KS // SUBLANES, SUBLANES, TQ), axis=0)
                acc_ref[sel] = alpha * acc_ref[sel] + jnp.dot(vt, p.astype(BF16), preferred_element_type=F32)
                m_ref[sel] = m_new

    scores_into(items[0], s_slots[0])
    for n, item in enumerate(items):
        if n + 1 < len(items):
            scores_into(items[n + 1], s_slots[(n + 1) % 2])
        consume(item, s_slots[n % 2])

    for sel in range(2):
        outs = []
        for m_ref, l_ref, acc_ref in states:
            inv_l = 1.0 / jnp.sum(l_ref[sel], axis=0, keepdims=True)
            outs.append(acc_ref[sel] * inv_l)
        if n_maps == 2:
            ot = outs[0] - lam_ref[0, 0] * outs[1]
            ot = ot * lax.rsqrt(jnp.mean(ot * ot, axis=0, keepdims=True) + DIFF_SUBLN_EPS)
            o_ref[rows(q_blocks[sel], TQ), :] = (ot.T * (g_ref[...] * out_scale)).astype(ACT)
        else:
            o_ref[rows(q_blocks[sel], TQ), :] = outs[0].T.astype(ACT)


def _attention(qs, k, vt, batch, heads, dqk, dv, extra=(), out_scale=1.0, name="attn"):
    n_maps = len(qs)
    t = k.shape[0]
    seq = t // batch
    nq = seq // TQ
    assert nq % 2 == 0
    seq_spec = lambda width: pl.BlockSpec((seq, width), lambda b, h, i: (b, h))
    in_specs = [seq_spec(dqk)] * (n_maps + 1) + [pl.BlockSpec((seq // KS, dv, KS), lambda b, h, i: (b, h, 0))]
    if n_maps == 2:
        in_specs += [pl.BlockSpec(memory_space=pltpu.SMEM), pl.BlockSpec((1, dv), lambda b, h, i: (0, 0))]
    state = [pltpu.VMEM((2, 1, TQ), F32), pltpu.VMEM((2, SUBLANES, TQ), F32), pltpu.VMEM((2, dv, TQ), F32)]
    s_slot = pltpu.VMEM((n_maps, TQ // KS, KS, TQ), F32)
    return pl.pallas_call(
        functools.partial(_attn_kernel, n_maps=n_maps, out_scale=out_scale),
        grid=(batch, heads, nq // 2),
        in_specs=in_specs,
        out_specs=seq_spec(dv),
        out_shape=jax.ShapeDtypeStruct((t, heads * dv), ACT),
        scratch_shapes=state * n_maps + [s_slot, s_slot],
        compiler_params=_cparams("parallel", "parallel", "arbitrary"),
        name=name,
    )(*qs, k, vt, *extra)


def _diff_prep_kernel(q_ref, k_ref, v_ref, c_ref, sa_ref, sb_ref, q1_ref, q2_ref, ko_ref, vt_ref):
    half = DIFF_ROT // 2
    scale = DIFF_QK ** -0.5 * LOG2E
    lane = lax.broadcasted_iota(jnp.int32, (1, LANES), 1)
    m0 = jnp.where(lane < DIFF_QK, scale, 0.0).astype(F32)
    m1 = scale - m0
    c, sa, sb = c_ref[...], sa_ref[...], sb_ref[...]
    for h in range(DIFF_HEADS):
        sl = slice(LANES * h, LANES * (h + 1))
        q = _rope128(q_ref[:, sl].astype(F32), c, sa, sb, half)
        q1_ref[:, sl] = (q * m0).astype(BF16)
        q2_ref[:, sl] = (q * m1).astype(BF16)
        ko_ref[:, sl] = _rope128(k_ref[:, sl].astype(F32), c, sa, sb, half).astype(BF16)
    _store_vt_tiles(v_ref[...].astype(F32), vt_ref)


def _diff_prep(proj, c, sa, sb):
    t = proj.shape[0]
    tm = TM_PREP
    tab = pl.BlockSpec((tm, LANES), lambda i: (i, 0))
    seg = lambda col: pl.BlockSpec((tm, W_GROUP), lambda i: (i, col // W_GROUP))
    out = pl.BlockSpec((tm, W_GROUP), lambda i: (i, 0))
    shp = jax.ShapeDtypeStruct((t, W_GROUP), BF16)
    return pl.pallas_call(
        _diff_prep_kernel,
        grid=(t // tm,),
        in_specs=[seg(COL_QD), seg(COL_KD), seg(COL_VD), tab, tab, tab],
        out_specs=[out, out, out, pl.BlockSpec((tm // KS, W_GROUP, KS), lambda i: (i, 0, 0))],
        out_shape=[shp, shp, shp, jax.ShapeDtypeStruct((t // KS, W_GROUP, KS), BF16)],
        compiler_params=_cparams("parallel"),
        name="diff_prep",
    )(proj, proj, proj, c, sa, sb)


def _s5_kernel(u_ref, bmat_ref, cre_ref, cim_ref, a1r_ref, a1i_ref, pwr_ref, pwi_ref, pcr_ref, pci_ref, d_ref,
               wglu_ref, bglu_ref, o_ref, hr_ref, hi_ref, sr_ref, si_ref, uf_ref, up_ref, op_ref):
    @pl.when(pl.program_id(1) == 0)
    def _():
        sr_ref[...] = jnp.zeros(sr_ref.shape, F32)
        si_ref[...] = jnp.zeros(si_ref.shape, F32)

    n_slabs = W_GROUP // LANES
    uf = u_ref[...].astype(F32)
    for q in range(n_slabs):
        uf_ref[q] = uf[:, q * LANES:(q + 1) * LANES]
    for t in range(S5_SEG):
        for q in range(n_slabs):
            up_ref[t * SUBLANES:(t + 1) * SUBLANES, q * LANES:(q + 1) * LANES] = (
                uf_ref[q, pl.ds(t, SUBLANES, stride=S5_SEG), :])
    u = up_ref[...]
    ub = u.astype(BF16)
    sw = S5_WIDTH // n_slabs
    for q in range(n_slabs):
        bu = jnp.dot(ub[:, q * LANES:(q + 1) * LANES], bmat_ref[q], preferred_element_type=F32)
        hr_ref[:, q * sw:(q + 1) * sw] = bu[:, :sw]
        hi_ref[:, q * sw:(q + 1) * sw] = bu[:, sw:]

    first_row = lax.broadcasted_iota(jnp.int32, (SUBLANES, S5_LANE_CHUNK), 0) == 0
    for c in range(S5_WIDTH // S5_LANE_CHUNK):
        ls = slice(c * S5_LANE_CHUNK, (c + 1) * S5_LANE_CHUNK)
        ar, ai = a1r_ref[:, ls], a1i_ref[:, ls]

        def tile(t, carry):
            xr, xi = carry
            rows = pl.ds(pl.multiple_of(t * SUBLANES, SUBLANES), SUBLANES)
            xr, xi = hr_ref[rows, ls] + (ar * xr - ai * xi), hi_ref[rows, ls] + (ar * xi + ai * xr)
            hr_ref[rows, ls] = xr
            hi_ref[rows, ls] = xi
            return xr, xi

        zero = jnp.zeros((SUBLANES, S5_LANE_CHUNK), F32)
        er, ei = lax.fori_loop(0, S5_SEG, tile, (zero, zero), unroll=4)

        gr = jnp.where(first_row, sr_ref[:, ls], pltpu.roll(er, 1, axis=0))
        gi = jnp.where(first_row, si_ref[:, ls], pltpu.roll(ei, 1, axis=0))
        for s in range(3):
            pr, pi = pwr_ref[s, :, ls], pwi_ref[s, :, ls]
            rr = pltpu.roll(gr, 1 << s, axis=0)
            ri = pltpu.roll(gi, 1 << s, axis=0)
            gr, gi = gr + (pr * rr - pi * ri), gi + (pr * ri + pi * rr)
        last = slice(SUBLANES - 1, SUBLANES)
        pr, pi = pwr_ref[0, last, ls], pwi_ref[0, last, ls]
        sr_ref[:, ls] = er[last] + (pr * gr[last] - pi * gi[last])
        si_ref[:, ls] = ei[last] + (pr * gi[last] + pi * gr[last])

        for t in range(S5_SEG):
            rows = slice(t * SUBLANES, (t + 1) * SUBLANES)
            pr, pi = pcr_ref[t:t + 1, ls], pci_ref[t:t + 1, ls]
            hr_ref[rows, ls] = hr_ref[rows, ls] + (pr * gr - pi * gi)
            hi_ref[rows, ls] = hi_ref[rows, ls] + (pr * gi + pi * gr)

    y = jnp.concatenate(
        [jnp.dot(hr_ref[:, q * sw:(q + 1) * sw].astype(BF16), cre_ref[q], preferred_element_type=F32)
         + jnp.dot(hi_ref[:, q * sw:(q + 1) * sw].astype(BF16), cim_ref[q], preferred_element_type=F32)
         for q in range(n_slabs)], axis=1)
    y = y + d_ref[...] * u
    g = 0.5 * y * (1.0 + jnp.tanh(math.sqrt(2.0 / math.pi) * (y + 0.044715 * (y * y * y))))
    out = g * _sigmoid(_dot(g, wglu_ref[...]) + bglu_ref[...])
    for q in range(n_slabs):
        op_ref[q] = out[:, q * LANES:(q + 1) * LANES]
    for r in range(SUBLANES):
        o_ref[r * S5_SEG:(r + 1) * S5_SEG, :] = jnp.concatenate(
            [op_ref[q, pl.ds(r, S5_SEG, stride=SUBLANES), :] for q in range(n_slabs)], axis=1).astype(ACT)


def _s5(proj, batch, bmat, cre, cim, a1r, a1i, pwr, pwi, pcr, pci, d, wglu, bglu):
    t = proj.shape[0]
    nb = t // batch // S5_BLOCK
    full2 = lambda a: pl.BlockSpec(a.shape, lambda b, i: (0,) * a.ndim)
    ucol = COL_US5 // W_GROUP
    return pl.pallas_call(
        _s5_kernel,
        grid=(batch, nb),
        in_specs=[pl.BlockSpec((S5_BLOCK, W_GROUP), lambda b, i: (b * nb + i, ucol))]
        + [full2(a) for a in (bmat, cre, cim, a1r, a1i, pwr, pwi, pcr, pci, d, wglu, bglu)],
        out_specs=pl.BlockSpec((S5_BLOCK, W_GROUP), lambda b, i: (b * nb + i, 0)),
        out_shape=jax.ShapeDtypeStruct((t, W_GROUP), ACT),
        scratch_shapes=[pltpu.VMEM((S5_BLOCK, S5_WIDTH), F32), pltpu.VMEM((S5_BLOCK, S5_WIDTH), F32),
                        pltpu.VMEM((1, S5_WIDTH), F32), pltpu.VMEM((1, S5_WIDTH), F32),
                        pltpu.VMEM((W_GROUP // LANES, S5_BLOCK, LANES), F32), pltpu.VMEM((S5_BLOCK, W_GROUP), F32),
                        pltpu.VMEM((W_GROUP // LANES, S5_BLOCK, LANES), F32)],
        compiler_params=_cparams("parallel", "arbitrary"),
        name="s5",
    )(proj, bmat, cre, cim, a1r, a1i, pwr, pwi, pcr, pci, d, wglu, bglu)


def _rwkv_kernel(r_ref, k_ref, v_ref, lo_ref, mu_r_ref, mu_k_ref, mu_v_ref, mu_lo_ref, w0_ref, w2_ref, a0_ref,
                 a2_ref, kk_ref, ka_ref, rk_ref, lng_ref, lnb_ref, ones_ref,
                 o_ref,
                 st_ref, pr_ref, pk_ref, pv_ref, plo_ref,
                 rt_ref, at_ref, bt_ref, kt_ref, bg_ref, kg_ref, vv_ref, ge_ref, oo_ref,
                 qc_ref, ec_ref, mc_ref, dc_ref):
    tb, lc = RWKV_BLOCK, RWKV_CHUNK

    @pl.when(pl.program_id(1) == 0)
    def _():
        st_ref[...] = jnp.zeros(st_ref.shape, F32)
        pr_ref[...] = jnp.zeros(pr_ref.shape, F32)
        pk_ref[...] = jnp.zeros(pk_ref.shape, F32)
        pv_ref[...] = jnp.zeros(pv_ref.shape, F32)
        plo_ref[...] = jnp.zeros(plo_ref.shape, F32)

    def shifted(z_ref, prev_ref, mu_ref):
        z = z_ref[...].astype(F32)
        first = lax.broadcasted_iota(jnp.int32, z.shape, 0) == 0
        z_prev = jnp.where(first, prev_ref[...], pltpu.roll(z, 1, axis=0))
        prev_ref[...] = z[tb - 1:tb, :]
        return z + (z_prev - z) * mu_ref[...]

    r = shifted(r_ref, pr_ref, mu_r_ref)
    k = shifted(k_ref, pk_ref, mu_k_ref)
    v = shifted(v_ref, pv_ref, mu_v_ref)
    lo = shifted(lo_ref, plo_ref, mu_lo_ref)

    ones = ones_ref[...]
    wx = -(w0_ref[...] + _dot(jnp.tanh(lo), w2_ref[...]))
    w_log = -(jnp.maximum(wx, 0.0) + jnp.log(1.0 + jnp.exp(-jnp.abs(wx)))) - 0.5
    lw = -jnp.exp(w_log)
    a = _sigmoid(a0_ref[...] + _dot(lo, a2_ref[...]))
    kk = k * kk_ref[...]
    kk = kk / jnp.maximum(jnp.sqrt(_dot_split(kk * kk, ones)), 1e-12)
    k2 = k * (1.0 + (a - 1.0) * ka_ref[...])
    aa = -kk
    bb = kk * a

    row = lax.broadcasted_iota(jnp.int32, (lc, lc), 0)
    col = lax.broadcasted_iota(jnp.int32, (lc, lc), 1)
    tri = (row >= col).astype(BF16)
    for c in range(tb // lc):
        rs = slice(c * lc, (c + 1) * lc)
        lwc = lw[rs]
        cum = _dot_split3(tri, lwc)
        cum_last = cum[lc - 1:lc, :]
        e_neg = jnp.exp(-cum)
        e_end = jnp.exp(cum_last - cum)
        rt_ref[rs, :] = r[rs] * jnp.exp(cum)
        at_ref[rs, :] = aa[rs] * jnp.exp(cum - lwc)
        bt_ref[rs, :] = bb[rs] * e_neg
        kt_ref[rs, :] = k2[rs] * e_neg
        bg_ref[rs, :] = bb[rs] * e_end
        kg_ref[rs, :] = k2[rs] * e_end
        ge_ref[c] = jnp.broadcast_to(jnp.exp(cum_last), (SUBLANES, W_GROUP))
    vv_ref[...] = v

    lane = lax.broadcasted_iota(jnp.int32, (1, LANES), 1)
    m0 = (lane < RWKV_HEAD).astype(F32)
    m1 = 1.0 - m0
    r2 = lax.broadcasted_iota(jnp.int32, (2 * lc, 2 * lc), 0)
    c2 = lax.broadcasted_iota(jnp.int32, (2 * lc, 2 * lc), 1)
    same = jnp.right_shift(r2, 6) == jnp.right_shift(c2, 6)
    assert lc == 64
    strict = jnp.where(same & (r2 > c2), 1.0, 0.0).astype(F32)
    incl = jnp.where(same & (r2 >= c2), 1.0, 0.0).astype(F32)
    eye = jnp.where(r2 == c2, 1.0, 0.0).astype(F32)
    n_double = int(math.log2(lc)) - 1

    def stack(x):
        return jnp.concatenate([x * m0, x * m1], axis=0)

    pairs = range(RWKV_HEADS // 2)
    lsl = [slice(LANES * p, LANES * (p + 1)) for p in pairs]
    n_chunks = tb // lc
    dot_nt = lambda x, y: lax.dot_general(x, y, (((1,), (1,)), ((), ())), preferred_element_type=F32)
    dot_tn = lambda x, y: lax.dot_general(x, y, (((0,), (0,)), ((), ())), preferred_element_type=F32)
    dot_nn = lambda x, y: jnp.dot(x, y, preferred_element_type=F32)
    bf = lambda xs: [x.astype(BF16) for x in xs]

    def chunk_terms(chunks):
        chains = [(c, p) for c in chunks for p in range(RWKV_HEADS // 2)]
        pairs = range(len(chains))
        tile = lambda ref: [stack(ref[c * lc:(c + 1) * lc, LANES * p:LANES * (p + 1)]) for c, p in chains]
        a_s = bf(tile(at_ref))
        r_f = tile(rt_ref)
        b_s = bf(tile(bt_ref))
        k_s = bf(tile(kt_ref))
        v_s = bf(tile(vv_ref))
        bg_s = bf(tile(bg_ref))
        kg_s = bf(tile(kg_ref))
        big = [dot_nt(jnp.concatenate([a_s[p], r_f[p].astype(BF16)], axis=0),
                      jnp.concatenate([b_s[p], k_s[p]], axis=0)) for p in pairs]
        nil = [big[p][:2 * lc, :2 * lc] * strict for p in pairs]
        a_ak = bf([big[p][:2 * lc, 2 * lc:] * strict for p in pairs])
        a_rb = bf([big[p][2 * lc:, :2 * lc] * incl for p in pairs])
        a_rk = bf([big[p][2 * lc:, 2 * lc:] * incl for p in pairs])
        akv = bf([dot_nn(a_ak[p], v_s[p]) for p in pairs])
        inv = [eye + nil[p] for p in pairs]
        nb = bf(nil)
        nil = [dot_nn(nb[p], nb[p]) for p in pairs]
        for step in range(1, n_double):
            nb = bf(nil)
            prod = [dot_nn(nb[p], jnp.concatenate([nb[p], inv[p].astype(BF16)], axis=1)) for p in pairs]
            nil = [prod[p][:, :2 * lc] for p in pairs]
            inv = [inv[p] + prod[p][:, 2 * lc:] for p in pairs]
        ib = bf(inv)
        inv = [inv[p] + dot_nn(nil[p].astype(BF16), ib[p]) for p in pairs]
        ib = bf(inv)
        tatv = bf([dot_nn(ib[p], jnp.concatenate([a_s[p], akv[p]], axis=1)) for p in pairs])
        qe = [dot_nn(a_rb[p], tatv[p]) for p in pairs]
        qc = [r_f[p] + qe[p][:, :2 * lc] for p in pairs]
        ec = [qe[p][:, 2 * lc:] + dot_nn(a_rk[p], v_s[p]) for p in pairs]
        md = [dot_tn(tatv[p], bg_s[p]) for p in pairs]
        mc = [md[p][:2 * lc] for p in pairs]
        dc = [md[p][2 * lc:] + dot_tn(v_s[p], kg_s[p]) for p in pairs]
        for j, (c, p) in enumerate(chains):
            idx = c * (RWKV_HEADS // 2) + p
            qc_ref[idx] = qc[j].astype(BF16)
            ec_ref[idx] = ec[j]
            mc_ref[idx] = mc[j].astype(BF16)
            dc_ref[idx] = dc[j]

    for c0 in range(0, n_chunks, RWKV_CHUNKS_INTERLEAVED):
        chunk_terms(range(c0, c0 + RWKV_CHUNKS_INTERLEAVED))

    state = [st_ref[p] for p in pairs]
    for c in range(n_chunks):
        sb = bf(state)
        os_ = [dot_nt(qc_ref[c * len(pairs) + p], sb[p]) + ec_ref[c * len(pairs) + p] for p in pairs]
        state = [state[p] * ge_ref[c, 0:1, lsl[p]] + dot_nn(sb[p], mc_ref[c * len(pairs) + p])
                 + dc_ref[c * len(pairs) + p] for p in pairs]
        for p in pairs:
            oo_ref[c * lc:(c + 1) * lc, lsl[p]] = os_[p][:lc] + os_[p][lc:]
    for p in pairs:
        st_ref[p] = state[p]

    o = oo_ref[...]
    inv_n = 1.0 / RWKV_HEAD
    mean = _dot_split(o, ones) * inv_n
    oc = o - mean
    var = _dot_split(oc * oc, ones) * inv_n
    o = oc * lax.rsqrt(var + RWKV_GN_EPS) * lng_ref[...] + lnb_ref[...]
    bonus = _dot_split(r * k2 * rk_ref[...], ones) * v
    o_ref[...] = (o + bonus).astype(ACT)


def _rwkv(proj, batch, mu_r, mu_k, mu_v, mu_lo, w0, w2p, a0, a2p, k_k, k_a, r_k, ln_g, ln_b, ones):
    t = proj.shape[0]
    tb = RWKV_BLOCK
    nb = t // batch // tb
    seg = lambda col: pl.BlockSpec((tb, W_GROUP), lambda b, i: (b * nb + i, col // W_GROUP))
    full2 = lambda a: pl.BlockSpec(a.shape, lambda b, i: (0,) * a.ndim)
    params = (mu_r, mu_k, mu_v, mu_lo, w0, w2p, a0, a2p, k_k, k_a, r_k, ln_g, ln_b, ones)
    buf = pltpu.VMEM((tb, W_GROUP), F32)
    term = (tb // RWKV_CHUNK * (RWKV_HEADS // 2), LANES, LANES)
    return pl.pallas_call(
        _rwkv_kernel,
        grid=(batch, nb),
        in_specs=[seg(COL_R), seg(COL_K), seg(COL_V),
                  pl.BlockSpec((tb, LANES), lambda b, i: (b * nb + i, COL_LORA // LANES))]
        + [full2(a) for a in params],
        out_specs=pl.BlockSpec((tb, W_GROUP), lambda b, i: (b * nb + i, 0)),
        out_shape=jax.ShapeDtypeStruct((t, W_GROUP), ACT),
        scratch_shapes=[pltpu.VMEM((RWKV_HEADS // 2, LANES, LANES), F32),
                        pltpu.VMEM((1, W_GROUP), F32), pltpu.VMEM((1, W_GROUP), F32),
                        pltpu.VMEM((1, W_GROUP), F32), pltpu.VMEM((1, LANES), F32),
                        buf, buf, buf, buf, buf, buf, buf,
                        pltpu.VMEM((tb // RWKV_CHUNK, SUBLANES, W_GROUP), F32), buf,
                        pltpu.VMEM(term, BF16), pltpu.VMEM(term, F32), pltpu.VMEM(term, BF16), pltpu.VMEM(term, F32)],
        compiler_params=_cparams("parallel", "arbitrary"),
        name="rwkv",
    )(proj, proj, proj, proj, *params)


def _rope_tables(positions, rot, period, scale):
    half = rot // 2
    inv = ROPE_THETA ** (-jnp.arange(0, rot, 2, dtype=F32) / rot)
    ang = positions.reshape(-1).astype(F32)[:, None] * inv
    cos, sin = jnp.cos(ang), jnp.sin(ang)
    t = ang.shape[0]
    passthrough = period - rot
    c = jnp.concatenate([cos, cos, jnp.ones((t, passthrough), F32)], axis=1)
    sa = jnp.concatenate([-sin, jnp.zeros((t, half + passthrough), F32)], axis=1)
    sb = jnp.concatenate([jnp.zeros((t, half), F32), sin, jnp.zeros((t, passthrough), F32)], axis=1)
    reps = LANES // period
    return tuple(jnp.tile(a, (1, reps)) * scale for a in (c, sa, sb))


def _permute_w_in(w_in):
    d = w_in.shape[0]
    w_in = w_in.astype(BF16)
    sizes = (MLA_Q_LORA, MLA_KV_LORA, MLA_ROPE, W_GROUP, W_GROUP, W_GROUP, W_GROUP, RWKV_LORA, RWKV_LORA,
             W_GROUP, W_GROUP, W_GROUP, D_MODEL)
    pts = [int(p) for p in np.cumsum(sizes)[:-1]]
    cq, ckv, krope, us5, r, k, v, wlo, alo, qd, kd, vd, gate = jnp.split(w_in, pts, axis=1)
    out = jnp.concatenate([gate, cq, us5, qd, kd, vd, r, k, v, ckv, krope, jnp.zeros((d, LANES - MLA_ROPE), w_in.dtype),
                           wlo, alo], axis=1)
    assert out.shape[1] == N_PROJ
    return out.astype(BF16)


def _s5_tables(a_re, a_im, log_dt, b_re, b_im, c_re, c_im):
    lr = jnp.minimum(a_re, -1e-4)
    li = a_im
    dt = jnp.exp(log_dt)[:, None]
    mag = jnp.exp(dt * lr)
    ab_re, ab_im = mag * jnp.cos(dt * li), mag * jnp.sin(dt * li)
    den = lr * lr + li * li
    nr, ni = ab_re - 1.0, ab_im
    f_re = (nr * lr + ni * li) / den
    f_im = (ni * lr - nr * li) / den
    bb_re = f_re[..., None] * b_re - f_im[..., None] * b_im
    bb_im = f_re[..., None] * b_im + f_im[..., None] * b_re
    eye = jnp.eye(S5_GROUPS, dtype=F32)
    blk_in = lambda m: jnp.einsum('gpc,gh->gchp', m, eye).reshape(W_GROUP, S5_WIDTH)
    blk_out = lambda m: jnp.einsum('gcp,gh->gphc', m, eye).reshape(S5_WIDTH, W_GROUP)
    n_slabs = W_GROUP // LANES
    sw = S5_WIDTH // n_slabs
    diag_in = lambda m: jnp.stack([m[q * LANES:(q + 1) * LANES, q * sw:(q + 1) * sw] for q in range(n_slabs)])
    diag_out = lambda m: jnp.stack([m[q * sw:(q + 1) * sw, q * LANES:(q + 1) * LANES] for q in range(n_slabs)])
    bmat = jnp.concatenate([diag_in(blk_in(bb_re)), diag_in(blk_in(bb_im))], axis=2).astype(BF16)
    cre = diag_out(blk_out(c_re)).astype(BF16)
    cim = diag_out(blk_out(-c_im)).astype(BF16)

    def power(n):
        m = jnp.exp(n * dt * lr)
        return m * jnp.cos(n * dt * li), m * jnp.sin(n * dt * li)

    rows = jnp.arange(SUBLANES, dtype=F32)[:, None, None]
    a1r, a1i = power(jnp.ones((SUBLANES, 1, 1), F32))
    pwr, pwi = [], []
    for s in (1, 2, 4):
        pr, pi = power(jnp.full((1, 1, 1), float(s * S5_SEG), F32))
        keep = (rows >= s).astype(F32)
        pwr.append((keep * pr).reshape(SUBLANES, S5_WIDTH))
        pwi.append((keep * pi).reshape(SUBLANES, S5_WIDTH))
    pcr, pci = power(jnp.arange(1, S5_SEG + 1, dtype=F32)[:, None, None])
    return (bmat, cre, cim, a1r.reshape(SUBLANES, S5_WIDTH), a1i.reshape(SUBLANES, S5_WIDTH),
            jnp.stack(pwr), jnp.stack(pwi), pcr.reshape(S5_SEG, S5_WIDTH), pci.reshape(S5_SEG, S5_WIDTH))


def kernel(x, positions, norm_g, w_in, w_out, mla_q_norm_g, mla_kv_norm_g, mla_w_uq, mla_w_ukv, s5_a_re, s5_a_im, s5_log_dt, s5_b_re, s5_b_im, s5_c_re, s5_c_im, s5_d, s5_w_glu, s5_b_glu, rwkv_mu, rwkv_w0, rwkv_w2, rwkv_a0, rwkv_a2, rwkv_k_k, rwkv_k_a, rwkv_r_k, rwkv_ln_g, rwkv_ln_b, diff_lq1, diff_lk1, diff_lq2, diff_lk2, diff_subln_g, final_norm_g):
    batch, seq, d = x.shape
    depth = w_in.shape[0]
    t = batch * seq
    assert d == D_MODEL and seq % TQ == 0 and t % TM_INPROJ == 0
    assert seq % S5_BLOCK == 0 and seq % RWKV_BLOCK == 0

    ca, saa, sba = _rope_tables(positions, MLA_ROPE, LANES, 1.0)
    zero_hi = (jnp.arange(LANES) < MLA_ROPE).astype(F32)[None, :]
    ca = ca * zero_hi
    cd, sad, sbd = _rope_tables(positions, DIFF_ROT, DIFF_QK, 1.0)

    head_ones = jnp.kron(jnp.eye(RWKV_HEADS, dtype=F32), jnp.ones((RWKV_HEAD, RWKV_HEAD), F32)).astype(BF16)
    row = lambda a: a.reshape(1, -1).astype(F32)

    def layer_params(norm_g, w_in, w_out, qg, kvg, w_uq, w_ukv, a_re, a_im, log_dt, b_re, b_im, c_re, c_im, s5_d,
                     w_glu, b_glu, mu, w0, w2, a0, a2, k_k, k_a, r_k, ln_g, ln_b, lq1, lk1, lq2, lk2, subln_g):
        wq = w_uq.reshape(MLA_Q_LORA, MLA_HEADS, MLA_NOPE + MLA_ROPE)
        wq = jnp.pad(wq, ((0, 0), (0, 0), (0, 2 * LANES - MLA_NOPE - MLA_ROPE))).reshape(MLA_Q_LORA, -1)
        wkv = w_ukv.reshape(MLA_KV_LORA, MLA_HEADS, MLA_NOPE + MLA_V)
        wkv = jnp.concatenate([wkv[:, :, :MLA_NOPE].reshape(MLA_KV_LORA, -1),
                               wkv[:, :, MLA_NOPE:].reshape(MLA_KV_LORA, -1)], axis=1)
        zpad = jnp.zeros((RWKV_LORA, W_GROUP), F32)
        return dict(
            inproj=(row(norm_g), _permute_w_in(w_in)),
            mla=(row(qg), row(kvg), wq.astype(BF16), wkv.astype(BF16)),
            s5=_s5_tables(a_re, a_im, log_dt, b_re, b_im, c_re, c_im) + (row(s5_d), w_glu.astype(BF16), row(b_glu)),
            rwkv=(row(mu[:W_GROUP]), row(mu[W_GROUP:2 * W_GROUP]), row(mu[2 * W_GROUP:3 * W_GROUP]),
                  row(mu[3 * W_GROUP:]), row(w0),
                  jnp.concatenate([w2, zpad], axis=0).astype(BF16),
                  row(a0),
                  jnp.concatenate([zpad, a2], axis=0).astype(BF16),
                  row(k_k), row(k_a), row(r_k), row(ln_g), row(ln_b)),
            lam=jnp.exp(jnp.sum(lq1 * lk1)) - jnp.exp(jnp.sum(lq2 * lk2)),
            subln_g=row(subln_g),
            w_out=w_out.astype(BF16),
        )

    params = jax.vmap(layer_params)(
        norm_g, w_in, w_out, mla_q_norm_g, mla_kv_norm_g, mla_w_uq, mla_w_ukv, s5_a_re, s5_a_im, s5_log_dt, s5_b_re,
        s5_b_im, s5_c_re, s5_c_im, s5_d, s5_w_glu, s5_b_glu, rwkv_mu, rwkv_w0, rwkv_w2, rwkv_a0, rwkv_a2, rwkv_k_k,
        rwkv_k_a, rwkv_r_k, rwkv_ln_g, rwkv_ln_b, diff_lq1, diff_lk1, diff_lq2, diff_lk2, diff_subln_g)

    xf = x.reshape(t, d)
    for l in range(depth):
        p = jax.tree.map(lambda a: a[l], params)
        proj = _inproj(xf, *p["inproj"])

        qa, ka, va = _mla_prep(proj, *p["mla"], ca, saa, sba)
        y_a = _attention([qa], ka, va, batch, MLA_HEADS, 2 * LANES, MLA_V, name="mla_attn")

        y_b = _s5(proj, batch, *p["s5"])

        y_c = _rwkv(proj, batch, *p["rwkv"], head_ones)

        lam_init = 0.8 - 0.6 * math.exp(-0.3 * l)
        q1, q2, kd, vd = _diff_prep(proj, cd, sad, sbd)
        y_d = _attention([q1, q2], kd, vd, batch, DIFF_HEADS, LANES, DIFF_V,
                         extra=((p["lam"] + lam_init).reshape(1, 1).astype(F32), p["subln_g"]),
                         out_scale=1.0 - lam_init, name="diff_attn")

        xf = _outproj(y_a, y_b, y_c, y_d, proj, xf, p["w_out"], row(final_norm_g), l == depth - 1)
    return xf.reshape(batch, seq, d)
```

```python
import functools
import math

import jax
import jax.numpy as jnp
import numpy as np
from jax import lax
from jax.experimental import pallas as pl
from jax.experimental.pallas import tpu as pltpu

F32 = jnp.float32
BF16 = jnp.bfloat16
ACT = jnp.bfloat16

D_MODEL = 2048
W_GROUP = 512
ROPE_THETA = 500000.0
NORM_EPS = 1e-6
MLA_HEADS, MLA_NOPE, MLA_ROPE, MLA_V = 4, 128, 64, 128
MLA_Q_LORA, MLA_KV_LORA = 512, 256
S5_GROUP, S5_GROUPS, S5_STATE = 16, 32, 64
S5_WIDTH = S5_GROUPS * S5_STATE
RWKV_HEAD, RWKV_HEADS = 64, 8
RWKV_LORA = 64
RWKV_GN_EPS = 64e-5
DIFF_HEADS, DIFF_QK, DIFF_V, DIFF_ROT = 4, 64, 128, 16
DIFF_SUBLN_EPS = 1e-5

LANES = 128
SUBLANES = 8
VMEM_LIMIT_BYTES = 56 * 1024 * 1024

COL_GATE = 0
COL_CQ = 2048
COL_US5 = 2560
COL_QD = 3072
COL_KD = 3584
COL_VD = 4096
COL_R = 4608
COL_K = 5120
COL_V = 5632
COL_CKV = 6144
COL_KROPE = 6400
COL_LORA = 6528
N_PROJ = 6656

TM_INPROJ = 1024
TN_INPROJ = 1664
TM_OUTPROJ = 512
TM_PREP = 512
TQ = 512
KS = 256
S5_BLOCK = 512
S5_SEG = S5_BLOCK // 8
S5_LANE_CHUNK = 512
RWKV_BLOCK = 256
RWKV_CHUNK = 64
RWKV_CHUNKS_INTERLEAVED = 2
NEG_BIG = -1e30
LOG2E = math.log2(math.e)


def _cparams(*sem):
    return pltpu.CompilerParams(dimension_semantics=sem, vmem_limit_bytes=VMEM_LIMIT_BYTES)


def _dot(a, b):
    return jnp.dot(a.astype(BF16), b.astype(BF16), preferred_element_type=F32)


def _dot_nt(a, b):
    return lax.dot_general(a.astype(BF16), b.astype(BF16), (((1,), (1,)), ((), ())), preferred_element_type=F32)


def _dot_split(x, w):
    hi = x.astype(BF16)
    lo = (x - hi.astype(F32)).astype(BF16)
    return jnp.dot(hi, w, preferred_element_type=F32) + jnp.dot(lo, w, preferred_element_type=F32)


def _dot_split3(w, x):
    hi = x.astype(BF16)
    r1 = x - hi.astype(F32)
    mid = r1.astype(BF16)
    lo = (r1 - mid.astype(F32)).astype(BF16)
    return (jnp.dot(w, hi, preferred_element_type=F32) + jnp.dot(w, mid, preferred_element_type=F32)
            + jnp.dot(w, lo, preferred_element_type=F32))


def _sigmoid(x):
    return 1.0 / (1.0 + jnp.exp(-x))


def _inproj_kernel(x_ref, g_ref, w_ref, o_ref, h_ref):
    @pl.when(pl.program_id(1) == 0)
    def _():
        x = x_ref[...]
        y = x * lax.rsqrt(jnp.mean(x * x, axis=-1, keepdims=True) + NORM_EPS)
        h_ref[...] = (y * g_ref[...]).astype(BF16)

    o_ref[...] = jnp.dot(h_ref[...], w_ref[...], preferred_element_type=F32).astype(ACT)


def _inproj(x, g, w):
    t, d = x.shape
    n = w.shape[1]
    return pl.pallas_call(
        _inproj_kernel,
        grid=(t // TM_INPROJ, n // TN_INPROJ),
        in_specs=[pl.BlockSpec((TM_INPROJ, d), lambda i, j: (i, 0)),
                  pl.BlockSpec((1, d), lambda i, j: (0, 0)),
                  pl.BlockSpec((d, TN_INPROJ), lambda i, j: (0, j))],
        out_specs=pl.BlockSpec((TM_INPROJ, TN_INPROJ), lambda i, j: (i, j)),
        out_shape=jax.ShapeDtypeStruct((t, n), ACT),
        scratch_shapes=[pltpu.VMEM((TM_INPROJ, d), BF16)],
        compiler_params=_cparams("parallel", "arbitrary"),
        name="inproj",
    )(x, g, w)


def _outproj_kernel(ya_ref, yb_ref, yc_ref, yd_ref, gate_ref, x_ref, w_ref, fg_ref, o_ref, *, final):
    y = jnp.concatenate([ya_ref[...], yb_ref[...], yc_ref[...], yd_ref[...]], axis=-1).astype(F32)
    gate = gate_ref[...].astype(F32)
    mixed = y * (gate * _sigmoid(gate))
    xn = x_ref[...] + _dot(mixed, w_ref[...])
    if final:
        xn = xn * lax.rsqrt(jnp.mean(xn * xn, axis=-1, keepdims=True) + NORM_EPS) * fg_ref[...]
    o_ref[...] = xn


def _outproj(ya, yb, yc, yd, proj, x, w, fg, final):
    t, d = x.shape
    tm = TM_OUTPROJ
    yspec = pl.BlockSpec((tm, W_GROUP), lambda i: (i, 0))
    return pl.pallas_call(
        functools.partial(_outproj_kernel, final=final),
        grid=(t // tm,),
        in_specs=[yspec, yspec, yspec, yspec,
                  pl.BlockSpec((tm, d), lambda i: (i, COL_GATE // D_MODEL)),
                  pl.BlockSpec((tm, d), lambda i: (i, 0)),
                  pl.BlockSpec((d, d), lambda i: (0, 0)),
                  pl.BlockSpec((1, d), lambda i: (0, 0))],
        out_specs=pl.BlockSpec((tm, d), lambda i: (i, 0)),
        out_shape=jax.ShapeDtypeStruct((t, d), F32),
        compiler_params=_cparams("parallel"),
        name="outproj_final" if final else "outproj",
    )(ya, yb, yc, yd, proj, x, w, fg)


def _rope128(x, c, sa, sb, half):
    return x * c + pltpu.roll(x, LANES - half, axis=1) * sa + pltpu.roll(x, half, axis=1) * sb


def _store_vt_tiles(v, vt_ref):
    for n in range(v.shape[0] // KS):
        vt_ref[n] = v[n * KS:(n + 1) * KS, :].T.astype(BF16)


def _mla_prep_kernel(cq_ref, ckv_ref, kr_ref, qg_ref, kvg_ref, wq_ref, wkv_ref, c_ref, sa_ref, sb_ref,
                     q_ref, k_ref, vt_ref):
    scale = (MLA_NOPE + MLA_ROPE) ** -0.5 * LOG2E
    half = MLA_ROPE // 2
    c, sa, sb = c_ref[...], sa_ref[...], sb_ref[...]
    cq = cq_ref[...].astype(F32)
    hq = cq * lax.rsqrt(jnp.mean(cq * cq, axis=-1, keepdims=True) + NORM_EPS) * qg_ref[...]
    q = _dot(hq, wq_ref[...])
    ckv = ckv_ref[...].astype(F32)
    hkv = ckv * lax.rsqrt(jnp.mean(ckv * ckv, axis=-1, keepdims=True) + NORM_EPS) * kvg_ref[...]
    kv = _dot(hkv, wkv_ref[...])
    kpe = _rope128(kr_ref[...].astype(F32), c, sa, sb, half).astype(BF16)
    for h in range(MLA_HEADS):
        base = 2 * LANES * h
        q_ref[:, base:base + LANES] = (q[:, base:base + LANES] * scale).astype(BF16)
        qpe = _rope128(q[:, base + LANES:base + 2 * LANES], c, sa, sb, half) * scale
        q_ref[:, base + LANES:base + 2 * LANES] = qpe.astype(BF16)
        k_ref[:, base:base + LANES] = kv[:, LANES * h:LANES * (h + 1)].astype(BF16)
        k_ref[:, base + LANES:base + 2 * LANES] = kpe
    _store_vt_tiles(kv[:, MLA_HEADS * MLA_NOPE:], vt_ref)


def _mla_prep(proj, qg, kvg, wq, wkv, c, sa, sb):
    t = proj.shape[0]
    tm = TM_PREP
    tab = pl.BlockSpec((tm, LANES), lambda i: (i, 0))
    full = lambda shape: pl.BlockSpec(shape, lambda i: (0, 0))
    return pl.pallas_call(
        _mla_prep_kernel,
        grid=(t // tm,),
        in_specs=[pl.BlockSpec((tm, MLA_Q_LORA), lambda i: (i, COL_CQ // MLA_Q_LORA)),
                  pl.BlockSpec((tm, MLA_KV_LORA), lambda i: (i, COL_CKV // MLA_KV_LORA)),
                  pl.BlockSpec((tm, LANES), lambda i: (i, COL_KROPE // LANES)),
                  full((1, MLA_Q_LORA)), full((1, MLA_KV_LORA)),
                  full(wq.shape), full(wkv.shape), tab, tab, tab],
        out_specs=[pl.BlockSpec((tm, MLA_HEADS * 2 * LANES), lambda i: (i, 0)),
                   pl.BlockSpec((tm, MLA_HEADS * 2 * LANES), lambda i: (i, 0)),
                   pl.BlockSpec((tm // KS, MLA_HEADS * MLA_V, KS), lambda i: (i, 0, 0))],
        out_shape=[jax.ShapeDtypeStruct((t, MLA_HEADS * 2 * LANES), BF16),
                   jax.ShapeDtypeStruct((t, MLA_HEADS * 2 * LANES), BF16),
                   jax.ShapeDtypeStruct((t // KS, MLA_HEADS * MLA_V, KS), BF16)],
        compiler_params=_cparams("parallel"),
        name="mla_prep",
    )(proj, proj, proj, qg, kvg, wq, wkv, c, sa, sb)


def _attn_kernel(*refs, n_maps, out_scale):
    q_refs = refs[:n_maps]
    k_ref, vt_ref = refs[n_maps:n_maps + 2]
    if n_maps == 2:
        lam_ref, g_ref = refs[n_maps + 2:n_maps + 4]
        o_ref = refs[n_maps + 4]
        scratch = refs[n_maps + 5:]
    else:
        o_ref = refs[n_maps + 2]
        scratch = refs[n_maps + 3:]
    states = [scratch[3 * a:3 * a + 3] for a in range(n_maps)]
    s_slots = scratch[3 * n_maps:3 * n_maps + 2]
    nq = k_ref.shape[0] // TQ
    sub_per_q = TQ // KS
    i = pl.program_id(2)
    q_blocks = (i, nq - 1 - i)

    for m_ref, l_ref, acc_ref in states:
        m_ref[...] = jnp.full(m_ref.shape, NEG_BIG, F32)
        l_ref[...] = jnp.zeros(l_ref.shape, F32)
        acc_ref[...] = jnp.zeros(acc_ref.shape, F32)

    half = nq // 2
    items = [(0, q_blocks[0], True), (1, q_blocks[1], True)] + [(1, t, False) for t in range(half)]
    for t in range(half - 1):
        first = t < i
        items.append((jnp.where(first, 0, 1), jnp.where(first, t, half + t - i), False))

    def rows(blk, size):
        if isinstance(blk, int):
            return pl.ds(blk * size, size)
        return pl.ds(pl.multiple_of(blk * size, size), size)

    def scores_into(item, s_ref):
        sel, jq, _ = item
        qblk = q_blocks[sel] if isinstance(sel, int) else jnp.where(sel == 0, q_blocks[0], q_blocks[1])
        for sub in range(sub_per_q):
            k = k_ref[rows(jq * sub_per_q + sub, KS), :]
            for a, q_ref in enumerate(q_refs):
                s_ref[a, sub] = lax.dot_general(k, q_ref[rows(qblk, TQ), :], (((1,), (1,)), ((), ())),
                                                preferred_element_type=F32)

    def consume(item, s_ref):
        sel, jq, diagonal = item
        for sub in range(sub_per_q):
            vt = vt_ref[jq * sub_per_q + sub]
            for a, (m_ref, l_ref, acc_ref) in enumerate(states):
                s = s_ref[a, sub]
                if diagonal:
                    key = lax.broadcasted_iota(jnp.int32, s.shape, 0) + sub * KS
                    qry = lax.broadcasted_iota(jnp.int32, s.shape, 1)
                    s = jnp.where(key <= qry, s, NEG_BIG)
                m_prev = m_ref[sel]
                m_new = jnp.maximum(m_prev, jnp.max(s, axis=0, keepdims=True))
                alpha = jnp.exp2(m_prev - m_new)
                p = jnp.exp2(s - m_new)
                l_ref[sel] = alpha * l_ref[sel] + jnp.sum(p.reshape(KS // SUBLANES, SUBLANES, TQ), axis=0)
                acc_ref[sel] = alpha * acc_ref[sel] + jnp.dot(vt, p.astype(BF16), preferred_element_type=F32)
                m_ref[sel] = m_new

    scores_into(items[0], s_slots[0])
    for n, item in enumerate(items):
        if n + 1 < len(items):
            scores_into(items[n + 1], s_slots[(n + 1) % 2])
        consume(item, s_slots[n % 2])

    for sel in range(2):
        outs = []
        for m_ref, l_ref, acc_ref in states:
            inv_l = 1.0 / jnp.sum(l_ref[sel], axis=0, keepdims=True)
            outs.append(acc_ref[sel] * inv_l)
        if n_maps == 2:
            ot = outs[0] - lam_ref[0, 0] * outs[1]
            ot = ot * lax.rsqrt(jnp.mean(ot * ot, axis=0, keepdims=True) + DIFF_SUBLN_EPS)
            o_ref[rows(q_blocks[sel], TQ), :] = (ot.T * (g_ref[...] * out_scale)).astype(ACT)
        else:
            o_ref[rows(q_blocks[sel], TQ), :] = outs[0].T.astype(ACT)


def _attention(qs, k, vt, batch, heads, dqk, dv, extra=(), out_scale=1.0, name="attn"):
    n_maps = len(qs)
    t = k.shape[0]
    seq = t // batch
    nq = seq // TQ
    assert nq % 2 == 0
    seq_spec = lambda width: pl.BlockSpec((seq, width), lambda b, h, i: (b, h))
    in_specs = [seq_spec(dqk)] * (n_maps + 1) + [pl.BlockSpec((seq // KS, dv, KS), lambda b, h, i: (b, h, 0))]
    if n_maps == 2:
        in_specs += [pl.BlockSpec(memory_space=pltpu.SMEM), pl.BlockSpec((1, dv), lambda b, h, i: (0, 0))]
    state = [pltpu.VMEM((2, 1, TQ), F32), pltpu.VMEM((2, SUBLANES, TQ), F32), pltpu.VMEM((2, dv, TQ), F32)]
    s_slot = pltpu.VMEM((n_maps, TQ // KS, KS, TQ), F32)
    return pl.pallas_call(
        functools.partial(_attn_kernel, n_maps=n_maps, out_scale=out_scale),
        grid=(batch, heads, nq // 2),
        in_specs=in_specs,
        out_specs=seq_spec(dv),
        out_shape=jax.ShapeDtypeStruct((t, heads * dv), ACT),
        scratch_shapes=state * n_maps + [s_slot, s_slot],
        compiler_params=_cparams("parallel", "parallel", "arbitrary"),
        name=name,
    )(*qs, k, vt, *extra)


def _diff_prep_kernel(q_ref, k_ref, v_ref, c_ref, sa_ref, sb_ref, q1_ref, q2_ref, ko_ref, vt_ref):
    half = DIFF_ROT // 2
    scale = DIFF_QK ** -0.5 * LOG2E
    lane = lax.broadcasted_iota(jnp.int32, (1, LANES), 1)
    m0 = jnp.where(lane < DIFF_QK, scale, 0.0).astype(F32)
    m1 = scale - m0
    c, sa, sb = c_ref[...], sa_ref[...], sb_ref[...]
    for h in range(DIFF_HEADS):
        sl = slice(LANES * h, LANES * (h + 1))
        q = _rope128(q_ref[:, sl].astype(F32), c, sa, sb, half)
        q1_ref[:, sl] = (q * m0).astype(BF16)
        q2_ref[:, sl] = (q * m1).astype(BF16)
        ko_ref[:, sl] = _rope128(k_ref[:, sl].astype(F32), c, sa, sb, half).astype(BF16)
    _store_vt_tiles(v_ref[...].astype(F32), vt_ref)


def _diff_prep(proj, c, sa, sb):
    t = proj.shape[0]
    tm = TM_PREP
    tab = pl.BlockSpec((tm, LANES), lambda i: (i, 0))
    seg = lambda col: pl.BlockSpec((tm, W_GROUP), lambda i: (i, col // W_GROUP))
    out = pl.BlockSpec((tm, W_GROUP), lambda i: (i, 0))
    shp = jax.ShapeDtypeStruct((t, W_GROUP), BF16)
    return pl.pallas_call(
        _diff_prep_kernel,
        grid=(t // tm,),
        in_specs=[seg(COL_QD), seg(COL_KD), seg(COL_VD), tab, tab, tab],
        out_specs=[out, out, out, pl.BlockSpec((tm // KS, W_GROUP, KS), lambda i: (i, 0, 0))],
        out_shape=[shp, shp, shp, jax.ShapeDtypeStruct((t // KS, W_GROUP, KS), BF16)],
        compiler_params=_cparams("parallel"),
        name="diff_prep",
    )(proj, proj, proj, c, sa, sb)


def _s5_kernel(u_ref, bmat_ref, cre_ref, cim_ref, a1r_ref, a1i_ref, pwr_ref, pwi_ref, pcr_ref, pci_ref, d_ref,
               wglu_ref, bglu_ref, o_ref, hr_ref, hi_ref, sr_ref, si_ref, uf_ref, up_ref, op_ref):
    @pl.when(pl.program_id(1) == 0)
    def _():
        sr_ref[...] = jnp.zeros(sr_ref.shape, F32)
        si_ref[...] = jnp.zeros(si_ref.shape, F32)

    n_slabs = W_GROUP // LANES
    uf = u_ref[...].astype(F32)
    for q in range(n_slabs):
        uf_ref[q] = uf[:, q * LANES:(q + 1) * LANES]
    for t in range(S5_SEG):
        for q in range(n_slabs):
            up_ref[t * SUBLANES:(t + 1) * SUBLANES, q * LANES:(q + 1) * LANES] = (
                uf_ref[q, pl.ds(t, SUBLANES, stride=S5_SEG), :])
    u = up_ref[...]
    ub = u.astype(BF16)
    sw = S5_WIDTH // n_slabs
    for q in range(n_slabs):
        bu = jnp.dot(ub[:, q * LANES:(q + 1) * LANES], bmat_ref[q], preferred_element_type=F32)
        hr_ref[:, q * sw:(q + 1) * sw] = bu[:, :sw]
        hi_ref[:, q * sw:(q + 1) * sw] = bu[:, sw:]

    first_row = lax.broadcasted_iota(jnp.int32, (SUBLANES, S5_LANE_CHUNK), 0) == 0
    for c in range(S5_WIDTH // S5_LANE_CHUNK):
        ls = slice(c * S5_LANE_CHUNK, (c + 1) * S5_LANE_CHUNK)
        ar, ai = a1r_ref[:, ls], a1i_ref[:, ls]

        def tile(t, carry):
            xr, xi = carry
            rows = pl.ds(pl.multiple_of(t * SUBLANES, SUBLANES), SUBLANES)
            xr, xi = hr_ref[rows, ls] + (ar * xr - ai * xi), hi_ref[rows, ls] + (ar * xi + ai * xr)
            hr_ref[rows, ls] = xr
            hi_ref[rows, ls] = xi
            return xr, xi

        zero = jnp.zeros((SUBLANES, S5_LANE_CHUNK), F32)
        er, ei = lax.fori_loop(0, S5_SEG, tile, (zero, zero), unroll=4)

        gr = jnp.where(first_row, sr_ref[:, ls], pltpu.roll(er, 1, axis=0))
        gi = jnp.where(first_row, si_ref[:, ls], pltpu.roll(ei, 1, axis=0))
        for s in range(3):
            pr, pi = pwr_ref[s, :, ls], pwi_ref[s, :, ls]
            rr = pltpu.roll(gr, 1 << s, axis=0)
            ri = pltpu.roll(gi, 1 << s, axis=0)
            gr, gi = gr + (pr * rr - pi * ri), gi + (pr * ri + pi * rr)
        last = slice(SUBLANES - 1, SUBLANES)
        pr, pi = pwr_ref[0, last, ls], pwi_ref[0, last, ls]
        sr_ref[:, ls] = er[last] + (pr * gr[last] - pi * gi[last])
        si_ref[:, ls] = ei[last] + (pr * gi[last] + pi * gr[last])

        for t in range(S5_SEG):
            rows = slice(t * SUBLANES, (t + 1) * SUBLANES)
            pr, pi = pcr_ref[t:t + 1, ls], pci_ref[t:t + 1, ls]
            hr_ref[rows, ls] = hr_ref[rows, ls] + (pr * gr - pi * gi)
            hi_ref[rows, ls] = hi_ref[rows, ls] + (pr * gi + pi * gr)

    y = jnp.concatenate(
        [jnp.dot(hr_ref[:, q * sw:(q + 1) * sw].astype(BF16), cre_ref[q], preferred_element_type=F32)
         + jnp.dot(hi_ref[:, q * sw:(q + 1) * sw].astype(BF16), cim_ref[q], preferred_element_type=F32)
         for q in range(n_slabs)], axis=1)
    y = y + d_ref[...] * u
    g = 0.5 * y * (1.0 + jnp.tanh(math.sqrt(2.0 / math.pi) * (y + 0.044715 * (y * y * y))))
    out = g * _sigmoid(_dot(g, wglu_ref[...]) + bglu_ref[...])
    for q in range(n_slabs):
        op_ref[q] = out[:, q * LANES:(q + 1) * LANES]
    for r in range(SUBLANES):
        o_ref[r * S5_SEG:(r + 1) * S5_SEG, :] = jnp.concatenate(
            [op_ref[q, pl.ds(r, S5_SEG, stride=SUBLANES), :] for q in range(n_slabs)], axis=1).astype(ACT)


def _s5(proj, batch, bmat, cre, cim, a1r, a1i, pwr, pwi, pcr, pci, d, wglu, bglu):
    t = proj.shape[0]
    nb = t // batch // S5_BLOCK
    full2 = lambda a: pl.BlockSpec(a.shape, lambda b, i: (0,) * a.ndim)
    ucol = COL_US5 // W_GROUP
    return pl.pallas_call(
        _s5_kernel,
        grid=(batch, nb),
        in_specs=[pl.BlockSpec((S5_BLOCK, W_GROUP), lambda b, i: (b * nb + i, ucol))]
        + [full2(a) for a in (bmat, cre, cim, a1r, a1i, pwr, pwi, pcr, pci, d, wglu, bglu)],
        out_specs=pl.BlockSpec((S5_BLOCK, W_GROUP), lambda b, i: (b * nb + i, 0)),
        out_shape=jax.ShapeDtypeStruct((t, W_GROUP), ACT),
        scratch_shapes=[pltpu.VMEM((S5_BLOCK, S5_WIDTH), F32), pltpu.VMEM((S5_BLOCK, S5_WIDTH), F32),
                        pltpu.VMEM((1, S5_WIDTH), F32), pltpu.VMEM((1, S5_WIDTH), F32),
                        pltpu.VMEM((W_GROUP // LANES, S5_BLOCK, LANES), F32), pltpu.VMEM((S5_BLOCK, W_GROUP), F32),
                        pltpu.VMEM((W_GROUP // LANES, S5_BLOCK, LANES), F32)],
        compiler_params=_cparams("parallel", "arbitrary"),
        name="s5",
    )(proj, bmat, cre, cim, a1r, a1i, pwr, pwi, pcr, pci, d, wglu, bglu)


def _rwkv_kernel(r_ref, k_ref, v_ref, lo_ref, mu_r_ref, mu_k_ref, mu_v_ref, mu_lo_ref, w0_ref, w2_ref, a0_ref,
                 a2_ref, kk_ref, ka_ref, rk_ref, lng_ref, lnb_ref, ones_ref,
                 o_ref,
                 st_ref, pr_ref, pk_ref, pv_ref, plo_ref,
                 rt_ref, at_ref, bt_ref, kt_ref, bg_ref, kg_ref, vv_ref, ge_ref, oo_ref,
                 qc_ref, ec_ref, mc_ref, dc_ref):
    tb, lc = RWKV_BLOCK, RWKV_CHUNK

    @pl.when(pl.program_id(1) == 0)
    def _():
        st_ref[...] = jnp.zeros(st_ref.shape, F32)
        pr_ref[...] = jnp.zeros(pr_ref.shape, F32)
        pk_ref[...] = jnp.zeros(pk_ref.shape, F32)
        pv_ref[...] = jnp.zeros(pv_ref.shape, F32)
        plo_ref[...] = jnp.zeros(plo_ref.shape, F32)

    def shifted(z_ref, prev_ref, mu_ref):
        z = z_ref[...].astype(F32)
        first = lax.broadcasted_iota(jnp.int32, z.shape, 0) == 0
        z_prev = jnp.where(first, prev_ref[...], pltpu.roll(z, 1, axis=0))
        prev_ref[...] = z[tb - 1:tb, :]
        return z + (z_prev - z) * mu_ref[...]

    r = shifted(r_ref, pr_ref, mu_r_ref)
    k = shifted(k_ref, pk_ref, mu_k_ref)
    v = shifted(v_ref, pv_ref, mu_v_ref)
    lo = shifted(lo_ref, plo_ref, mu_lo_ref)

    ones = ones_ref[...]
    wx = -(w0_ref[...] + _dot(jnp.tanh(lo), w2_ref[...]))
    w_log = -(jnp.maximum(wx, 0.0) + jnp.log(1.0 + jnp.exp(-jnp.abs(wx)))) - 0.5
    lw = -jnp.exp(w_log)
    a = _sigmoid(a0_ref[...] + _dot(lo, a2_ref[...]))
    kk = k * kk_ref[...]
    kk = kk / jnp.maximum(jnp.sqrt(_dot_split(kk * kk, ones)), 1e-12)
    k2 = k * (1.0 + (a - 1.0) * ka_ref[...])
    aa = -kk
    bb = kk * a

    row = lax.broadcasted_iota(jnp.int32, (lc, lc), 0)
    col = lax.broadcasted_iota(jnp.int32, (lc, lc), 1)
    tri = (row >= col).astype(BF16)
    for c in range(tb // lc):
        rs = slice(c * lc, (c + 1) * lc)
        lwc = lw[rs]
        cum = _dot_split3(tri, lwc)
        cum_last = cum[lc - 1:lc, :]
        e_neg = jnp.exp(-cum)
        e_end = jnp.exp(cum_last - cum)
        rt_ref[rs, :] = r[rs] * jnp.exp(cum)
        at_ref[rs, :] = aa[rs] * jnp.exp(cum - lwc)
        bt_ref[rs, :] = bb[rs] * e_neg
        kt_ref[rs, :] = k2[rs] * e_neg
        bg_ref[rs, :] = bb[rs] * e_end
        kg_ref[rs, :] = k2[rs] * e_end
        ge_ref[c] = jnp.broadcast_to(jnp.exp(cum_last), (SUBLANES, W_GROUP))
    vv_ref[...] = v

    lane = lax.broadcasted_iota(jnp.int32, (1, LANES), 1)
    m0 = (lane < RWKV_HEAD).astype(F32)
    m1 = 1.0 - m0
    r2 = lax.broadcasted_iota(jnp.int32, (2 * lc, 2 * lc), 0)
    c2 = lax.broadcasted_iota(jnp.int32, (2 * lc, 2 * lc), 1)
    same = jnp.right_shift(r2, 6) == jnp.right_shift(c2, 6)
    assert lc == 64
    strict = jnp.where(same & (r2 > c2), 1.0, 0.0).astype(F32)
    incl = jnp.where(same & (r2 >= c2), 1.0, 0.0).astype(F32)
    eye = jnp.where(r2 == c2, 1.0, 0.0).astype(F32)
    n_double = int(math.log2(lc)) - 1

    def stack(x):
        return jnp.concatenate([x * m0, x * m1], axis=0)

    pairs = range(RWKV_HEADS // 2)
    lsl = [slice(LANES * p, LANES * (p + 1)) for p in pairs]
    n_chunks = tb // lc
    dot_nt = lambda x, y: lax.dot_general(x, y, (((1,), (1,)), ((), ())), preferred_element_type=F32)
    dot_tn = lambda x, y: lax.dot_general(x, y, (((0,), (0,)), ((), ())), preferred_element_type=F32)
    dot_nn = lambda x, y: jnp.dot(x, y, preferred_element_type=F32)
    bf = lambda xs: [x.astype(BF16) for x in xs]

    def chunk_terms(chunks):
        chains = [(c, p) for c in chunks for p in range(RWKV_HEADS // 2)]
        pairs = range(len(chains))
        tile = lambda ref: [stack(ref[c * lc:(c + 1) * lc, LANES * p:LANES * (p + 1)]) for c, p in chains]
        a_s = bf(tile(at_ref))
        r_f = tile(rt_ref)
        b_s = bf(tile(bt_ref))
        k_s = bf(tile(kt_ref))
        v_s = bf(tile(vv_ref))
        bg_s = bf(tile(bg_ref))
        kg_s = bf(tile(kg_ref))
        big = [dot_nt(jnp.concatenate([a_s[p], r_f[p].astype(BF16)], axis=0),
                      jnp.concatenate([b_s[p], k_s[p]], axis=0)) for p in pairs]
        nil = [big[p][:2 * lc, :2 * lc] * strict for p in pairs]
        a_ak = bf([big[p][:2 * lc, 2 * lc:] * strict for p in pairs])
        a_rb = bf([big[p][2 * lc:, :2 * lc] * incl for p in pairs])
        a_rk = bf([big[p][2 * lc:, 2 * lc:] * incl for p in pairs])
        akv = bf([dot_nn(a_ak[p], v_s[p]) for p in pairs])
        inv = [eye + nil[p] for p in pairs]
        nb = bf(nil)
        nil = [dot_nn(nb[p], nb[p]) for p in pairs]
        for step in range(1, n_double):
            nb = bf(nil)
            prod = [dot_nn(nb[p], jnp.concatenate([nb[p], inv[p].astype(BF16)], axis=1)) for p in pairs]
            nil = [prod[p][:, :2 * lc] for p in pairs]
            inv = [inv[p] + prod[p][:, 2 * lc:] for p in pairs]
        ib = bf(inv)
        inv = [inv[p] + dot_nn(nil[p].astype(BF16), ib[p]) for p in pairs]
        ib = bf(inv)
        tatv = bf([dot_nn(ib[p], jnp.concatenate([a_s[p], akv[p]], axis=1)) for p in pairs])
        qe = [dot_nn(a_rb[p], tatv[p]) for p in pairs]
        qc = [r_f[p] + qe[p][:, :2 * lc] for p in pairs]
        ec = [qe[p][:, 2 * lc:] + dot_nn(a_rk[p], v_s[p]) for p in pairs]
        md = [dot_tn(tatv[p], bg_s[p]) for p in pairs]
        mc = [md[p][:2 * lc] for p in pairs]
        dc = [md[p][2 * lc:] + dot_tn(v_s[p], kg_s[p]) for p in pairs]
        for j, (c, p) in enumerate(chains):
            idx = c * (RWKV_HEADS // 2) + p
            qc_ref[idx] = qc[j].astype(BF16)
            ec_ref[idx] = ec[j]
            mc_ref[idx] = mc[j].astype(BF16)
            dc_ref[idx] = dc[j]

    for c0 in range(0, n_chunks, RWKV_CHUNKS_INTERLEAVED):
        chunk_terms(range(c0, c0 + RWKV_CHUNKS_INTERLEAVED))

    state = [st_ref[p] for p in pairs]
    for c in range(n_chunks):
        sb = bf(state)
        os_ = [dot_nt(qc_ref[c * len(pairs) + p], sb[p]) + ec_ref[c * len(pairs) + p] for p in pairs]
        state = [state[p] * ge_ref[c, 0:1, lsl[p]] + dot_nn(sb[p], mc_ref[c * len(pairs) + p])
                 + dc_ref[c * len(pairs) + p] for p in pairs]
        for p in pairs:
            oo_ref[c * lc:(c + 1) * lc, lsl[p]] = os_[p][:lc] + os_[p][lc:]
    for p in pairs:
        st_ref[p] = state[p]

    o = oo_ref[...]
    inv_n = 1.0 / RWKV_HEAD
    mean = _dot_split(o, ones) * inv_n
    oc = o - mean
    var = _dot_split(oc * oc, ones) * inv_n
    o = oc * lax.rsqrt(var + RWKV_GN_EPS) * lng_ref[...] + lnb_ref[...]
    bonus = _dot_split(r * k2 * rk_ref[...], ones) * v
    o_ref[...] = (o + bonus).astype(ACT)


def _rwkv(proj, batch, mu_r, mu_k, mu_v, mu_lo, w0, w2p, a0, a2p, k_k, k_a, r_k, ln_g, ln_b, ones):
    t = proj.shape[0]
    tb = RWKV_BLOCK
    nb = t // batch // tb
    seg = lambda col: pl.BlockSpec((tb, W_GROUP), lambda b, i: (b * nb + i, col // W_GROUP))
    full2 = lambda a: pl.BlockSpec(a.shape, lambda b, i: (0,) * a.ndim)
    params = (mu_r, mu_k, mu_v, mu_lo, w0, w2p, a0, a2p, k_k, k_a, r_k, ln_g, ln_b, ones)
    buf = pltpu.VMEM((tb, W_GROUP), F32)
    term = (tb // RWKV_CHUNK * (RWKV_HEADS // 2), LANES, LANES)
    return pl.pallas_call(
        _rwkv_kernel,
        grid=(batch, nb),
        in_specs=[seg(COL_R), seg(COL_K), seg(COL_V),
                  pl.BlockSpec((tb, LANES), lambda b, i: (b * nb + i, COL_LORA // LANES))]
        + [full2(a) for a in params],
        out_specs=pl.BlockSpec((tb, W_GROUP), lambda b, i: (b * nb + i, 0)),
        out_shape=jax.ShapeDtypeStruct((t, W_GROUP), ACT),
        scratch_shapes=[pltpu.VMEM((RWKV_HEADS // 2, LANES, LANES), F32),
                        pltpu.VMEM((1, W_GROUP), F32), pltpu.VMEM((1, W_GROUP), F32),
                        pltpu.VMEM((1, W_GROUP), F32), pltpu.VMEM((1, LANES), F32),
                        buf, buf, buf, buf, buf, buf, buf,
                        pltpu.VMEM((tb // RWKV_CHUNK, SUBLANES, W_GROUP), F32), buf,
                        pltpu.VMEM(term, BF16), pltpu.VMEM(term, F32), pltpu.VMEM(term, BF16), pltpu.VMEM(term, F32)],
        compiler_params=_cparams("parallel", "arbitrary"),
        name="rwkv",
    )(proj, proj, proj, proj, *params)


def _rope_tables(positions, rot, period, scale):
    half = rot // 2
    inv = ROPE_THETA ** (-jnp.arange(0, rot, 2, dtype=F32) / rot)
    ang = positions.reshape(-1).astype(F32)[:, None] * inv
    cos, sin = jnp.cos(ang), jnp.sin(ang)
    t = ang.shape[0]
    passthrough = period - rot
    c = jnp.concatenate([cos, cos, jnp.ones((t, passthrough), F32)], axis=1)
    sa = jnp.concatenate([-sin, jnp.zeros((t, half + passthrough), F32)], axis=1)
    sb = jnp.concatenate([jnp.zeros((t, half), F32), sin, jnp.zeros((t, passthrough), F32)], axis=1)
    reps = LANES // period
    return tuple(jnp.tile(a, (1, reps)) * scale for a in (c, sa, sb))


def _permute_w_in(w_in):
    d = w_in.shape[0]
    w_in = w_in.astype(BF16)
    sizes = (MLA_Q_LORA, MLA_KV_LORA, MLA_ROPE, W_GROUP, W_GROUP, W_GROUP, W_GROUP, RWKV_LORA, RWKV_LORA,
             W_GROUP, W_GROUP, W_GROUP, D_MODEL)
    pts = [int(p) for p in np.cumsum(sizes)[:-1]]
    cq, ckv, krope, us5, r, k, v, wlo, alo, qd, kd, vd, gate = jnp.split(w_in, pts, axis=1)
    out = jnp.concatenate([gate, cq, us5, qd, kd, vd, r, k, v, ckv, krope, jnp.zeros((d, LANES - MLA_ROPE), w_in.dtype),
                           wlo, alo], axis=1)
    assert out.shape[1] == N_PROJ
    return out.astype(BF16)


def _s5_tables(a_re, a_im, log_dt, b_re, b_im, c_re, c_im):
    lr = jnp.minimum(a_re, -1e-4)
    li = a_im
    dt = jnp.exp(log_dt)[:, None]
    mag = jnp.exp(dt * lr)
    ab_re, ab_im = mag * jnp.cos(dt * li), mag * jnp.sin(dt * li)
    den = lr * lr + li * li
    nr, ni = ab_re - 1.0, ab_im
    f_re = (nr * lr + ni * li) / den
    f_im = (ni * lr - nr * li) / den
    bb_re = f_re[..., None] * b_re - f_im[..., None] * b_im
    bb_im = f_re[..., None] * b_im + f_im[..., None] * b_re
    eye = jnp.eye(S5_GROUPS, dtype=F32)
    blk_in = lambda m: jnp.einsum('gpc,gh->gchp', m, eye).reshape(W_GROUP, S5_WIDTH)
    blk_out = lambda m: jnp.einsum('gcp,gh->gphc', m, eye).reshape(S5_WIDTH, W_GROUP)
    n_slabs = W_GROUP // LANES
    sw = S5_WIDTH // n_slabs
    diag_in = lambda m: jnp.stack([m[q * LANES:(q + 1) * LANES, q * sw:(q + 1) * sw] for q in range(n_slabs)])
    diag_out = lambda m: jnp.stack([m[q * sw:(q + 1) * sw, q * LANES:(q + 1) * LANES] for q in range(n_slabs)])
    bmat = jnp.concatenate([diag_in(blk_in(bb_re)), diag_in(blk_in(bb_im))], axis=2).astype(BF16)
    cre = diag_out(blk_out(c_re)).astype(BF16)
    cim = diag_out(blk_out(-c_im)).astype(BF16)

    def power(n):
        m = jnp.exp(n * dt * lr)
        return m * jnp.cos(n * dt * li), m * jnp.sin(n * dt * li)

    rows = jnp.arange(SUBLANES, dtype=F32)[:, None, None]
    a1r, a1i = power(jnp.ones((SUBLANES, 1, 1), F32))
    pwr, pwi = [], []
    for s in (1, 2, 4):
        pr, pi = power(jnp.full((1, 1, 1), float(s * S5_SEG), F32))
        keep = (rows >= s).astype(F32)
        pwr.append((keep * pr).reshape(SUBLANES, S5_WIDTH))
        pwi.append((keep * pi).reshape(SUBLANES, S5_WIDTH))
    pcr, pci = power(jnp.arange(1, S5_SEG + 1, dtype=F32)[:, None, None])
    return (bmat, cre, cim, a1r.reshape(SUBLANES, S5_WIDTH), a1i.reshape(SUBLANES, S5_WIDTH),
            jnp.stack(pwr), jnp.stack(pwi), pcr.reshape(S5_SEG, S5_WIDTH), pci.reshape(S5_SEG, S5_WIDTH))


def kernel(x, positions, norm_g, w_in, w_out, mla_q_norm_g, mla_kv_norm_g, mla_w_uq, mla_w_ukv, s5_a_re, s5_a_im, s5_log_dt, s5_b_re, s5_b_im, s5_c_re, s5_c_im, s5_d, s5_w_glu, s5_b_glu, rwkv_mu, rwkv_w0, rwkv_w2, rwkv_a0, rwkv_a2, rwkv_k_k, rwkv_k_a, rwkv_r_k, rwkv_ln_g, rwkv_ln_b, diff_lq1, diff_lk1, diff_lq2, diff_lk2, diff_subln_g, final_norm_g):
    batch, seq, d = x.shape
    depth = w_in.shape[0]
    t = batch * seq
    assert d == D_MODEL and seq % TQ == 0 and t % TM_INPROJ == 0
    assert seq % S5_BLOCK == 0 and seq % RWKV_BLOCK == 0

    ca, saa, sba = _rope_tables(positions, MLA_ROPE, LANES, 1.0)
    zero_hi = (jnp.arange(LANES) < MLA_ROPE).astype(F32)[None, :]
    ca = ca * zero_hi
    cd, sad, sbd = _rope_tables(positions, DIFF_ROT, DIFF_QK, 1.0)

    head_ones = jnp.kron(jnp.eye(RWKV_HEADS, dtype=F32), jnp.ones((RWKV_HEAD, RWKV_HEAD), F32)).astype(BF16)
    row = lambda a: a.reshape(1, -1).astype(F32)

    def layer_params(norm_g, qg, kvg, w_uq, w_ukv, a_re, a_im, log_dt, b_re, b_im, c_re, c_im, s5_d,
                     w_glu, b_glu, mu, w0, w2, a0, a2, k_k, k_a, r_k, ln_g, ln_b, lq1, lk1, lq2, lk2, subln_g):
        wq = w_uq.reshape(MLA_Q_LORA, MLA_HEADS, MLA_NOPE + MLA_ROPE)
        wq = jnp.pad(wq, ((0, 0), (0, 0), (0, 2 * LANES - MLA_NOPE - MLA_ROPE))).reshape(MLA_Q_LORA, -1)
        wkv = w_ukv.reshape(MLA_KV_LORA, MLA_HEADS, MLA_NOPE + MLA_V)
        wkv = jnp.concatenate([wkv[:, :, :MLA_NOPE].reshape(MLA_KV_LORA, -1),
                               wkv[:, :, MLA_NOPE:].reshape(MLA_KV_LORA, -1)], axis=1)
        zpad = jnp.zeros((RWKV_LORA, W_GROUP), F32)
        return dict(
            norm_g=row(norm_g),
            mla=(row(qg), row(kvg), wq.astype(BF16), wkv.astype(BF16)),
            s5=_s5_tables(a_re, a_im, log_dt, b_re, b_im, c_re, c_im) + (row(s5_d), w_glu.astype(BF16), row(b_glu)),
            rwkv=(row(mu[:W_GROUP]), row(mu[W_GROUP:2 * W_GROUP]), row(mu[2 * W_GROUP:3 * W_GROUP]),
                  row(mu[3 * W_GROUP:]), row(w0),
                  jnp.concatenate([w2, zpad], axis=0).astype(BF16),
                  row(a0),
                  jnp.concatenate([zpad, a2], axis=0).astype(BF16),
                  row(k_k), row(k_a), row(r_k), row(ln_g), row(ln_b)),
            lam=jnp.exp(jnp.sum(lq1 * lk1)) - jnp.exp(jnp.sum(lq2 * lk2)),
            subln_g=row(subln_g),
        )

    params = jax.vmap(layer_params)(
        norm_g, mla_q_norm_g, mla_kv_norm_g, mla_w_uq, mla_w_ukv, s5_a_re, s5_a_im, s5_log_dt, s5_b_re,
        s5_b_im, s5_c_re, s5_c_im, s5_d, s5_w_glu, s5_b_glu, rwkv_mu, rwkv_w0, rwkv_w2, rwkv_a0, rwkv_a2, rwkv_k_k,
        rwkv_k_a, rwkv_r_k, rwkv_ln_g, rwkv_ln_b, diff_lq1, diff_lk1, diff_lq2, diff_lk2, diff_subln_g)

    xf = x.reshape(t, d)
    for l in range(depth):
        p = jax.tree.map(lambda a: a[l], params)
        proj = _inproj(xf, p["norm_g"], _permute_w_in(w_in[l]))

        qa, ka, va = _mla_prep(proj, *p["mla"], ca, saa, sba)
        y_a = _attention([qa], ka, va, batch, MLA_HEADS, 2 * LANES, MLA_V, name="mla_attn")

        y_b = _s5(proj, batch, *p["s5"])

        y_c = _rwkv(proj, batch, *p["rwkv"], head_ones)

        lam_init = 0.8 - 0.6 * math.exp(-0.3 * l)
        q1, q2, kd, vd = _diff_prep(proj, cd, sad, sbd)
        y_d = _attention([q1, q2], kd, vd, batch, DIFF_HEADS, LANES, DIFF_V,
                         extra=((p["lam"] + lam_init).reshape(1, 1).astype(F32), p["subln_g"]),
                         out_scale=1.0 - lam_init, name="diff_attn")

        xf = _outproj(y_a, y_b, y_c, y_d, proj, xf, w_out[l].astype(BF16), row(final_norm_g), l == depth - 1)
    return xf.reshape(batch, seq, d)
```

```python
import functools
import math

import jax
import jax.numpy as jnp
import numpy as np
from jax import lax
from jax.experimental import pallas as pl
from jax.experimental.pallas import tpu as pltpu

F32 = jnp.float32
BF16 = jnp.bfloat16
ACT = jnp.bfloat16

D_MODEL = 2048
W_GROUP = 512
ROPE_THETA = 500000.0
NORM_EPS = 1e-6
MLA_HEADS, MLA_NOPE, MLA_ROPE, MLA_V = 4, 128, 64, 128
MLA_Q_LORA, MLA_KV_LORA = 512, 256
S5_GROUP, S5_GROUPS, S5_STATE = 16, 32, 64
S5_WIDTH = S5_GROUPS * S5_STATE
RWKV_HEAD, RWKV_HEADS = 64, 8
RWKV_LORA = 64
RWKV_GN_EPS = 64e-5
DIFF_HEADS, DIFF_QK, DIFF_V, DIFF_ROT = 4, 64, 128, 16
DIFF_SUBLN_EPS = 1e-5

LANES = 128
SUBLANES = 8
VMEM_LIMIT_BYTES = 56 * 1024 * 1024

COL_GATE = 0
COL_CQ = 2048
COL_US5 = 2560
COL_QD = 3072
COL_KD = 3584
COL_VD = 4096
COL_R = 4608
COL_K = 5120
COL_V = 5632
COL_CKV = 6144
COL_KROPE = 6400
COL_LORA = 6528
N_PROJ = 6656

TM_INPROJ = 1024
TN_INPROJ = 1664
TM_OUTPROJ = 512
TM_PREP = 512
TQ = 512
KS = 256
S5_BLOCK = 512
S5_SEG = S5_BLOCK // 8
S5_LANE_CHUNK = 512
RWKV_BLOCK = 256
RWKV_CHUNK = 64
RWKV_CHUNKS_INTERLEAVED = 2
NEG_BIG = -1e30
LOG2E = math.log2(math.e)


def _cparams(*sem):
    return pltpu.CompilerParams(dimension_semantics=sem, vmem_limit_bytes=VMEM_LIMIT_BYTES)


def _dot(a, b):
    return jnp.dot(a.astype(BF16), b.astype(BF16), preferred_element_type=F32)


def _dot_nt(a, b):
    return lax.dot_general(a.astype(BF16), b.astype(BF16), (((1,), (1,)), ((), ())), preferred_element_type=F32)


def _dot_split(x, w):
    hi = x.astype(BF16)
    lo = (x - hi.astype(F32)).astype(BF16)
    return jnp.dot(hi, w, preferred_element_type=F32) + jnp.dot(lo, w, preferred_element_type=F32)


def _dot_split3(w, x):
    hi = x.astype(BF16)
    r1 = x - hi.astype(F32)
    mid = r1.astype(BF16)
    lo = (r1 - mid.astype(F32)).astype(BF16)
    return (jnp.dot(w, hi, preferred_element_type=F32) + jnp.dot(w, mid, preferred_element_type=F32)
            + jnp.dot(w, lo, preferred_element_type=F32))


def _sigmoid(x):
    return 1.0 / (1.0 + jnp.exp(-x))


_W_IN_SEGMENTS = ((COL_GATE, 4544, D_MODEL), (COL_CQ, 0, W_GROUP), (COL_US5, 832, W_GROUP), (COL_QD, 3008, W_GROUP),
                  (COL_KD, 3520, W_GROUP), (COL_VD, 4032, W_GROUP), (COL_R, 1344, W_GROUP), (COL_K, 1856, W_GROUP),
                  (COL_V, 2368, W_GROUP), (COL_CKV, 512, MLA_KV_LORA), (COL_KROPE, 768, MLA_ROPE),
                  (COL_LORA, 2880, 2 * RWKV_LORA))
W_PREP_SUB = 4


def _w_in_block_table():
    src_blk, shift, zero_hi = [], [], []
    for ob in range(N_PROJ // LANES):
        col = ob * LANES
        out0, src0, width = next(s for s in _W_IN_SEGMENTS if s[0] <= col < s[0] + max(s[2], LANES))
        src = src0 + (col - out0)
        src_blk.append(src // LANES)
        shift.append((src % LANES) // (LANES // 2))
        zero_hi.append(int(width < LANES))
    return (np.asarray(src_blk, np.int32), np.asarray(shift, np.int32), np.asarray(zero_hi, np.int32))


def _w_prep_kernel(blk_ref, shift_ref, zero_ref, *refs):
    in_refs, o_ref = refs[:-1], refs[-1]
    c = pl.program_id(1)
    lower = lax.broadcasted_iota(jnp.int32, (1, LANES), 1) < LANES // 2
    for sub in range(W_PREP_SUB):
        a_ref, b_ref = in_refs[2 * sub], in_refs[2 * sub + 1]
        ob = c * W_PREP_SUB + sub
        cols = slice(sub * LANES, (sub + 1) * LANES)

        @pl.when(shift_ref[ob] == 1)
        def _():
            o_ref[0, :, cols] = jnp.where(lower, pltpu.roll(a_ref[0], LANES // 2, axis=1),
                                          pltpu.roll(b_ref[0], LANES // 2, axis=1)).astype(BF16)

        @pl.when((shift_ref[ob] == 0) & (zero_ref[ob] == 0))
        def _():
            o_ref[0, :, cols] = a_ref[0].astype(BF16)

        @pl.when((shift_ref[ob] == 0) & (zero_ref[ob] == 1))
        def _():
            o_ref[0, :, cols] = jnp.where(lower, a_ref[0], 0.0).astype(BF16)


def _w_prep(w_in):
    depth, d, n_in = w_in.shape
    src_blk, shift, zero_hi = _w_in_block_table()
    last = (n_in - 1) // LANES

    def in_spec(sub, nxt):
        return pl.BlockSpec((1, d, LANES), lambda l, c, blk, sh, zh: (
            l, 0, jnp.minimum(blk[c * W_PREP_SUB + sub] + nxt, last)))

    return pl.pallas_call(
        _w_prep_kernel,
        grid_spec=pltpu.PrefetchScalarGridSpec(
            num_scalar_prefetch=3,
            grid=(depth, N_PROJ // (LANES * W_PREP_SUB)),
            in_specs=[in_spec(sub, nxt) for sub in range(W_PREP_SUB) for nxt in (0, 1)],
            out_specs=pl.BlockSpec((1, d, LANES * W_PREP_SUB), lambda l, c, blk, sh, zh: (l, 0, c)),
        ),
        out_shape=jax.ShapeDtypeStruct((depth, d, N_PROJ), BF16),
        compiler_params=_cparams("parallel", "parallel"),
        name="w_prep",
    )(jnp.asarray(src_blk), jnp.asarray(shift), jnp.asarray(zero_hi), *([w_in] * (2 * W_PREP_SUB)))


def _inproj_kernel(x_ref, g_ref, w_ref, o_ref, h_ref):
    @pl.when(pl.program_id(1) == 0)
    def _():
        x = x_ref[...]
        y = x * lax.rsqrt(jnp.mean(x * x, axis=-1, keepdims=True) + NORM_EPS)
        h_ref[...] = (y * g_ref[...]).astype(BF16)

    o_ref[...] = jnp.dot(h_ref[...], w_ref[0], preferred_element_type=F32).astype(ACT)


def _inproj(x, g, w_all, layer):
    t, d = x.shape
    n = w_all.shape[2]
    return pl.pallas_call(
        _inproj_kernel,
        grid=(t // TM_INPROJ, n // TN_INPROJ),
        in_specs=[pl.BlockSpec((TM_INPROJ, d), lambda i, j: (i, 0)),
                  pl.BlockSpec((1, d), lambda i, j: (0, 0)),
                  pl.BlockSpec((1, d, TN_INPROJ), lambda i, j: (layer, 0, j))],
        out_specs=pl.BlockSpec((TM_INPROJ, TN_INPROJ), lambda i, j: (i, j)),
        out_shape=jax.ShapeDtypeStruct((t, n), ACT),
        scratch_shapes=[pltpu.VMEM((TM_INPROJ, d), BF16)],
        compiler_params=_cparams("parallel", "arbitrary"),
        name="inproj",
    )(x, g, w_all)


def _outproj_kernel(ya_ref, yb_ref, yc_ref, yd_ref, gate_ref, x_ref, w_ref, fg_ref, o_ref, *, final):
    y = jnp.concatenate([ya_ref[...], yb_ref[...], yc_ref[...], yd_ref[...]], axis=-1).astype(F32)
    gate = gate_ref[...].astype(F32)
    mixed = y * (gate * _sigmoid(gate))
    xn = x_ref[...] + _dot(mixed, w_ref[0])
    if final:
        xn = xn * lax.rsqrt(jnp.mean(xn * xn, axis=-1, keepdims=True) + NORM_EPS) * fg_ref[...]
    o_ref[...] = xn


def _outproj(ya, yb, yc, yd, proj, x, w_all, layer, fg, final):
    t, d = x.shape
    tm = TM_OUTPROJ
    yspec = pl.BlockSpec((tm, W_GROUP), lambda i: (i, 0))
    return pl.pallas_call(
        functools.partial(_outproj_kernel, final=final),
        grid=(t // tm,),
        in_specs=[yspec, yspec, yspec, yspec,
                  pl.BlockSpec((tm, d), lambda i: (i, COL_GATE // D_MODEL)),
                  pl.BlockSpec((tm, d), lambda i: (i, 0)),
                  pl.BlockSpec((1, d, d), lambda i: (layer, 0, 0)),
                  pl.BlockSpec((1, d), lambda i: (0, 0))],
        out_specs=pl.BlockSpec((tm, d), lambda i: (i, 0)),
        out_shape=jax.ShapeDtypeStruct((t, d), F32),
        compiler_params=_cparams("parallel"),
        name="outproj_final" if final else "outproj",
    )(ya, yb, yc, yd, proj, x, w_all, fg)


def _rope128(x, c, sa, sb, half):
    return x * c + pltpu.roll(x, LANES - half, axis=1) * sa + pltpu.roll(x, half, axis=1) * sb


def _store_vt_tiles(v, vt_ref):
    for n in range(v.shape[0] // KS):
        vt_ref[n] = v[n * KS:(n + 1) * KS, :].T.astype(BF16)


def _mla_prep_kernel(cq_ref, ckv_ref, kr_ref, qg_ref, kvg_ref, wq_ref, wkv_ref, c_ref, sa_ref, sb_ref,
                     q_ref, k_ref, vt_ref):
    scale = (MLA_NOPE + MLA_ROPE) ** -0.5 * LOG2E
    half = MLA_ROPE // 2
    c, sa, sb = c_ref[...], sa_ref[...], sb_ref[...]
    cq = cq_ref[...].astype(F32)
    hq = cq * lax.rsqrt(jnp.mean(cq * cq, axis=-1, keepdims=True) + NORM_EPS) * qg_ref[...]
    q = _dot(hq, wq_ref[...])
    ckv = ckv_ref[...].astype(F32)
    hkv = ckv * lax.rsqrt(jnp.mean(ckv * ckv, axis=-1, keepdims=True) + NORM_EPS) * kvg_ref[...]
    kv = _dot(hkv, wkv_ref[...])
    kpe = _rope128(kr_ref[...].astype(F32), c, sa, sb, half).astype(BF16)
    for h in range(MLA_HEADS):
        base = 2 * LANES * h
        q_ref[:, base:base + LANES] = (q[:, base:base + LANES] * scale).astype(BF16)
        qpe = _rope128(q[:, base + LANES:base + 2 * LANES], c, sa, sb, half) * scale
        q_ref[:, base + LANES:base + 2 * LANES] = qpe.astype(BF16)
        k_ref[:, base:base + LANES] = kv[:, LANES * h:LANES * (h + 1)].astype(BF16)
        k_ref[:, base + LANES:base + 2 * LANES] = kpe
    _store_vt_tiles(kv[:, MLA_HEADS * MLA_NOPE:], vt_ref)


def _mla_prep(proj, qg, kvg, wq, wkv, c, sa, sb):
    t = proj.shape[0]
    tm = TM_PREP
    tab = pl.BlockSpec((tm, LANES), lambda i: (i, 0))
    full = lambda shape: pl.BlockSpec(shape, lambda i: (0, 0))
    return pl.pallas_call(
        _mla_prep_kernel,
        grid=(t // tm,),
        in_specs=[pl.BlockSpec((tm, MLA_Q_LORA), lambda i: (i, COL_CQ // MLA_Q_LORA)),
                  pl.BlockSpec((tm, MLA_KV_LORA), lambda i: (i, COL_CKV // MLA_KV_LORA)),
                  pl.BlockSpec((tm, LANES), lambda i: (i, COL_KROPE // LANES)),
                  full((1, MLA_Q_LORA)), full((1, MLA_KV_LORA)),
                  full(wq.shape), full(wkv.shape), tab, tab, tab],
        out_specs=[pl.BlockSpec((tm, MLA_HEADS * 2 * LANES), lambda i: (i, 0)),
                   pl.BlockSpec((tm, MLA_HEADS * 2 * LANES), lambda i: (i, 0)),
                   pl.BlockSpec((tm // KS, MLA_HEADS * MLA_V, KS), lambda i: (i, 0, 0))],
        out_shape=[jax.ShapeDtypeStruct((t, MLA_HEADS * 2 * LANES), BF16),
                   jax.ShapeDtypeStruct((t, MLA_HEADS * 2 * LANES), BF16),
                   jax.ShapeDtypeStruct((t // KS, MLA_HEADS * MLA_V, KS), BF16)],
        compiler_params=_cparams("parallel"),
        name="mla_prep",
    )(proj, proj, proj, qg, kvg, wq, wkv, c, sa, sb)


def _attn_kernel(*refs, n_maps, out_scale):
    q_refs = refs[:n_maps]
    k_ref, vt_ref = refs[n_maps:n_maps + 2]
    if n_maps == 2:
        lam_ref, g_ref = refs[n_maps + 2:n_maps + 4]
        o_ref = refs[n_maps + 4]
        scratch = refs[n_maps + 5:]
    else:
        o_ref = refs[n_maps + 2]
        scratch = refs[n_maps + 3:]
    states = [scratch[3 * a:3 * a + 3] for a in range(n_maps)]
    s_slots = scratch[3 * n_maps:3 * n_maps + 2]
    nq = k_ref.shape[0] // TQ
    sub_per_q = TQ // KS
    i = pl.program_id(2)
    q_blocks = (i, nq - 1 - i)

    for m_ref, l_ref, acc_ref in states:
        m_ref[...] = jnp.full(m_ref.shape, NEG_BIG, F32)
        l_ref[...] = jnp.zeros(l_ref.shape, F32)
        acc_ref[...] = jnp.zeros(acc_ref.shape, F32)

    half = nq // 2
    items = [(0, q_blocks[0], True), (1, q_blocks[1], True)] + [(1, t, False) for t in range(half)]
    for t in range(half - 1):
        first = t < i
        items.append((jnp.where(first, 0, 1), jnp.where(first, t, half + t - i), False))

    def rows(blk, size):
        if isinstance(blk, int):
            return pl.ds(blk * size, size)
        return pl.ds(pl.multiple_of(blk * size, size), size)

    def scores_into(item, s_ref):
        sel, jq, _ = item
        qblk = q_blocks[sel] if isinstance(sel, int) else jnp.where(sel == 0, q_blocks[0], q_blocks[1])
        for sub in range(sub_per_q):
            k = k_ref[rows(jq * sub_per_q + sub, KS), :]
            for a, q_ref in enumerate(q_refs):
                s_ref[a, sub] = lax.dot_general(k, q_ref[rows(qblk, TQ), :], (((1,), (1,)), ((), ())),
                                                preferred_element_type=F32)

    def consume(item, s_ref):
        sel, jq, diagonal = item
        for sub in range(sub_per_q):
            vt = vt_ref[jq * sub_per_q + sub]
            for a, (m_ref, l_ref, acc_ref) in enumerate(states):
                s = s_ref[a, sub]
                if diagonal:
                    key = lax.broadcasted_iota(jnp.int32, s.shape, 0) + sub * KS
                    qry = lax.broadcasted_iota(jnp.int32, s.shape, 1)
                    s = jnp.where(key <= qry, s, NEG_BIG)
                m_prev = m_ref[sel]
                m_new = jnp.maximum(m_prev, jnp.max(s, axis=0, keepdims=True))
                alpha = jnp.exp2(m_prev - m_new)
                p = jnp.exp2(s - m_new)
                l_ref[sel] = alpha * l_ref[sel] + jnp.sum(p.reshape(KS // SUBLANES, SUBLANES, TQ), axis=0)
                acc_ref[sel] = alpha * acc_ref[sel] + jnp.dot(vt, p.astype(BF16), preferred_element_type=F32)
                m_ref[sel] = m_new

    scores_into(items[0], s_slots[0])
    for n, item in enumerate(items):
        if n + 1 < len(items):
            scores_into(items[n + 1], s_slots[(n + 1) % 2])
        consume(item, s_slots[n % 2])

    for sel in range(2):
        outs = []
        for m_ref, l_ref, acc_ref in states:
            inv_l = 1.0 / jnp.sum(l_ref[sel], axis=0, keepdims=True)
            outs.append(acc_ref[sel] * inv_l)
        if n_maps == 2:
            ot = outs[0] - lam_ref[0, 0] * outs[1]
            ot = ot * lax.rsqrt(jnp.mean(ot * ot, axis=0, keepdims=True) + DIFF_SUBLN_EPS)
            o_ref[rows(q_blocks[sel], TQ), :] = (ot.T * (g_ref[...] * out_scale)).astype(ACT)
        else:
            o_ref[rows(q_blocks[sel], TQ), :] = outs[0].T.astype(ACT)


def _attention(qs, k, vt, batch, heads, dqk, dv, extra=(), out_scale=1.0, name="attn"):
    n_maps = len(qs)
    t = k.shape[0]
    seq = t // batch
    nq = seq // TQ
    assert nq % 2 == 0
    seq_spec = lambda width: pl.BlockSpec((seq, width), lambda b, h, i: (b, h))
    in_specs = [seq_spec(dqk)] * (n_maps + 1) + [pl.BlockSpec((seq // KS, dv, KS), lambda b, h, i: (b, h, 0))]
    if n_maps == 2:
        in_specs += [pl.BlockSpec(memory_space=pltpu.SMEM), pl.BlockSpec((1, dv), lambda b, h, i: (0, 0))]
    state = [pltpu.VMEM((2, 1, TQ), F32), pltpu.VMEM((2, SUBLANES, TQ), F32), pltpu.VMEM((2, dv, TQ), F32)]
    s_slot = pltpu.VMEM((n_maps, TQ // KS, KS, TQ), F32)
    return pl.pallas_call(
        functools.partial(_attn_kernel, n_maps=n_maps, out_scale=out_scale),
        grid=(batch, heads, nq // 2),
        in_specs=in_specs,
        out_specs=seq_spec(dv),
        out_shape=jax.ShapeDtypeStruct((t, heads * dv), ACT),
        scratch_shapes=state * n_maps + [s_slot, s_slot],
        compiler_params=_cparams("parallel", "parallel", "arbitrary"),
        name=name,
    )(*qs, k, vt, *extra)


def _diff_prep_kernel(q_ref, k_ref, v_ref, c_ref, sa_ref, sb_ref, q1_ref, q2_ref, ko_ref, vt_ref):
    half = DIFF_ROT // 2
    scale = DIFF_QK ** -0.5 * LOG2E
    lane = lax.broadcasted_iota(jnp.int32, (1, LANES), 1)
    m0 = jnp.where(lane < DIFF_QK, scale, 0.0).astype(F32)
    m1 = scale - m0
    c, sa, sb = c_ref[...], sa_ref[...], sb_ref[...]
    for h in range(DIFF_HEADS):
        sl = slice(LANES * h, LANES * (h + 1))
        q = _rope128(q_ref[:, sl].astype(F32), c, sa, sb, half)
        q1_ref[:, sl] = (q * m0).astype(BF16)
        q2_ref[:, sl] = (q * m1).astype(BF16)
        ko_ref[:, sl] = _rope128(k_ref[:, sl].astype(F32), c, sa, sb, half).astype(BF16)
    _store_vt_tiles(v_ref[...].astype(F32), vt_ref)


def _diff_prep(proj, c, sa, sb):
    t = proj.shape[0]
    tm = TM_PREP
    tab = pl.BlockSpec((tm, LANES), lambda i: (i, 0))
    seg = lambda col: pl.BlockSpec((tm, W_GROUP), lambda i: (i, col // W_GROUP))
    out = pl.BlockSpec((tm, W_GROUP), lambda i: (i, 0))
    shp = jax.ShapeDtypeStruct((t, W_GROUP), BF16)
    return pl.pallas_call(
        _diff_prep_kernel,
        grid=(t // tm,),
        in_specs=[seg(COL_QD), seg(COL_KD), seg(COL_VD), tab, tab, tab],
        out_specs=[out, out, out, pl.BlockSpec((tm // KS, W_GROUP, KS), lambda i: (i, 0, 0))],
        out_shape=[shp, shp, shp, jax.ShapeDtypeStruct((t // KS, W_GROUP, KS), BF16)],
        compiler_params=_cparams("parallel"),
        name="diff_prep",
    )(proj, proj, proj, c, sa, sb)


def _s5_kernel(u_ref, bmat_ref, cre_ref, cim_ref, a1r_ref, a1i_ref, pwr_ref, pwi_ref, pcr_ref, pci_ref, d_ref,
               wglu_ref, bglu_ref, o_ref, hr_ref, hi_ref, sr_ref, si_ref, uf_ref, up_ref, op_ref):
    @pl.when(pl.program_id(1) == 0)
    def _():
        sr_ref[...] = jnp.zeros(sr_ref.shape, F32)
        si_ref[...] = jnp.zeros(si_ref.shape, F32)

    n_slabs = W_GROUP // LANES
    uf = u_ref[...].astype(F32)
    for q in range(n_slabs):
        uf_ref[q] = uf[:, q * LANES:(q + 1) * LANES]
    for t in range(S5_SEG):
        for q in range(n_slabs):
            up_ref[t * SUBLANES:(t + 1) * SUBLANES, q * LANES:(q + 1) * LANES] = (
                uf_ref[q, pl.ds(t, SUBLANES, stride=S5_SEG), :])
    u = up_ref[...]
    ub = u.astype(BF16)
    sw = S5_WIDTH // n_slabs
    for q in range(n_slabs):
        bu = jnp.dot(ub[:, q * LANES:(q + 1) * LANES], bmat_ref[q], preferred_element_type=F32)
        hr_ref[:, q * sw:(q + 1) * sw] = bu[:, :sw]
        hi_ref[:, q * sw:(q + 1) * sw] = bu[:, sw:]

    first_row = lax.broadcasted_iota(jnp.int32, (SUBLANES, S5_LANE_CHUNK), 0) == 0
    for c in range(S5_WIDTH // S5_LANE_CHUNK):
        ls = slice(c * S5_LANE_CHUNK, (c + 1) * S5_LANE_CHUNK)
        ar, ai = a1r_ref[:, ls], a1i_ref[:, ls]

        def tile(t, carry):
            xr, xi = carry
            rows = pl.ds(pl.multiple_of(t * SUBLANES, SUBLANES), SUBLANES)
            xr, xi = hr_ref[rows, ls] + (ar * xr - ai * xi), hi_ref[rows, ls] + (ar * xi + ai * xr)
            hr_ref[rows, ls] = xr
            hi_ref[rows, ls] = xi
            return xr, xi

        zero = jnp.zeros((SUBLANES, S5_LANE_CHUNK), F32)
        er, ei = lax.fori_loop(0, S5_SEG, tile, (zero, zero), unroll=4)

        gr = jnp.where(first_row, sr_ref[:, ls], pltpu.roll(er, 1, axis=0))
        gi = jnp.where(first_row, si_ref[:, ls], pltpu.roll(ei, 1, axis=0))
        for s in range(3):
            pr, pi = pwr_ref[s, :, ls], pwi_ref[s, :, ls]
            rr = pltpu.roll(gr, 1 << s, axis=0)
            ri = pltpu.roll(gi, 1 << s, axis=0)
            gr, gi = gr + (pr * rr - pi * ri), gi + (pr * ri + pi * rr)
        last = slice(SUBLANES - 1, SUBLANES)
        pr, pi = pwr_ref[0, last, ls], pwi_ref[0, last, ls]
        sr_ref[:, ls] = er[last] + (pr * gr[last] - pi * gi[last])
        si_ref[:, ls] = ei[last] + (pr * gi[last] + pi * gr[last])

        for t in range(S5_SEG):
            rows = slice(t * SUBLANES, (t + 1) * SUBLANES)
            pr, pi = pcr_ref[t:t + 1, ls], pci_ref[t:t + 1, ls]
            hr_ref[rows, ls] = hr_ref[rows, ls] + (pr * gr - pi * gi)
            hi_ref[rows, ls] = hi_ref[rows, ls] + (pr * gi + pi * gr)

    y = jnp.concatenate(
        [jnp.dot(hr_ref[:, q * sw:(q + 1) * sw].astype(BF16), cre_ref[q], preferred_element_type=F32)
         + jnp.dot(hi_ref[:, q * sw:(q + 1) * sw].astype(BF16), cim_ref[q], preferred_element_type=F32)
         for q in range(n_slabs)], axis=1)
    y = y + d_ref[...] * u
    g = 0.5 * y * (1.0 + jnp.tanh(math.sqrt(2.0 / math.pi) * (y + 0.044715 * (y * y * y))))
    out = g * _sigmoid(_dot(g, wglu_ref[...]) + bglu_ref[...])
    for q in range(n_slabs):
        op_ref[q] = out[:, q * LANES:(q + 1) * LANES]
    for r in range(SUBLANES):
        o_ref[r * S5_SEG:(r + 1) * S5_SEG, :] = jnp.concatenate(
            [op_ref[q, pl.ds(r, S5_SEG, stride=SUBLANES), :] for q in range(n_slabs)], axis=1).astype(ACT)


def _s5(proj, batch, bmat, cre, cim, a1r, a1i, pwr, pwi, pcr, pci, d, wglu, bglu):
    t = proj.shape[0]
    nb = t // batch // S5_BLOCK
    full2 = lambda a: pl.BlockSpec(a.shape, lambda b, i: (0,) * a.ndim)
    ucol = COL_US5 // W_GROUP
    return pl.pallas_call(
        _s5_kernel,
        grid=(batch, nb),
        in_specs=[pl.BlockSpec((S5_BLOCK, W_GROUP), lambda b, i: (b * nb + i, ucol))]
        + [full2(a) for a in (bmat, cre, cim, a1r, a1i, pwr, pwi, pcr, pci, d, wglu, bglu)],
        out_specs=pl.BlockSpec((S5_BLOCK, W_GROUP), lambda b, i: (b * nb + i, 0)),
        out_shape=jax.ShapeDtypeStruct((t, W_GROUP), ACT),
        scratch_shapes=[pltpu.VMEM((S5_BLOCK, S5_WIDTH), F32), pltpu.VMEM((S5_BLOCK, S5_WIDTH), F32),
                        pltpu.VMEM((1, S5_WIDTH), F32), pltpu.VMEM((1, S5_WIDTH), F32),
                        pltpu.VMEM((W_GROUP // LANES, S5_BLOCK, LANES), F32), pltpu.VMEM((S5_BLOCK, W_GROUP), F32),
                        pltpu.VMEM((W_GROUP // LANES, S5_BLOCK, LANES), F32)],
        compiler_params=_cparams("parallel", "arbitrary"),
        name="s5",
    )(proj, bmat, cre, cim, a1r, a1i, pwr, pwi, pcr, pci, d, wglu, bglu)


def _rwkv_kernel(r_ref, k_ref, v_ref, lo_ref, mu_r_ref, mu_k_ref, mu_v_ref, mu_lo_ref, w0_ref, w2_ref, a0_ref,
                 a2_ref, kk_ref, ka_ref, rk_ref, lng_ref, lnb_ref, ones_ref,
                 o_ref,
                 st_ref, pr_ref, pk_ref, pv_ref, plo_ref,
                 rt_ref, at_ref, bt_ref, kt_ref, bg_ref, kg_ref, vv_ref, ge_ref, oo_ref,
                 qc_ref, ec_ref, mc_ref, dc_ref):
    tb, lc = RWKV_BLOCK, RWKV_CHUNK

    @pl.when(pl.program_id(1) == 0)
    def _():
        st_ref[...] = jnp.zeros(st_ref.shape, F32)
        pr_ref[...] = jnp.zeros(pr_ref.shape, F32)
        pk_ref[...] = jnp.zeros(pk_ref.shape, F32)
        pv_ref[...] = jnp.zeros(pv_ref.shape, F32)
        plo_ref[...] = jnp.zeros(plo_ref.shape, F32)

    def shifted(z_ref, prev_ref, mu_ref):
        z = z_ref[...].astype(F32)
        first = lax.broadcasted_iota(jnp.int32, z.shape, 0) == 0
        z_prev = jnp.where(first, prev_ref[...], pltpu.roll(z, 1, axis=0))
        prev_ref[...] = z[tb - 1:tb, :]
        return z + (z_prev - z) * mu_ref[...]

    r = shifted(r_ref, pr_ref, mu_r_ref)
    k = shifted(k_ref, pk_ref, mu_k_ref)
    v = shifted(v_ref, pv_ref, mu_v_ref)
    lo = shifted(lo_ref, plo_ref, mu_lo_ref)

    ones = ones_ref[...]
    wx = -(w0_ref[...] + _dot(jnp.tanh(lo), w2_ref[...]))
    w_log = -(jnp.maximum(wx, 0.0) + jnp.log(1.0 + jnp.exp(-jnp.abs(wx)))) - 0.5
    lw = -jnp.exp(w_log)
    a = _sigmoid(a0_ref[...] + _dot(lo, a2_ref[...]))
    kk = k * kk_ref[...]
    kk = kk / jnp.maximum(jnp.sqrt(_dot_split(kk * kk, ones)), 1e-12)
    k2 = k * (1.0 + (a - 1.0) * ka_ref[...])
    aa = -kk
    bb = kk * a

    row = lax.broadcasted_iota(jnp.int32, (lc, lc), 0)
    col = lax.broadcasted_iota(jnp.int32, (lc, lc), 1)
    tri = (row >= col).astype(BF16)
    for c in range(tb // lc):
        rs = slice(c * lc, (c + 1) * lc)
        lwc = lw[rs]
        cum = _dot_split3(tri, lwc)
        cum_last = cum[lc - 1:lc, :]
        e_neg = jnp.exp(-cum)
        e_end = jnp.exp(cum_last - cum)
        rt_ref[rs, :] = r[rs] * jnp.exp(cum)
        at_ref[rs, :] = aa[rs] * jnp.exp(cum - lwc)
        bt_ref[rs, :] = bb[rs] * e_neg
        kt_ref[rs, :] = k2[rs] * e_neg
        bg_ref[rs, :] = bb[rs] * e_end
        kg_ref[rs, :] = k2[rs] * e_end
        ge_ref[c] = jnp.broadcast_to(jnp.exp(cum_last), (SUBLANES, W_GROUP))
    vv_ref[...] = v

    lane = lax.broadcasted_iota(jnp.int32, (1, LANES), 1)
    m0 = (lane < RWKV_HEAD).astype(F32)
    m1 = 1.0 - m0
    r2 = lax.broadcasted_iota(jnp.int32, (2 * lc, 2 * lc), 0)
    c2 = lax.broadcasted_iota(jnp.int32, (2 * lc, 2 * lc), 1)
    same = jnp.right_shift(r2, 6) == jnp.right_shift(c2, 6)
    assert lc == 64
    strict = jnp.where(same & (r2 > c2), 1.0, 0.0).astype(F32)
    incl = jnp.where(same & (r2 >= c2), 1.0, 0.0).astype(F32)
    eye = jnp.where(r2 == c2, 1.0, 0.0).astype(F32)
    n_double = int(math.log2(lc)) - 1

    def stack(x):
        return jnp.concatenate([x * m0, x * m1], axis=0)

    pairs = range(RWKV_HEADS // 2)
    lsl = [slice(LANES * p, LANES * (p + 1)) for p in pairs]
    n_chunks = tb // lc
    dot_nt = lambda x, y: lax.dot_general(x, y, (((1,), (1,)), ((), ())), preferred_element_type=F32)
    dot_tn = lambda x, y: lax.dot_general(x, y, (((0,), (0,)), ((), ())), preferred_element_type=F32)
    dot_nn = lambda x, y: jnp.dot(x, y, preferred_element_type=F32)
    bf = lambda xs: [x.astype(BF16) for x in xs]

    def chunk_terms(chunks):
        chains = [(c, p) for c in chunks for p in range(RWKV_HEADS // 2)]
        pairs = range(len(chains))
        tile = lambda ref: [stack(ref[c * lc:(c + 1) * lc, LANES * p:LANES * (p + 1)]) for c, p in chains]
        a_s = bf(tile(at_ref))
        r_f = tile(rt_ref)
        b_s = bf(tile(bt_ref))
        k_s = bf(tile(kt_ref))
        v_s = bf(tile(vv_ref))
        bg_s = bf(tile(bg_ref))
        kg_s = bf(tile(kg_ref))
        big = [dot_nt(jnp.concatenate([a_s[p], r_f[p].astype(BF16)], axis=0),
                      jnp.concatenate([b_s[p], k_s[p]], axis=0)) for p in pairs]
        nil = [big[p][:2 * lc, :2 * lc] * strict for p in pairs]
        a_ak = bf([big[p][:2 * lc, 2 * lc:] * strict for p in pairs])
        a_rb = bf([big[p][2 * lc:, :2 * lc] * incl for p in pairs])
        a_rk = bf([big[p][2 * lc:, 2 * lc:] * incl for p in pairs])
        akv = bf([dot_nn(a_ak[p], v_s[p]) for p in pairs])
        inv = [eye + nil[p] for p in pairs]
        nb = bf(nil)
        nil = [dot_nn(nb[p], nb[p]) for p in pairs]
        for step in range(1, n_double):
            nb = bf(nil)
            prod = [dot_nn(nb[p], jnp.concatenate([nb[p], inv[p].astype(BF16)], axis=1)) for p in pairs]
            nil = [prod[p][:, :2 * lc] for p in pairs]
            inv = [inv[p] + prod[p][:, 2 * lc:] for p in pairs]
        ib = bf(inv)
        inv = [inv[p] + dot_nn(nil[p].astype(BF16), ib[p]) for p in pairs]
        ib = bf(inv)
        tatv = bf([dot_nn(ib[p], jnp.concatenate([a_s[p], akv[p]], axis=1)) for p in pairs])
        qe = [dot_nn(a_rb[p], tatv[p]) for p in pairs]
        qc = [r_f[p] + qe[p][:, :2 * lc] for p in pairs]
        ec = [qe[p][:, 2 * lc:] + dot_nn(a_rk[p], v_s[p]) for p in pairs]
        md = [dot_tn(tatv[p], bg_s[p]) for p in pairs]
        mc = [md[p][:2 * lc] for p in pairs]
        dc = [md[p][2 * lc:] + dot_tn(v_s[p], kg_s[p]) for p in pairs]
        for j, (c, p) in enumerate(chains):
            idx = c * (RWKV_HEADS // 2) + p
            qc_ref[idx] = qc[j].astype(BF16)
            ec_ref[idx] = ec[j]
            mc_ref[idx] = mc[j].astype(BF16)
            dc_ref[idx] = dc[j]

    for c0 in range(0, n_chunks, RWKV_CHUNKS_INTERLEAVED):
        chunk_terms(range(c0, c0 + RWKV_CHUNKS_INTERLEAVED))

    state = [st_ref[p] for p in pairs]
    for c in range(n_chunks):
        sb = bf(state)
        os_ = [dot_nt(qc_ref[c * len(pairs) + p], sb[p]) + ec_ref[c * len(pairs) + p] for p in pairs]
        state = [state[p] * ge_ref[c, 0:1, lsl[p]] + dot_nn(sb[p], mc_ref[c * len(pairs) + p])
                 + dc_ref[c * len(pairs) + p] for p in pairs]
        for p in pairs:
            oo_ref[c * lc:(c + 1) * lc, lsl[p]] = os_[p][:lc] + os_[p][lc:]
    for p in pairs:
        st_ref[p] = state[p]

    o = oo_ref[...]
    inv_n = 1.0 / RWKV_HEAD
    mean = _dot_split(o, ones) * inv_n
    oc = o - mean
    var = _dot_split(oc * oc, ones) * inv_n
    o = oc * lax.rsqrt(var + RWKV_GN_EPS) * lng_ref[...] + lnb_ref[...]
    bonus = _dot_split(r * k2 * rk_ref[...], ones) * v
    o_ref[...] = (o + bonus).astype(ACT)


def _rwkv(proj, batch, mu_r, mu_k, mu_v, mu_lo, w0, w2p, a0, a2p, k_k, k_a, r_k, ln_g, ln_b, ones):
    t = proj.shape[0]
    tb = RWKV_BLOCK
    nb = t // batch // tb
    seg = lambda col: pl.BlockSpec((tb, W_GROUP), lambda b, i: (b * nb + i, col // W_GROUP))
    full2 = lambda a: pl.BlockSpec(a.shape, lambda b, i: (0,) * a.ndim)
    params = (mu_r, mu_k, mu_v, mu_lo, w0, w2p, a0, a2p, k_k, k_a, r_k, ln_g, ln_b, ones)
    buf = pltpu.VMEM((tb, W_GROUP), F32)
    term = (tb // RWKV_CHUNK * (RWKV_HEADS // 2), LANES, LANES)
    return pl.pallas_call(
        _rwkv_kernel,
        grid=(batch, nb),
        in_specs=[seg(COL_R), seg(COL_K), seg(COL_V),
                  pl.BlockSpec((tb, LANES), lambda b, i: (b * nb + i, COL_LORA // LANES))]
        + [full2(a) for a in params],
        out_specs=pl.BlockSpec((tb, W_GROUP), lambda b, i: (b * nb + i, 0)),
        out_shape=jax.ShapeDtypeStruct((t, W_GROUP), ACT),
        scratch_shapes=[pltpu.VMEM((RWKV_HEADS // 2, LANES, LANES), F32),
                        pltpu.VMEM((1, W_GROUP), F32), pltpu.VMEM((1, W_GROUP), F32),
                        pltpu.VMEM((1, W_GROUP), F32), pltpu.VMEM((1, LANES), F32),
                        buf, buf, buf, buf, buf, buf, buf,
                        pltpu.VMEM((tb // RWKV_CHUNK, SUBLANES, W_GROUP), F32), buf,
                        pltpu.VMEM(term, BF16), pltpu.VMEM(term, F32), pltpu.VMEM(term, BF16), pltpu.VMEM(term, F32)],
        compiler_params=_cparams("parallel", "arbitrary"),
        name="rwkv",
    )(proj, proj, proj, proj, *params)


def _rope_tables(positions, rot, period, scale):
    half = rot // 2
    inv = ROPE_THETA ** (-jnp.arange(0, rot, 2, dtype=F32) / rot)
    ang = positions.reshape(-1).astype(F32)[:, None] * inv
    cos, sin = jnp.cos(ang), jnp.sin(ang)
    t = ang.shape[0]
    passthrough = period - rot
    c = jnp.concatenate([cos, cos, jnp.ones((t, passthrough), F32)], axis=1)
    sa = jnp.concatenate([-sin, jnp.zeros((t, half + passthrough), F32)], axis=1)
    sb = jnp.concatenate([jnp.zeros((t, half), F32), sin, jnp.zeros((t, passthrough), F32)], axis=1)
    reps = LANES // period
    return tuple(jnp.tile(a, (1, reps)) * scale for a in (c, sa, sb))


def _s5_tables(a_re, a_im, log_dt, b_re, b_im, c_re, c_im):
    lr = jnp.minimum(a_re, -1e-4)
    li = a_im
    dt = jnp.exp(log_dt)[:, None]
    mag = jnp.exp(dt * lr)
    ab_re, ab_im = mag * jnp.cos(dt * li), mag * jnp.sin(dt * li)
    den = lr * lr + li * li
    nr, ni = ab_re - 1.0, ab_im
    f_re = (nr * lr + ni * li) / den
    f_im = (ni * lr - nr * li) / den
    bb_re = f_re[..., None] * b_re - f_im[..., None] * b_im
    bb_im = f_re[..., None] * b_im + f_im[..., None] * b_re
    eye = jnp.eye(S5_GROUPS, dtype=F32)
    blk_in = lambda m: jnp.einsum('gpc,gh->gchp', m, eye).reshape(W_GROUP, S5_WIDTH)
    blk_out = lambda m: jnp.einsum('gcp,gh->gphc', m, eye).reshape(S5_WIDTH, W_GROUP)
    n_slabs = W_GROUP // LANES
    sw = S5_WIDTH // n_slabs
    diag_in = lambda m: jnp.stack([m[q * LANES:(q + 1) * LANES, q * sw:(q + 1) * sw] for q in range(n_slabs)])
    diag_out = lambda m: jnp.stack([m[q * sw:(q + 1) * sw, q * LANES:(q + 1) * LANES] for q in range(n_slabs)])
    bmat = jnp.concatenate([diag_in(blk_in(bb_re)), diag_in(blk_in(bb_im))], axis=2).astype(BF16)
    cre = diag_out(blk_out(c_re)).astype(BF16)
    cim = diag_out(blk_out(-c_im)).astype(BF16)

    def power(n):
        m = jnp.exp(n * dt * lr)
        return m * jnp.cos(n * dt * li), m * jnp.sin(n * dt * li)

    rows = jnp.arange(SUBLANES, dtype=F32)[:, None, None]
    a1r, a1i = power(jnp.ones((SUBLANES, 1, 1), F32))
    pwr, pwi = [], []
    for s in (1, 2, 4):
        pr, pi = power(jnp.full((1, 1, 1), float(s * S5_SEG), F32))
        keep = (rows >= s).astype(F32)
        pwr.append((keep * pr).reshape(SUBLANES, S5_WIDTH))
        pwi.append((keep * pi).reshape(SUBLANES, S5_WIDTH))
    pcr, pci = power(jnp.arange(1, S5_SEG + 1, dtype=F32)[:, None, None])
    return (bmat, cre, cim, a1r.reshape(SUBLANES, S5_WIDTH), a1i.reshape(SUBLANES, S5_WIDTH),
            jnp.stack(pwr), jnp.stack(pwi), pcr.reshape(S5_SEG, S5_WIDTH), pci.reshape(S5_SEG, S5_WIDTH))


def kernel(x, positions, norm_g, w_in, w_out, mla_q_norm_g, mla_kv_norm_g, mla_w_uq, mla_w_ukv, s5_a_re, s5_a_im, s5_log_dt, s5_b_re, s5_b_im, s5_c_re, s5_c_im, s5_d, s5_w_glu, s5_b_glu, rwkv_mu, rwkv_w0, rwkv_w2, rwkv_a0, rwkv_a2, rwkv_k_k, rwkv_k_a, rwkv_r_k, rwkv_ln_g, rwkv_ln_b, diff_lq1, diff_lk1, diff_lq2, diff_lk2, diff_subln_g, final_norm_g):
    batch, seq, d = x.shape
    depth = w_in.shape[0]
    t = batch * seq
    assert d == D_MODEL and seq % TQ == 0 and t % TM_INPROJ == 0
    assert seq % S5_BLOCK == 0 and seq % RWKV_BLOCK == 0

    ca, saa, sba = _rope_tables(positions, MLA_ROPE, LANES, 1.0)
    zero_hi = (jnp.arange(LANES) < MLA_ROPE).astype(F32)[None, :]
    ca = ca * zero_hi
    cd, sad, sbd = _rope_tables(positions, DIFF_ROT, DIFF_QK, 1.0)

    head_ones = jnp.kron(jnp.eye(RWKV_HEADS, dtype=F32), jnp.ones((RWKV_HEAD, RWKV_HEAD), F32)).astype(BF16)
    row = lambda a: a.reshape(1, -1).astype(F32)

    def layer_params(norm_g, qg, kvg, w_uq, w_ukv, a_re, a_im, log_dt, b_re, b_im, c_re, c_im, s5_d,
                     w_glu, b_glu, mu, w0, w2, a0, a2, k_k, k_a, r_k, ln_g, ln_b, lq1, lk1, lq2, lk2, subln_g):
        wq = w_uq.reshape(MLA_Q_LORA, MLA_HEADS, MLA_NOPE + MLA_ROPE)
        wq = jnp.pad(wq, ((0, 0), (0, 0), (0, 2 * LANES - MLA_NOPE - MLA_ROPE))).reshape(MLA_Q_LORA, -1)
        wkv = w_ukv.reshape(MLA_KV_LORA, MLA_HEADS, MLA_NOPE + MLA_V)
        wkv = jnp.concatenate([wkv[:, :, :MLA_NOPE].reshape(MLA_KV_LORA, -1),
                               wkv[:, :, MLA_NOPE:].reshape(MLA_KV_LORA, -1)], axis=1)
        zpad = jnp.zeros((RWKV_LORA, W_GROUP), F32)
        return dict(
            norm_g=row(norm_g),
            mla=(row(qg), row(kvg), wq.astype(BF16), wkv.astype(BF16)),
            s5=_s5_tables(a_re, a_im, log_dt, b_re, b_im, c_re, c_im) + (row(s5_d), w_glu.astype(BF16), row(b_glu)),
            rwkv=(row(mu[:W_GROUP]), row(mu[W_GROUP:2 * W_GROUP]), row(mu[2 * W_GROUP:3 * W_GROUP]),
                  row(mu[3 * W_GROUP:]), row(w0),
                  jnp.concatenate([w2, zpad], axis=0).astype(BF16),
                  row(a0),
                  jnp.concatenate([zpad, a2], axis=0).astype(BF16),
                  row(k_k), row(k_a), row(r_k), row(ln_g), row(ln_b)),
            lam=jnp.exp(jnp.sum(lq1 * lk1)) - jnp.exp(jnp.sum(lq2 * lk2)),
            subln_g=row(subln_g),
        )

    params = jax.vmap(layer_params)(
        norm_g, mla_q_norm_g, mla_kv_norm_g, mla_w_uq, mla_w_ukv, s5_a_re, s5_a_im, s5_log_dt, s5_b_re,
        s5_b_im, s5_c_re, s5_c_im, s5_d, s5_w_glu, s5_b_glu, rwkv_mu, rwkv_w0, rwkv_w2, rwkv_a0, rwkv_a2, rwkv_k_k,
        rwkv_k_a, rwkv_r_k, rwkv_ln_g, rwkv_ln_b, diff_lq1, diff_lk1, diff_lq2, diff_lk2, diff_subln_g)

    w_in_all = _w_prep(w_in)
    w_out_all = w_out.astype(BF16)

    xf = x.reshape(t, d)
    for l in range(depth):
        p = jax.tree.map(lambda a: a[l], params)
        proj = _inproj(xf, p["norm_g"], w_in_all, l)

        qa, ka, va = _mla_prep(proj, *p["mla"], ca, saa, sba)
        y_a = _attention([qa], ka, va, batch, MLA_HEADS, 2 * LANES, MLA_V, name="mla_attn")

        y_b = _s5(proj, batch, *p["s5"])

        y_c = _rwkv(proj, batch, *p["rwkv"], head_ones)

        lam_init = 0.8 - 0.6 * math.exp(-0.3 * l)
        q1, q2, kd, vd = _diff_prep(proj, cd, sad, sbd)
        y_d = _attention([q1, q2], kd, vd, batch, DIFF_HEADS, LANES, DIFF_V,
                         extra=((p["lam"] + lam_init).reshape(1, 1).astype(F32), p["subln_g"]),
                         out_scale=1.0 - lam_init, name="diff_attn")

        xf = _outproj(y_a, y_b, y_c, y_d, proj, xf, w_out_all, l, row(final_norm_g), l == depth - 1)
    return xf.reshape(batch, seq, d)
```

```python
import functools
import math

import jax
import jax.numpy as jnp
import numpy as np
from jax import lax
from jax.experimental import pallas as pl
from jax.experimental.pallas import tpu as pltpu

F32 = jnp.float32
BF16 = jnp.bfloat16
ACT = jnp.bfloat16

D_MODEL = 2048
W_GROUP = 512
ROPE_THETA = 500000.0
NORM_EPS = 1e-6
MLA_HEADS, MLA_NOPE, MLA_ROPE, MLA_V = 4, 128, 64, 128
MLA_Q_LORA, MLA_KV_LORA = 512, 256
S5_GROUP, S5_GROUPS, S5_STATE = 16, 32, 64
S5_WIDTH = S5_GROUPS * S5_STATE
RWKV_HEAD, RWKV_HEADS = 64, 8
RWKV_LORA = 64
RWKV_GN_EPS = 64e-5
DIFF_HEADS, DIFF_QK, DIFF_V, DIFF_ROT = 4, 64, 128, 16
DIFF_SUBLN_EPS = 1e-5

LANES = 128
SUBLANES = 8
VMEM_LIMIT_BYTES = 56 * 1024 * 1024

COL_GATE = 0
COL_CQ = 2048
COL_US5 = 2560
COL_QD = 3072
COL_KD = 3584
COL_VD = 4096
COL_R = 4608
COL_K = 5120
COL_V = 5632
COL_CKV = 6144
COL_KROPE = 6400
COL_LORA = 6528
N_PROJ = 6656

TM_INPROJ = 1024
TN_INPROJ = 1664
TM_OUTPROJ = 512
TM_PREP = 512
TQ = 512
KS = 256
S5_BLOCK = 512
S5_SEG = S5_BLOCK // 8
S5_LANE_CHUNK = 512
RWKV_BLOCK = 512
RWKV_CHUNK = 64
RWKV_CHUNKS_INTERLEAVED = 2
NEG_BIG = -1e30
LOG2E = math.log2(math.e)


def _cparams(*sem):
    return pltpu.CompilerParams(dimension_semantics=sem, vmem_limit_bytes=VMEM_LIMIT_BYTES)


def _dot(a, b):
    return jnp.dot(a.astype(BF16), b.astype(BF16), preferred_element_type=F32)


def _dot_nt(a, b):
    return lax.dot_general(a.astype(BF16), b.astype(BF16), (((1,), (1,)), ((), ())), preferred_element_type=F32)


def _dot_split(x, w):
    hi = x.astype(BF16)
    lo = (x - hi.astype(F32)).astype(BF16)
    return jnp.dot(hi, w, preferred_element_type=F32) + jnp.dot(lo, w, preferred_element_type=F32)


def _dot_split3(w, x):
    hi = x.astype(BF16)
    r1 = x - hi.astype(F32)
    mid = r1.astype(BF16)
    lo = (r1 - mid.astype(F32)).astype(BF16)
    return (jnp.dot(w, hi, preferred_element_type=F32) + jnp.dot(w, mid, preferred_element_type=F32)
            + jnp.dot(w, lo, preferred_element_type=F32))


def _sigmoid(x):
    return 1.0 / (1.0 + jnp.exp(-x))


_W_IN_SEGMENTS = ((COL_GATE, 4544, D_MODEL), (COL_CQ, 0, W_GROUP), (COL_US5, 832, W_GROUP), (COL_QD, 3008, W_GROUP),
                  (COL_KD, 3520, W_GROUP), (COL_VD, 4032, W_GROUP), (COL_R, 1344, W_GROUP), (COL_K, 1856, W_GROUP),
                  (COL_V, 2368, W_GROUP), (COL_CKV, 512, MLA_KV_LORA), (COL_KROPE, 768, MLA_ROPE),
                  (COL_LORA, 2880, 2 * RWKV_LORA))
W_PREP_SUB = 4


def _w_in_block_table():
    src_blk, shift, zero_hi = [], [], []
    for ob in range(N_PROJ // LANES):
        col = ob * LANES
        out0, src0, width = next(s for s in _W_IN_SEGMENTS if s[0] <= col < s[0] + max(s[2], LANES))
        src = src0 + (col - out0)
        src_blk.append(src // LANES)
        shift.append((src % LANES) // (LANES // 2))
        zero_hi.append(int(width < LANES))
    return (np.asarray(src_blk, np.int32), np.asarray(shift, np.int32), np.asarray(zero_hi, np.int32))


def _w_prep_kernel(blk_ref, shift_ref, zero_ref, *refs):
    in_refs, o_ref = refs[:-1], refs[-1]
    c = pl.program_id(1)
    lower = lax.broadcasted_iota(jnp.int32, (1, LANES), 1) < LANES // 2
    for sub in range(W_PREP_SUB):
        a_ref, b_ref = in_refs[2 * sub], in_refs[2 * sub + 1]
        ob = c * W_PREP_SUB + sub
        cols = slice(sub * LANES, (sub + 1) * LANES)

        @pl.when(shift_ref[ob] == 1)
        def _():
            o_ref[0, :, cols] = jnp.where(lower, pltpu.roll(a_ref[0], LANES // 2, axis=1),
                                          pltpu.roll(b_ref[0], LANES // 2, axis=1)).astype(BF16)

        @pl.when((shift_ref[ob] == 0) & (zero_ref[ob] == 0))
        def _():
            o_ref[0, :, cols] = a_ref[0].astype(BF16)

        @pl.when((shift_ref[ob] == 0) & (zero_ref[ob] == 1))
        def _():
            o_ref[0, :, cols] = jnp.where(lower, a_ref[0], 0.0).astype(BF16)


def _w_prep(w_in):
    depth, d, n_in = w_in.shape
    src_blk, shift, zero_hi = _w_in_block_table()
    last = (n_in - 1) // LANES

    def in_spec(sub, nxt):
        return pl.BlockSpec((1, d, LANES), lambda l, c, blk, sh, zh: (
            l, 0, jnp.minimum(blk[c * W_PREP_SUB + sub] + nxt, last)))

    return pl.pallas_call(
        _w_prep_kernel,
        grid_spec=pltpu.PrefetchScalarGridSpec(
            num_scalar_prefetch=3,
            grid=(depth, N_PROJ // (LANES * W_PREP_SUB)),
            in_specs=[in_spec(sub, nxt) for sub in range(W_PREP_SUB) for nxt in (0, 1)],
            out_specs=pl.BlockSpec((1, d, LANES * W_PREP_SUB), lambda l, c, blk, sh, zh: (l, 0, c)),
        ),
        out_shape=jax.ShapeDtypeStruct((depth, d, N_PROJ), BF16),
        compiler_params=_cparams("parallel", "parallel"),
        name="w_prep",
    )(jnp.asarray(src_blk), jnp.asarray(shift), jnp.asarray(zero_hi), *([w_in] * (2 * W_PREP_SUB)))


def _inproj_kernel(x_ref, g_ref, w_ref, o_ref, h_ref):
    @pl.when(pl.program_id(1) == 0)
    def _():
        x = x_ref[...]
        y = x * lax.rsqrt(jnp.mean(x * x, axis=-1, keepdims=True) + NORM_EPS)
        h_ref[...] = (y * g_ref[...]).astype(BF16)

    o_ref[...] = jnp.dot(h_ref[...], w_ref[0], preferred_element_type=F32).astype(ACT)


def _inproj(x, g, w_all, layer):
    t, d = x.shape
    n = w_all.shape[2]
    return pl.pallas_call(
        _inproj_kernel,
        grid=(t // TM_INPROJ, n // TN_INPROJ),
        in_specs=[pl.BlockSpec((TM_INPROJ, d), lambda i, j: (i, 0)),
                  pl.BlockSpec((1, d), lambda i, j: (0, 0)),
                  pl.BlockSpec((1, d, TN_INPROJ), lambda i, j: (layer, 0, j))],
        out_specs=pl.BlockSpec((TM_INPROJ, TN_INPROJ), lambda i, j: (i, j)),
        out_shape=jax.ShapeDtypeStruct((t, n), ACT),
        scratch_shapes=[pltpu.VMEM((TM_INPROJ, d), BF16)],
        compiler_params=_cparams("parallel", "arbitrary"),
        name="inproj",
    )(x, g, w_all)


def _outproj_kernel(ya_ref, yb_ref, yc_ref, yd_ref, gate_ref, x_ref, w_ref, fg_ref, o_ref, *, final):
    y = jnp.concatenate([ya_ref[...], yb_ref[...], yc_ref[...], yd_ref[...]], axis=-1).astype(F32)
    gate = gate_ref[...].astype(F32)
    mixed = y * (gate * _sigmoid(gate))
    xn = x_ref[...] + _dot(mixed, w_ref[0])
    if final:
        xn = xn * lax.rsqrt(jnp.mean(xn * xn, axis=-1, keepdims=True) + NORM_EPS) * fg_ref[...]
    o_ref[...] = xn


def _outproj(ya, yb, yc, yd, proj, x, w_all, layer, fg, final):
    t, d = x.shape
    tm = TM_OUTPROJ
    yspec = pl.BlockSpec((tm, W_GROUP), lambda i: (i, 0))
    return pl.pallas_call(
        functools.partial(_outproj_kernel, final=final),
        grid=(t // tm,),
        in_specs=[yspec, yspec, yspec, yspec,
                  pl.BlockSpec((tm, d), lambda i: (i, COL_GATE // D_MODEL)),
                  pl.BlockSpec((tm, d), lambda i: (i, 0)),
                  pl.BlockSpec((1, d, d), lambda i: (layer, 0, 0)),
                  pl.BlockSpec((1, d), lambda i: (0, 0))],
        out_specs=pl.BlockSpec((tm, d), lambda i: (i, 0)),
        out_shape=jax.ShapeDtypeStruct((t, d), F32),
        compiler_params=_cparams("parallel"),
        name="outproj_final" if final else "outproj",
    )(ya, yb, yc, yd, proj, x, w_all, fg)


def _rope128(x, c, sa, sb, half):
    return x * c + pltpu.roll(x, LANES - half, axis=1) * sa + pltpu.roll(x, half, axis=1) * sb


def _store_vt_tiles(v, vt_ref):
    for n in range(v.shape[0] // KS):
        vt_ref[n] = v[n * KS:(n + 1) * KS, :].T.astype(BF16)


def _mla_prep_kernel(cq_ref, ckv_ref, kr_ref, qg_ref, kvg_ref, wq_ref, wkv_ref, c_ref, sa_ref, sb_ref,
                     q_ref, k_ref, vt_ref):
    scale = (MLA_NOPE + MLA_ROPE) ** -0.5 * LOG2E
    half = MLA_ROPE // 2
    c, sa, sb = c_ref[...], sa_ref[...], sb_ref[...]
    cq = cq_ref[...].astype(F32)
    hq = cq * lax.rsqrt(jnp.mean(cq * cq, axis=-1, keepdims=True) + NORM_EPS) * qg_ref[...]
    q = _dot(hq, wq_ref[...])
    ckv = ckv_ref[...].astype(F32)
    hkv = ckv * lax.rsqrt(jnp.mean(ckv * ckv, axis=-1, keepdims=True) + NORM_EPS) * kvg_ref[...]
    kv = _dot(hkv, wkv_ref[...])
    kpe = _rope128(kr_ref[...].astype(F32), c, sa, sb, half).astype(BF16)
    for h in range(MLA_HEADS):
        base = 2 * LANES * h
        q_ref[:, base:base + LANES] = (q[:, base:base + LANES] * scale).astype(BF16)
        qpe = _rope128(q[:, base + LANES:base + 2 * LANES], c, sa, sb, half) * scale
        q_ref[:, base + LANES:base + 2 * LANES] = qpe.astype(BF16)
        k_ref[:, base:base + LANES] = kv[:, LANES * h:LANES * (h + 1)].astype(BF16)
        k_ref[:, base + LANES:base + 2 * LANES] = kpe
    _store_vt_tiles(kv[:, MLA_HEADS * MLA_NOPE:], vt_ref)


def _mla_prep(proj, qg, kvg, wq, wkv, c, sa, sb):
    t = proj.shape[0]
    tm = TM_PREP
    tab = pl.BlockSpec((tm, LANES), lambda i: (i, 0))
    full = lambda shape: pl.BlockSpec(shape, lambda i: (0, 0))
    return pl.pallas_call(
        _mla_prep_kernel,
        grid=(t // tm,),
        in_specs=[pl.BlockSpec((tm, MLA_Q_LORA), lambda i: (i, COL_CQ // MLA_Q_LORA)),
                  pl.BlockSpec((tm, MLA_KV_LORA), lambda i: (i, COL_CKV // MLA_KV_LORA)),
                  pl.BlockSpec((tm, LANES), lambda i: (i, COL_KROPE // LANES)),
                  full((1, MLA_Q_LORA)), full((1, MLA_KV_LORA)),
                  full(wq.shape), full(wkv.shape), tab, tab, tab],
        out_specs=[pl.BlockSpec((tm, MLA_HEADS * 2 * LANES), lambda i: (i, 0)),
                   pl.BlockSpec((tm, MLA_HEADS * 2 * LANES), lambda i: (i, 0)),
                   pl.BlockSpec((tm // KS, MLA_HEADS * MLA_V, KS), lambda i: (i, 0, 0))],
        out_shape=[jax.ShapeDtypeStruct((t, MLA_HEADS * 2 * LANES), BF16),
                   jax.ShapeDtypeStruct((t, MLA_HEADS * 2 * LANES), BF16),
                   jax.ShapeDtypeStruct((t // KS, MLA_HEADS * MLA_V, KS), BF16)],
        compiler_params=_cparams("parallel"),
        name="mla_prep",
    )(proj, proj, proj, qg, kvg, wq, wkv, c, sa, sb)


def _attn_kernel(*refs, n_maps, out_scale):
    q_refs = refs[:n_maps]
    k_ref, vt_ref = refs[n_maps:n_maps + 2]
    if n_maps == 2:
        lam_ref, g_ref = refs[n_maps + 2:n_maps + 4]
        o_ref = refs[n_maps + 4]
        scratch = refs[n_maps + 5:]
    else:
        o_ref = refs[n_maps + 2]
        scratch = refs[n_maps + 3:]
    states = [scratch[3 * a:3 * a + 3] for a in range(n_maps)]
    s_slots = scratch[3 * n_maps:3 * n_maps + 2]
    nq = k_ref.shape[0] // TQ
    sub_per_q = TQ // KS
    i = pl.program_id(2)
    q_blocks = (i, nq - 1 - i)

    for m_ref, l_ref, acc_ref in states:
        m_ref[...] = jnp.full(m_ref.shape, NEG_BIG, F32)
        l_ref[...] = jnp.zeros(l_ref.shape, F32)
        acc_ref[...] = jnp.zeros(acc_ref.shape, F32)

    half = nq // 2
    items = [(0, q_blocks[0], True), (1, q_blocks[1], True)] + [(1, t, False) for t in range(half)]
    for t in range(half - 1):
        first = t < i
        items.append((jnp.where(first, 0, 1), jnp.where(first, t, half + t - i), False))

    def rows(blk, size):
        if isinstance(blk, int):
            return pl.ds(blk * size, size)
        return pl.ds(pl.multiple_of(blk * size, size), size)

    def scores_into(item, s_ref):
        sel, jq, _ = item
        qblk = q_blocks[sel] if isinstance(sel, int) else jnp.where(sel == 0, q_blocks[0], q_blocks[1])
        for sub in range(sub_per_q):
            k = k_ref[rows(jq * sub_per_q + sub, KS), :]
            for a, q_ref in enumerate(q_refs):
                s_ref[a, sub] = lax.dot_general(k, q_ref[rows(qblk, TQ), :], (((1,), (1,)), ((), ())),
                                                preferred_element_type=F32)

    def consume(item, s_ref):
        sel, jq, diagonal = item
        for sub in range(sub_per_q):
            vt = vt_ref[jq * sub_per_q + sub]
            for a, (m_ref, l_ref, acc_ref) in enumerate(states):
                s = s_ref[a, sub]
                if diagonal:
                    key = lax.broadcasted_iota(jnp.int32, s.shape, 0) + sub * KS
                    qry = lax.broadcasted_iota(jnp.int32, s.shape, 1)
                    s = jnp.where(key <= qry, s, NEG_BIG)
                m_prev = m_ref[sel]
                m_new = jnp.maximum(m_prev, jnp.max(s, axis=0, keepdims=True))
                alpha = jnp.exp2(m_prev - m_new)
                p = jnp.exp2(s - m_new)
                l_ref[sel] = alpha * l_ref[sel] + jnp.sum(p.reshape(KS // SUBLANES, SUBLANES, TQ), axis=0)
                acc_ref[sel] = alpha * acc_ref[sel] + jnp.dot(vt, p.astype(BF16), preferred_element_type=F32)
                m_ref[sel] = m_new

    scores_into(items[0], s_slots[0])
    for n, item in enumerate(items):
        if n + 1 < len(items):
            scores_into(items[n + 1], s_slots[(n + 1) % 2])
        consume(item, s_slots[n % 2])

    for sel in range(2):
        outs = []
        for m_ref, l_ref, acc_ref in states:
            inv_l = 1.0 / jnp.sum(l_ref[sel], axis=0, keepdims=True)
            outs.append(acc_ref[sel] * inv_l)
        if n_maps == 2:
            ot = outs[0] - lam_ref[0, 0] * outs[1]
            ot = ot * lax.rsqrt(jnp.mean(ot * ot, axis=0, keepdims=True) + DIFF_SUBLN_EPS)
            o_ref[rows(q_blocks[sel], TQ), :] = (ot.T * (g_ref[...] * out_scale)).astype(ACT)
        else:
            o_ref[rows(q_blocks[sel], TQ), :] = outs[0].T.astype(ACT)


def _attention(qs, k, vt, batch, heads, dqk, dv, extra=(), out_scale=1.0, name="attn"):
    n_maps = len(qs)
    t = k.shape[0]
    seq = t // batch
    nq = seq // TQ
    assert nq % 2 == 0
    seq_spec = lambda width: pl.BlockSpec((seq, width), lambda b, h, i: (b, h))
    in_specs = [seq_spec(dqk)] * (n_maps + 1) + [pl.BlockSpec((seq // KS, dv, KS), lambda b, h, i: (b, h, 0))]
    if n_maps == 2:
        in_specs += [pl.BlockSpec(memory_space=pltpu.SMEM), pl.BlockSpec((1, dv), lambda b, h, i: (0, 0))]
    state = [pltpu.VMEM((2, 1, TQ), F32), pltpu.VMEM((2, SUBLANES, TQ), F32), pltpu.VMEM((2, dv, TQ), F32)]
    s_slot = pltpu.VMEM((n_maps, TQ // KS, KS, TQ), F32)
    return pl.pallas_call(
        functools.partial(_attn_kernel, n_maps=n_maps, out_scale=out_scale),
        grid=(batch, heads, nq // 2),
        in_specs=in_specs,
        out_specs=seq_spec(dv),
        out_shape=jax.ShapeDtypeStruct((t, heads * dv), ACT),
        scratch_shapes=state * n_maps + [s_slot, s_slot],
        compiler_params=_cparams("parallel", "parallel", "arbitrary"),
        name=name,
    )(*qs, k, vt, *extra)


def _diff_prep_kernel(q_ref, k_ref, v_ref, c_ref, sa_ref, sb_ref, q1_ref, q2_ref, ko_ref, vt_ref):
    half = DIFF_ROT // 2
    scale = DIFF_QK ** -0.5 * LOG2E
    lane = lax.broadcasted_iota(jnp.int32, (1, LANES), 1)
    m0 = jnp.where(lane < DIFF_QK, scale, 0.0).astype(F32)
    m1 = scale - m0
    c, sa, sb = c_ref[...], sa_ref[...], sb_ref[...]
    for h in range(DIFF_HEADS):
        sl = slice(LANES * h, LANES * (h + 1))
        q = _rope128(q_ref[:, sl].astype(F32), c, sa, sb, half)
        q1_ref[:, sl] = (q * m0).astype(BF16)
        q2_ref[:, sl] = (q * m1).astype(BF16)
        ko_ref[:, sl] = _rope128(k_ref[:, sl].astype(F32), c, sa, sb, half).astype(BF16)
    _store_vt_tiles(v_ref[...].astype(F32), vt_ref)


def _diff_prep(proj, c, sa, sb):
    t = proj.shape[0]
    tm = TM_PREP
    tab = pl.BlockSpec((tm, LANES), lambda i: (i, 0))
    seg = lambda col: pl.BlockSpec((tm, W_GROUP), lambda i: (i, col // W_GROUP))
    out = pl.BlockSpec((tm, W_GROUP), lambda i: (i, 0))
    shp = jax.ShapeDtypeStruct((t, W_GROUP), BF16)
    return pl.pallas_call(
        _diff_prep_kernel,
        grid=(t // tm,),
        in_specs=[seg(COL_QD), seg(COL_KD), seg(COL_VD), tab, tab, tab],
        out_specs=[out, out, out, pl.BlockSpec((tm // KS, W_GROUP, KS), lambda i: (i, 0, 0))],
        out_shape=[shp, shp, shp, jax.ShapeDtypeStruct((t // KS, W_GROUP, KS), BF16)],
        compiler_params=_cparams("parallel"),
        name="diff_prep",
    )(proj, proj, proj, c, sa, sb)


def _s5_kernel(u_ref, bmat_ref, cre_ref, cim_ref, a1r_ref, a1i_ref, pwr_ref, pwi_ref, pcr_ref, pci_ref, d_ref,
               wglu_ref, bglu_ref, o_ref, hr_ref, hi_ref, sr_ref, si_ref, uf_ref, up_ref, op_ref):
    @pl.when(pl.program_id(1) == 0)
    def _():
        sr_ref[...] = jnp.zeros(sr_ref.shape, F32)
        si_ref[...] = jnp.zeros(si_ref.shape, F32)

    n_slabs = W_GROUP // LANES
    uf = u_ref[...].astype(F32)
    for q in range(n_slabs):
        uf_ref[q] = uf[:, q * LANES:(q + 1) * LANES]
    for t in range(S5_SEG):
        for q in range(n_slabs):
            up_ref[t * SUBLANES:(t + 1) * SUBLANES, q * LANES:(q + 1) * LANES] = (
                uf_ref[q, pl.ds(t, SUBLANES, stride=S5_SEG), :])
    u = up_ref[...]
    ub = u.astype(BF16)
    sw = S5_WIDTH // n_slabs
    for q in range(n_slabs):
        bu = jnp.dot(ub[:, q * LANES:(q + 1) * LANES], bmat_ref[q], preferred_element_type=F32)
        hr_ref[:, q * sw:(q + 1) * sw] = bu[:, :sw]
        hi_ref[:, q * sw:(q + 1) * sw] = bu[:, sw:]

    first_row = lax.broadcasted_iota(jnp.int32, (SUBLANES, S5_LANE_CHUNK), 0) == 0
    for c in range(S5_WIDTH // S5_LANE_CHUNK):
        ls = slice(c * S5_LANE_CHUNK, (c + 1) * S5_LANE_CHUNK)
        ar, ai = a1r_ref[:, ls], a1i_ref[:, ls]

        def tile(t, carry):
            xr, xi = carry
            rows = pl.ds(pl.multiple_of(t * SUBLANES, SUBLANES), SUBLANES)
            xr, xi = hr_ref[rows, ls] + (ar * xr - ai * xi), hi_ref[rows, ls] + (ar * xi + ai * xr)
            hr_ref[rows, ls] = xr
            hi_ref[rows, ls] = xi
            return xr, xi

        zero = jnp.zeros((SUBLANES, S5_LANE_CHUNK), F32)
        er, ei = lax.fori_loop(0, S5_SEG, tile, (zero, zero), unroll=4)

        gr = jnp.where(first_row, sr_ref[:, ls], pltpu.roll(er, 1, axis=0))
        gi = jnp.where(first_row, si_ref[:, ls], pltpu.roll(ei, 1, axis=0))
        for s in range(3):
            pr, pi = pwr_ref[s, :, ls], pwi_ref[s, :, ls]
            rr = pltpu.roll(gr, 1 << s, axis=0)
            ri = pltpu.roll(gi, 1 << s, axis=0)
            gr, gi = gr + (pr * rr - pi * ri), gi + (pr * ri + pi * rr)
        last = slice(SUBLANES - 1, SUBLANES)
        pr, pi = pwr_ref[0, last, ls], pwi_ref[0, last, ls]
        sr_ref[:, ls] = er[last] + (pr * gr[last] - pi * gi[last])
        si_ref[:, ls] = ei[last] + (pr * gi[last] + pi * gr[last])

        for t in range(S5_SEG):
            rows = slice(t * SUBLANES, (t + 1) * SUBLANES)
            pr, pi = pcr_ref[t:t + 1, ls], pci_ref[t:t + 1, ls]
            hr_ref[rows, ls] = hr_ref[rows, ls] + (pr * gr - pi * gi)
            hi_ref[rows, ls] = hi_ref[rows, ls] + (pr * gi + pi * gr)

    y = jnp.concatenate(
        [jnp.dot(hr_ref[:, q * sw:(q + 1) * sw].astype(BF16), cre_ref[q], preferred_element_type=F32)
         + jnp.dot(hi_ref[:, q * sw:(q + 1) * sw].astype(BF16), cim_ref[q], preferred_element_type=F32)
         for q in range(n_slabs)], axis=1)
    y = y + d_ref[...] * u
    g = 0.5 * y * (1.0 + jnp.tanh(math.sqrt(2.0 / math.pi) * (y + 0.044715 * (y * y * y))))
    out = g * _sigmoid(_dot(g, wglu_ref[...]) + bglu_ref[...])
    for q in range(n_slabs):
        op_ref[q] = out[:, q * LANES:(q + 1) * LANES]
    for r in range(SUBLANES):
        o_ref[r * S5_SEG:(r + 1) * S5_SEG, :] = jnp.concatenate(
            [op_ref[q, pl.ds(r, S5_SEG, stride=SUBLANES), :] for q in range(n_slabs)], axis=1).astype(ACT)


def _s5(proj, batch, bmat, cre, cim, a1r, a1i, pwr, pwi, pcr, pci, d, wglu, bglu):
    t = proj.shape[0]
    nb = t // batch // S5_BLOCK
    full2 = lambda a: pl.BlockSpec(a.shape, lambda b, i: (0,) * a.ndim)
    ucol = COL_US5 // W_GROUP
    return pl.pallas_call(
        _s5_kernel,
        grid=(batch, nb),
        in_specs=[pl.BlockSpec((S5_BLOCK, W_GROUP), lambda b, i: (b * nb + i, ucol))]
        + [full2(a) for a in (bmat, cre, cim, a1r, a1i, pwr, pwi, pcr, pci, d, wglu, bglu)],
        out_specs=pl.BlockSpec((S5_BLOCK, W_GROUP), lambda b, i: (b * nb + i, 0)),
        out_shape=jax.ShapeDtypeStruct((t, W_GROUP), ACT),
        scratch_shapes=[pltpu.VMEM((S5_BLOCK, S5_WIDTH), F32), pltpu.VMEM((S5_BLOCK, S5_WIDTH), F32),
                        pltpu.VMEM((1, S5_WIDTH), F32), pltpu.VMEM((1, S5_WIDTH), F32),
                        pltpu.VMEM((W_GROUP // LANES, S5_BLOCK, LANES), F32), pltpu.VMEM((S5_BLOCK, W_GROUP), F32),
                        pltpu.VMEM((W_GROUP // LANES, S5_BLOCK, LANES), F32)],
        compiler_params=_cparams("parallel", "arbitrary"),
        name="s5",
    )(proj, bmat, cre, cim, a1r, a1i, pwr, pwi, pcr, pci, d, wglu, bglu)


def _rwkv_kernel(r_ref, k_ref, v_ref, lo_ref, mu_r_ref, mu_k_ref, mu_v_ref, mu_lo_ref, w0_ref, w2_ref, a0_ref,
                 a2_ref, kk_ref, ka_ref, rk_ref, lng_ref, lnb_ref, ones_ref,
                 o_ref,
                 st_ref, pr_ref, pk_ref, pv_ref, plo_ref,
                 rt_ref, at_ref, bt_ref, kt_ref, bg_ref, kg_ref, vv_ref, ge_ref, oo_ref,
                 qc_ref, ec_ref, mc_ref, dc_ref):
    tb, lc = RWKV_BLOCK, RWKV_CHUNK

    @pl.when(pl.program_id(1) == 0)
    def _():
        st_ref[...] = jnp.zeros(st_ref.shape, F32)
        pr_ref[...] = jnp.zeros(pr_ref.shape, F32)
        pk_ref[...] = jnp.zeros(pk_ref.shape, F32)
        pv_ref[...] = jnp.zeros(pv_ref.shape, F32)
        plo_ref[...] = jnp.zeros(plo_ref.shape, F32)

    def shifted(z_ref, prev_ref, mu_ref):
        z = z_ref[...].astype(F32)
        first = lax.broadcasted_iota(jnp.int32, z.shape, 0) == 0
        z_prev = jnp.where(first, prev_ref[...], pltpu.roll(z, 1, axis=0))
        prev_ref[...] = z[tb - 1:tb, :]
        return z + (z_prev - z) * mu_ref[...]

    r = shifted(r_ref, pr_ref, mu_r_ref)
    k = shifted(k_ref, pk_ref, mu_k_ref)
    v = shifted(v_ref, pv_ref, mu_v_ref)
    lo = shifted(lo_ref, plo_ref, mu_lo_ref)

    ones = ones_ref[...]
    row = lax.broadcasted_iota(jnp.int32, (lc, lc), 0)
    col = lax.broadcasted_iota(jnp.int32, (lc, lc), 1)
    tri = (row >= col).astype(BF16)

    def prepare(chunks, rk_out):
        rg = slice(chunks[0] * lc, (chunks[-1] + 1) * lc)
        r_g, k_g, lo_g = r[rg], k[rg], lo[rg]
        wx = -(w0_ref[...] + _dot(jnp.tanh(lo_g), w2_ref[...]))
        yield
        w_log = -(jnp.maximum(wx, 0.0) + jnp.log(1.0 + jnp.exp(-jnp.abs(wx)))) - 0.5
        lw = -jnp.exp(w_log)
        yield
        a = _sigmoid(a0_ref[...] + _dot(lo_g, a2_ref[...]))
        yield
        kk = k_g * kk_ref[...]
        kk = kk / jnp.maximum(jnp.sqrt(_dot_split(kk * kk, ones)), 1e-12)
        yield
        k2 = k_g * (1.0 + (a - 1.0) * ka_ref[...])
        aa = -kk
        bb = kk * a
        rk_out.append(r_g * k2)
        vv_ref[rg, :] = v[rg]
        yield
        for n, c in enumerate(chunks):
            loc = slice(n * lc, (n + 1) * lc)
            rs = slice(c * lc, (c + 1) * lc)
            lwc = lw[loc]
            cum = _dot_split3(tri, lwc)
            cum_last = cum[lc - 1:lc, :]
            yield
            e_neg = jnp.exp(-cum)
            e_end = jnp.exp(cum_last - cum)
            rt_ref[rs, :] = r_g[loc] * jnp.exp(cum)
            at_ref[rs, :] = aa[loc] * jnp.exp(cum - lwc)
            yield
            bt_ref[rs, :] = bb[loc] * e_neg
            kt_ref[rs, :] = k2[loc] * e_neg
            bg_ref[rs, :] = bb[loc] * e_end
            kg_ref[rs, :] = k2[loc] * e_end
            ge_ref[c] = jnp.broadcast_to(jnp.exp(cum_last), (SUBLANES, W_GROUP))
            yield

    lane = lax.broadcasted_iota(jnp.int32, (1, LANES), 1)
    m0 = (lane < RWKV_HEAD).astype(F32)
    m1 = 1.0 - m0
    r2 = lax.broadcasted_iota(jnp.int32, (2 * lc, 2 * lc), 0)
    c2 = lax.broadcasted_iota(jnp.int32, (2 * lc, 2 * lc), 1)
    same = jnp.right_shift(r2, 6) == jnp.right_shift(c2, 6)
    assert lc == 64
    strict = jnp.where(same & (r2 > c2), 1.0, 0.0).astype(F32)
    incl = jnp.where(same & (r2 >= c2), 1.0, 0.0).astype(F32)
    eye = jnp.where(r2 == c2, 1.0, 0.0).astype(F32)
    n_double = int(math.log2(lc)) - 1

    def stack(x):
        return jnp.concatenate([x * m0, x * m1], axis=0)

    pairs = range(RWKV_HEADS // 2)
    lsl = [slice(LANES * p, LANES * (p + 1)) for p in pairs]
    n_chunks = tb // lc
    dot_nt = lambda x, y: lax.dot_general(x, y, (((1,), (1,)), ((), ())), preferred_element_type=F32)
    dot_tn = lambda x, y: lax.dot_general(x, y, (((0,), (0,)), ((), ())), preferred_element_type=F32)
    dot_nn = lambda x, y: jnp.dot(x, y, preferred_element_type=F32)
    bf = lambda xs: [x.astype(BF16) for x in xs]

    def chunk_terms(chunks):
        chains = [(c, p) for c in chunks for p in range(RWKV_HEADS // 2)]
        pairs = range(len(chains))
        tile = lambda ref: [stack(ref[c * lc:(c + 1) * lc, LANES * p:LANES * (p + 1)]) for c, p in chains]
        a_s = bf(tile(at_ref))
        r_f = tile(rt_ref)
        b_s = bf(tile(bt_ref))
        k_s = bf(tile(kt_ref))
        v_s = bf(tile(vv_ref))
        bg_s = bf(tile(bg_ref))
        kg_s = bf(tile(kg_ref))
        big = [dot_nt(jnp.concatenate([a_s[p], r_f[p].astype(BF16)], axis=0),
                      jnp.concatenate([b_s[p], k_s[p]], axis=0)) for p in pairs]
        nil = [big[p][:2 * lc, :2 * lc] * strict for p in pairs]
        a_ak = bf([big[p][:2 * lc, 2 * lc:] * strict for p in pairs])
        a_rb = bf([big[p][2 * lc:, :2 * lc] * incl for p in pairs])
        a_rk = bf([big[p][2 * lc:, 2 * lc:] * incl for p in pairs])
        yield
        akv = bf([dot_nn(a_ak[p], v_s[p]) for p in pairs])
        inv = [eye + nil[p] for p in pairs]
        nb = bf(nil)
        nil = [dot_nn(nb[p], nb[p]) for p in pairs]
        yield
        for step in range(1, n_double):
            nb = bf(nil)
            prod = [dot_nn(nb[p], jnp.concatenate([nb[p], inv[p].astype(BF16)], axis=1)) for p in pairs]
            nil = [prod[p][:, :2 * lc] for p in pairs]
            inv = [inv[p] + prod[p][:, 2 * lc:] for p in pairs]
            yield
        ib = bf(inv)
        inv = [inv[p] + dot_nn(nil[p].astype(BF16), ib[p]) for p in pairs]
        yield
        ib = bf(inv)
        tatv = bf([dot_nn(ib[p], jnp.concatenate([a_s[p], akv[p]], axis=1)) for p in pairs])
        yield
        qe = [dot_nn(a_rb[p], tatv[p]) for p in pairs]
        qc = [r_f[p] + qe[p][:, :2 * lc] for p in pairs]
        ec = [qe[p][:, 2 * lc:] + dot_nn(a_rk[p], v_s[p]) for p in pairs]
        yield
        md = [dot_tn(tatv[p], bg_s[p]) for p in pairs]
        mc = [md[p][:2 * lc] for p in pairs]
        dc = [md[p][2 * lc:] + dot_tn(v_s[p], kg_s[p]) for p in pairs]
        for j, (c, p) in enumerate(chains):
            idx = c * (RWKV_HEADS // 2) + p
            qc_ref[idx] = qc[j].astype(BF16)
            ec_ref[idx] = ec[j]
            mc_ref[idx] = mc[j].astype(BF16)
            dc_ref[idx] = dc[j]

    def run_alternating(*stage_generators):
        active = list(stage_generators)
        while active:
            for gen in list(active):
                if next(gen, "done") == "done":
                    active.remove(gen)

    state = [st_ref[p] for p in pairs]

    def advance_state(chunks):
        for c in chunks:
            sb = bf(state)
            os_ = [dot_nt(qc_ref[c * len(pairs) + p], sb[p]) + ec_ref[c * len(pairs) + p] for p in pairs]
            state[:] = [state[p] * ge_ref[c, 0:1, lsl[p]] + dot_nn(sb[p], mc_ref[c * len(pairs) + p])
                        + dc_ref[c * len(pairs) + p] for p in pairs]
            for p in pairs:
                oo_ref[c * lc:(c + 1) * lc, lsl[p]] = os_[p][:lc] + os_[p][lc:]
            yield

    groups = [range(c0, c0 + RWKV_CHUNKS_INTERLEAVED) for c0 in range(0, n_chunks, RWKV_CHUNKS_INTERLEAVED)]
    rk_parts = []
    run_alternating(prepare(groups[0], rk_parts))
    for g, group in enumerate(groups):
        stages = [chunk_terms(group)]
        if g + 1 < len(groups):
            stages.append(prepare(groups[g + 1], rk_parts))
        if g > 0:
            stages.append(advance_state(groups[g - 1]))
        run_alternating(*stages)
    run_alternating(advance_state(groups[-1]))
    rk2 = jnp.concatenate(rk_parts, axis=0)
    for p in pairs:
        st_ref[p] = state[p]

    o = oo_ref[...]
    inv_n = 1.0 / RWKV_HEAD
    mean = _dot_split(o, ones) * inv_n
    oc = o - mean
    var = _dot_split(oc * oc, ones) * inv_n
    o = oc * lax.rsqrt(var + RWKV_GN_EPS) * lng_ref[...] + lnb_ref[...]
    bonus = _dot_split(rk2 * rk_ref[...], ones) * v
    o_ref[...] = (o + bonus).astype(ACT)


def _rwkv(proj, batch, mu_r, mu_k, mu_v, mu_lo, w0, w2p, a0, a2p, k_k, k_a, r_k, ln_g, ln_b, ones):
    t = proj.shape[0]
    tb = RWKV_BLOCK
    nb = t // batch // tb
    seg = lambda col: pl.BlockSpec((tb, W_GROUP), lambda b, i: (b * nb + i, col // W_GROUP))
    full2 = lambda a: pl.BlockSpec(a.shape, lambda b, i: (0,) * a.ndim)
    params = (mu_r, mu_k, mu_v, mu_lo, w0, w2p, a0, a2p, k_k, k_a, r_k, ln_g, ln_b, ones)
    buf = pltpu.VMEM((tb, W_GROUP), F32)
    term = (tb // RWKV_CHUNK * (RWKV_HEADS // 2), LANES, LANES)
    return pl.pallas_call(
        _rwkv_kernel,
        grid=(batch, nb),
        in_specs=[seg(COL_R), seg(COL_K), seg(COL_V),
                  pl.BlockSpec((tb, LANES), lambda b, i: (b * nb + i, COL_LORA // LANES))]
        + [full2(a) for a in params],
        out_specs=pl.BlockSpec((tb, W_GROUP), lambda b, i: (b * nb + i, 0)),
        out_shape=jax.ShapeDtypeStruct((t, W_GROUP), ACT),
        scratch_shapes=[pltpu.VMEM((RWKV_HEADS // 2, LANES, LANES), F32),
                        pltpu.VMEM((1, W_GROUP), F32), pltpu.VMEM((1, W_GROUP), F32),
                        pltpu.VMEM((1, W_GROUP), F32), pltpu.VMEM((1, LANES), F32),
                        buf, buf, buf, buf, buf, buf, buf,
                        pltpu.VMEM((tb // RWKV_CHUNK, SUBLANES, W_GROUP), F32), buf,
                        pltpu.VMEM(term, BF16), pltpu.VMEM(term, F32), pltpu.VMEM(term, BF16), pltpu.VMEM(term, F32)],
        compiler_params=_cparams("parallel", "arbitrary"),
        name="rwkv",
    )(proj, proj, proj, proj, *params)


def _rope_tables(positions, rot, period, scale):
    half = rot // 2
    inv = ROPE_THETA ** (-jnp.arange(0, rot, 2, dtype=F32) / rot)
    ang = positions.reshape(-1).astype(F32)[:, None] * inv
    cos, sin = jnp.cos(ang), jnp.sin(ang)
    t = ang.shape[0]
    passthrough = period - rot
    c = jnp.concatenate([cos, cos, jnp.ones((t, passthrough), F32)], axis=1)
    sa = jnp.concatenate([-sin, jnp.zeros((t, half + passthrough), F32)], axis=1)
    sb = jnp.concatenate([jnp.zeros((t, half), F32), sin, jnp.zeros((t, passthrough), F32)], axis=1)
    reps = LANES // period
    return tuple(jnp.tile(a, (1, reps)) * scale for a in (c, sa, sb))


def _s5_tables(a_re, a_im, log_dt, b_re, b_im, c_re, c_im):
    lr = jnp.minimum(a_re, -1e-4)
    li = a_im
    dt = jnp.exp(log_dt)[:, None]
    mag = jnp.exp(dt * lr)
    ab_re, ab_im = mag * jnp.cos(dt * li), mag * jnp.sin(dt * li)
    den = lr * lr + li * li
    nr, ni = ab_re - 1.0, ab_im
    f_re = (nr * lr + ni * li) / den
    f_im = (ni * lr - nr * li) / den
    bb_re = f_re[..., None] * b_re - f_im[..., None] * b_im
    bb_im = f_re[..., None] * b_im + f_im[..., None] * b_re
    n_slabs = W_GROUP // LANES
    gps = S5_GROUPS // n_slabs
    sw = S5_WIDTH // n_slabs
    eye = jnp.eye(gps, dtype=F32)
    slab_in = lambda m: jnp.einsum('sgpc,gh->sgchp', m.reshape(n_slabs, gps, S5_STATE, S5_GROUP),
                                   eye).reshape(n_slabs, LANES, sw)
    slab_out = lambda m: jnp.einsum('sgcp,gh->sgphc', m.reshape(n_slabs, gps, S5_GROUP, S5_STATE),
                                    eye).reshape(n_slabs, sw, LANES)
    bmat = jnp.concatenate([slab_in(bb_re), slab_in(bb_im)], axis=2).astype(BF16)
    cre = slab_out(c_re).astype(BF16)
    cim = slab_out(-c_im).astype(BF16)

    def power(n):
        m = jnp.exp(n * dt * lr)
        return m * jnp.cos(n * dt * li), m * jnp.sin(n * dt * li)

    rows = jnp.arange(SUBLANES, dtype=F32)[:, None, None]
    a1r, a1i = power(jnp.ones((SUBLANES, 1, 1), F32))
    pwr, pwi = [], []
    for s in (1, 2, 4):
        pr, pi = power(jnp.full((1, 1, 1), float(s * S5_SEG), F32))
        keep = (rows >= s).astype(F32)
        pwr.append((keep * pr).reshape(SUBLANES, S5_WIDTH))
        pwi.append((keep * pi).reshape(SUBLANES, S5_WIDTH))
    pcr, pci = power(jnp.arange(1, S5_SEG + 1, dtype=F32)[:, None, None])
    return (bmat, cre, cim, a1r.reshape(SUBLANES, S5_WIDTH), a1i.reshape(SUBLANES, S5_WIDTH),
            jnp.stack(pwr), jnp.stack(pwi), pcr.reshape(S5_SEG, S5_WIDTH), pci.reshape(S5_SEG, S5_WIDTH))


def kernel(x, positions, norm_g, w_in, w_out, mla_q_norm_g, mla_kv_norm_g, mla_w_uq, mla_w_ukv, s5_a_re, s5_a_im, s5_log_dt, s5_b_re, s5_b_im, s5_c_re, s5_c_im, s5_d, s5_w_glu, s5_b_glu, rwkv_mu, rwkv_w0, rwkv_w2, rwkv_a0, rwkv_a2, rwkv_k_k, rwkv_k_a, rwkv_r_k, rwkv_ln_g, rwkv_ln_b, diff_lq1, diff_lk1, diff_lq2, diff_lk2, diff_subln_g, final_norm_g):
    batch, seq, d = x.shape
    depth = w_in.shape[0]
    t = batch * seq
    assert d == D_MODEL and seq % TQ == 0 and t % TM_INPROJ == 0
    assert seq % S5_BLOCK == 0 and seq % RWKV_BLOCK == 0

    ca, saa, sba = _rope_tables(positions, MLA_ROPE, LANES, 1.0)
    zero_hi = (jnp.arange(LANES) < MLA_ROPE).astype(F32)[None, :]
    ca = ca * zero_hi
    cd, sad, sbd = _rope_tables(positions, DIFF_ROT, DIFF_QK, 1.0)

    head_ones = jnp.kron(jnp.eye(RWKV_HEADS, dtype=F32), jnp.ones((RWKV_HEAD, RWKV_HEAD), F32)).astype(BF16)
    row = lambda a: a.reshape(1, -1).astype(F32)

    def layer_params(norm_g, qg, kvg, w_uq, w_ukv, a_re, a_im, log_dt, b_re, b_im, c_re, c_im, s5_d,
                     w_glu, b_glu, mu, w0, w2, a0, a2, k_k, k_a, r_k, ln_g, ln_b, lq1, lk1, lq2, lk2, subln_g):
        wq = w_uq.reshape(MLA_Q_LORA, MLA_HEADS, MLA_NOPE + MLA_ROPE)
        wq = jnp.pad(wq, ((0, 0), (0, 0), (0, 2 * LANES - MLA_NOPE - MLA_ROPE))).reshape(MLA_Q_LORA, -1)
        wkv = w_ukv.reshape(MLA_KV_LORA, MLA_HEADS, MLA_NOPE + MLA_V)
        wkv = jnp.concatenate([wkv[:, :, :MLA_NOPE].reshape(MLA_KV_LORA, -1),
                               wkv[:, :, MLA_NOPE:].reshape(MLA_KV_LORA, -1)], axis=1)
        zpad = jnp.zeros((RWKV_LORA, W_GROUP), F32)
        return dict(
            norm_g=row(norm_g),
            mla=(row(qg), row(kvg), wq.astype(BF16), wkv.astype(BF16)),
            s5=_s5_tables(a_re, a_im, log_dt, b_re, b_im, c_re, c_im) + (row(s5_d), w_glu.astype(BF16), row(b_glu)),
            rwkv=(row(mu[:W_GROUP]), row(mu[W_GROUP:2 * W_GROUP]), row(mu[2 * W_GROUP:3 * W_GROUP]),
                  row(mu[3 * W_GROUP:]), row(w0),
                  jnp.concatenate([w2, zpad], axis=0).astype(BF16),
                  row(a0),
                  jnp.concatenate([zpad, a2], axis=0).astype(BF16),
                  row(k_k), row(k_a), row(r_k), row(ln_g), row(ln_b)),
            lam=jnp.exp(jnp.sum(lq1 * lk1)) - jnp.exp(jnp.sum(lq2 * lk2)),
            subln_g=row(subln_g),
        )

    params = jax.vmap(layer_params)(
        norm_g, mla_q_norm_g, mla_kv_norm_g, mla_w_uq, mla_w_ukv, s5_a_re, s5_a_im, s5_log_dt, s5_b_re,
        s5_b_im, s5_c_re, s5_c_im, s5_d, s5_w_glu, s5_b_glu, rwkv_mu, rwkv_w0, rwkv_w2, rwkv_a0, rwkv_a2, rwkv_k_k,
        rwkv_k_a, rwkv_r_k, rwkv_ln_g, rwkv_ln_b, diff_lq1, diff_lk1, diff_lq2, diff_lk2, diff_subln_g)

    w_in_all = _w_prep(w_in)
    w_out_all = w_out.astype(BF16)

    xf = x.reshape(t, d)
    for l in range(depth):
        p = jax.tree.map(lambda a: a[l], params)
        proj = _inproj(xf, p["norm_g"], w_in_all, l)

        qa, ka, va = _mla_prep(proj, *p["mla"], ca, saa, sba)
        y_a = _attention([qa], ka, va, batch, MLA_HEADS, 2 * LANES, MLA_V, name="mla_attn")

        y_b = _s5(proj, batch, *p["s5"])

        y_c = _rwkv(proj, batch, *p["rwkv"], head_ones)

        lam_init = 0.8 - 0.6 * math.exp(-0.3 * l)
        q1, q2, kd, vd = _diff_prep(proj, cd, sad, sbd)
        y_d = _attention([q1, q2], kd, vd, batch, DIFF_HEADS, LANES, DIFF_V,
                         extra=((p["lam"] + lam_init).reshape(1, 1).astype(F32), p["subln_g"]),
                         out_scale=1.0 - lam_init, name="diff_attn")

        xf = _outproj(y_a, y_b, y_c, y_d, proj, xf, w_out_all, l, row(final_norm_g), l == depth - 1)
    return xf.reshape(batch, seq, d)
```

```python
import functools
import math

import jax
import jax.numpy as jnp
import numpy as np
from jax import lax
from jax.experimental import pallas as pl
from jax.experimental.pallas import tpu as pltpu

F32 = jnp.float32
BF16 = jnp.bfloat16
ACT = jnp.bfloat16

D_MODEL = 2048
W_GROUP = 512
ROPE_THETA = 500000.0
NORM_EPS = 1e-6
MLA_HEADS, MLA_NOPE, MLA_ROPE, MLA_V = 4, 128, 64, 128
MLA_Q_LORA, MLA_KV_LORA = 512, 256
S5_GROUP, S5_GROUPS, S5_STATE = 16, 32, 64
S5_WIDTH = S5_GROUPS * S5_STATE
RWKV_HEAD, RWKV_HEADS = 64, 8
RWKV_LORA = 64
RWKV_GN_EPS = 64e-5
DIFF_HEADS, DIFF_QK, DIFF_V, DIFF_ROT = 4, 64, 128, 16
DIFF_SUBLN_EPS = 1e-5

LANES = 128
SUBLANES = 8
VMEM_LIMIT_BYTES = 56 * 1024 * 1024

COL_GATE = 0
COL_CQ = 2048
COL_US5 = 2560
COL_QD = 3072
COL_KD = 3584
COL_VD = 4096
COL_R = 4608
COL_K = 5120
COL_V = 5632
COL_CKV = 6144
COL_KROPE = 6400
COL_LORA = 6528
N_PROJ = 6656

TM_INPROJ = 1024
TN_INPROJ = 1664
TM_OUTPROJ = 512
TM_PREP = 2048
TQ = 512
KS = 256
S5_BLOCK = 512
S5_SEG = S5_BLOCK // 8
S5_LANE_CHUNK = 2048
RWKV_BLOCK = 1024
RWKV_CHUNK = 64
RWKV_CHUNKS_INTERLEAVED = 4
NEG_BIG = -1e30
LOG2E = math.log2(math.e)


def _cparams(*sem):
    return pltpu.CompilerParams(dimension_semantics=sem, vmem_limit_bytes=VMEM_LIMIT_BYTES)


def _dot(a, b):
    return jnp.dot(a.astype(BF16), b.astype(BF16), preferred_element_type=F32)


def _dot_nt(a, b):
    return lax.dot_general(a.astype(BF16), b.astype(BF16), (((1,), (1,)), ((), ())), preferred_element_type=F32)


def _dot_split(x, w):
    hi = x.astype(BF16)
    lo = (x - hi.astype(F32)).astype(BF16)
    return jnp.dot(hi, w, preferred_element_type=F32) + jnp.dot(lo, w, preferred_element_type=F32)


def _dot_split3(w, x):
    hi = x.astype(BF16)
    r1 = x - hi.astype(F32)
    mid = r1.astype(BF16)
    lo = (r1 - mid.astype(F32)).astype(BF16)
    return (jnp.dot(w, hi, preferred_element_type=F32) + jnp.dot(w, mid, preferred_element_type=F32)
            + jnp.dot(w, lo, preferred_element_type=F32))


def _sigmoid(x):
    return 1.0 / (1.0 + jnp.exp(-x))


_W_IN_SEGMENTS = ((COL_GATE, 4544, D_MODEL), (COL_CQ, 0, W_GROUP), (COL_US5, 832, W_GROUP), (COL_QD, 3008, W_GROUP),
                  (COL_KD, 3520, W_GROUP), (COL_VD, 4032, W_GROUP), (COL_R, 1344, W_GROUP), (COL_K, 1856, W_GROUP),
                  (COL_V, 2368, W_GROUP), (COL_CKV, 512, MLA_KV_LORA), (COL_KROPE, 768, MLA_ROPE),
                  (COL_LORA, 2880, 2 * RWKV_LORA))
W_PREP_SUB = 4


def _w_in_block_table():
    src_blk, shift, zero_hi = [], [], []
    for ob in range(N_PROJ // LANES):
        col = ob * LANES
        out0, src0, width = next(s for s in _W_IN_SEGMENTS if s[0] <= col < s[0] + max(s[2], LANES))
        src = src0 + (col - out0)
        src_blk.append(src // LANES)
        shift.append((src % LANES) // (LANES // 2))
        zero_hi.append(int(width < LANES))
    return (np.asarray(src_blk, np.int32), np.asarray(shift, np.int32), np.asarray(zero_hi, np.int32))


def _w_prep_kernel(blk_ref, shift_ref, zero_ref, *refs):
    in_refs, o_ref = refs[:-1], refs[-1]
    c = pl.program_id(1)
    lower = lax.broadcasted_iota(jnp.int32, (1, LANES), 1) < LANES // 2
    for sub in range(W_PREP_SUB):
        a_ref, b_ref = in_refs[2 * sub], in_refs[2 * sub + 1]
        ob = c * W_PREP_SUB + sub
        cols = slice(sub * LANES, (sub + 1) * LANES)

        @pl.when(shift_ref[ob] == 1)
        def _():
            o_ref[0, :, cols] = jnp.where(lower, pltpu.roll(a_ref[0].astype(F32), LANES // 2, axis=1),
                                          pltpu.roll(b_ref[0].astype(F32), LANES // 2, axis=1)).astype(BF16)

        @pl.when((shift_ref[ob] == 0) & (zero_ref[ob] == 0))
        def _():
            o_ref[0, :, cols] = a_ref[0].astype(BF16)

        @pl.when((shift_ref[ob] == 0) & (zero_ref[ob] == 1))
        def _():
            o_ref[0, :, cols] = jnp.where(lower, a_ref[0].astype(F32), 0.0).astype(BF16)


def _w_prep(w_in):
    depth, d, n_in = w_in.shape
    src_blk, shift, zero_hi = _w_in_block_table()
    last = (n_in - 1) // LANES

    def in_spec(sub, nxt):
        return pl.BlockSpec((1, d, LANES), lambda l, c, blk, sh, zh: (
            l, 0, jnp.minimum(blk[c * W_PREP_SUB + sub] + nxt, last)))

    return pl.pallas_call(
        _w_prep_kernel,
        grid_spec=pltpu.PrefetchScalarGridSpec(
            num_scalar_prefetch=3,
            grid=(depth, N_PROJ // (LANES * W_PREP_SUB)),
            in_specs=[in_spec(sub, nxt) for sub in range(W_PREP_SUB) for nxt in (0, 1)],
            out_specs=pl.BlockSpec((1, d, LANES * W_PREP_SUB), lambda l, c, blk, sh, zh: (l, 0, c)),
        ),
        out_shape=jax.ShapeDtypeStruct((depth, d, N_PROJ), BF16),
        compiler_params=_cparams("parallel", "parallel"),
        name="w_prep",
    )(jnp.asarray(src_blk), jnp.asarray(shift), jnp.asarray(zero_hi), *([w_in] * (2 * W_PREP_SUB)))


def _inproj_kernel(x_ref, g_ref, w_ref, o_ref, h_ref):
    @pl.when(pl.program_id(1) == 0)
    def _():
        x = x_ref[...]
        y = x * lax.rsqrt(jnp.mean(x * x, axis=-1, keepdims=True) + NORM_EPS)
        h_ref[...] = (y * g_ref[...]).astype(BF16)

    o_ref[...] = jnp.dot(h_ref[...], w_ref[0], preferred_element_type=F32).astype(ACT)


def _inproj(x, g, w_all, layer):
    t, d = x.shape
    n = w_all.shape[2]
    return pl.pallas_call(
        _inproj_kernel,
        grid=(t // TM_INPROJ, n // TN_INPROJ),
        in_specs=[pl.BlockSpec((TM_INPROJ, d), lambda i, j: (i, 0)),
                  pl.BlockSpec((1, d), lambda i, j: (0, 0)),
                  pl.BlockSpec((1, d, TN_INPROJ), lambda i, j: (layer, 0, j))],
        out_specs=pl.BlockSpec((TM_INPROJ, TN_INPROJ), lambda i, j: (i, j)),
        out_shape=jax.ShapeDtypeStruct((t, n), ACT),
        scratch_shapes=[pltpu.VMEM((TM_INPROJ, d), BF16)],
        compiler_params=_cparams("parallel", "arbitrary"),
        name="inproj",
    )(x, g, w_all)


def _outproj_kernel(ya_ref, yb_ref, yc_ref, yd_ref, gate_ref, x_ref, w_ref, fg_ref, o_ref, *, final):
    y = jnp.concatenate([ya_ref[...], yb_ref[...], yc_ref[...], yd_ref[...]], axis=-1).astype(F32)
    gate = gate_ref[...].astype(F32)
    mixed = y * (gate * _sigmoid(gate))
    xn = x_ref[...] + _dot(mixed, w_ref[0])
    if final:
        xn = xn * lax.rsqrt(jnp.mean(xn * xn, axis=-1, keepdims=True) + NORM_EPS) * fg_ref[...]
    o_ref[...] = xn


def _outproj(ya, yb, yc, yd, proj, x, w_all, layer, fg, final):
    t, d = x.shape
    tm = TM_OUTPROJ
    yspec = pl.BlockSpec((tm, W_GROUP), lambda i: (i, 0))
    return pl.pallas_call(
        functools.partial(_outproj_kernel, final=final),
        grid=(t // tm,),
        in_specs=[yspec, yspec, yspec, yspec,
                  pl.BlockSpec((tm, d), lambda i: (i, COL_GATE // D_MODEL)),
                  pl.BlockSpec((tm, d), lambda i: (i, 0)),
                  pl.BlockSpec((1, d, d), lambda i: (layer, 0, 0)),
                  pl.BlockSpec((1, d), lambda i: (0, 0))],
        out_specs=pl.BlockSpec((tm, d), lambda i: (i, 0)),
        out_shape=jax.ShapeDtypeStruct((t, d), F32),
        compiler_params=_cparams("parallel"),
        name="outproj_final" if final else "outproj",
    )(ya, yb, yc, yd, proj, x, w_all, fg)


def _rope128(x, c, sa, sb, half):
    return x * c + pltpu.roll(x, LANES - half, axis=1) * sa + pltpu.roll(x, half, axis=1) * sb


def _store_vt_tiles(v, vt_ref):
    for n in range(v.shape[0] // KS):
        vt_ref[n] = v[n * KS:(n + 1) * KS, :].T.astype(BF16)


def _mla_prep_kernel(cq_ref, ckv_ref, kr_ref, qg_ref, kvg_ref, wq_ref, wkv_ref, c_ref, sa_ref, sb_ref,
                     q_ref, k_ref, vt_ref):
    scale = (MLA_NOPE + MLA_ROPE) ** -0.5 * LOG2E
    half = MLA_ROPE // 2
    c, sa, sb = c_ref[...], sa_ref[...], sb_ref[...]
    cq = cq_ref[...].astype(F32)
    hq = cq * lax.rsqrt(jnp.mean(cq * cq, axis=-1, keepdims=True) + NORM_EPS) * qg_ref[...]
    q = _dot(hq, wq_ref[...])
    ckv = ckv_ref[...].astype(F32)
    hkv = ckv * lax.rsqrt(jnp.mean(ckv * ckv, axis=-1, keepdims=True) + NORM_EPS) * kvg_ref[...]
    kv = _dot(hkv, wkv_ref[...])
    kpe = _rope128(kr_ref[...].astype(F32), c, sa, sb, half).astype(BF16)
    for h in range(MLA_HEADS):
        base = 2 * LANES * h
        q_ref[:, base:base + LANES] = (q[:, base:base + LANES] * scale).astype(BF16)
        qpe = _rope128(q[:, base + LANES:base + 2 * LANES], c, sa, sb, half) * scale
        q_ref[:, base + LANES:base + 2 * LANES] = qpe.astype(BF16)
        k_ref[:, base:base + LANES] = kv[:, LANES * h:LANES * (h + 1)].astype(BF16)
        k_ref[:, base + LANES:base + 2 * LANES] = kpe
    _store_vt_tiles(kv[:, MLA_HEADS * MLA_NOPE:], vt_ref)


def _mla_prep(proj, qg, kvg, wq, wkv, c, sa, sb):
    t = proj.shape[0]
    tm = TM_PREP
    tab = pl.BlockSpec((tm, LANES), lambda i: (i, 0))
    full = lambda shape: pl.BlockSpec(shape, lambda i: (0, 0))
    return pl.pallas_call(
        _mla_prep_kernel,
        grid=(t // tm,),
        in_specs=[pl.BlockSpec((tm, MLA_Q_LORA), lambda i: (i, COL_CQ // MLA_Q_LORA)),
                  pl.BlockSpec((tm, MLA_KV_LORA), lambda i: (i, COL_CKV // MLA_KV_LORA)),
                  pl.BlockSpec((tm, LANES), lambda i: (i, COL_KROPE // LANES)),
                  full((1, MLA_Q_LORA)), full((1, MLA_KV_LORA)),
                  full(wq.shape), full(wkv.shape), tab, tab, tab],
        out_specs=[pl.BlockSpec((tm, MLA_HEADS * 2 * LANES), lambda i: (i, 0)),
                   pl.BlockSpec((tm, MLA_HEADS * 2 * LANES), lambda i: (i, 0)),
                   pl.BlockSpec((tm // KS, MLA_HEADS * MLA_V, KS), lambda i: (i, 0, 0))],
        out_shape=[jax.ShapeDtypeStruct((t, MLA_HEADS * 2 * LANES), BF16),
                   jax.ShapeDtypeStruct((t, MLA_HEADS * 2 * LANES), BF16),
                   jax.ShapeDtypeStruct((t // KS, MLA_HEADS * MLA_V, KS), BF16)],
        compiler_params=_cparams("parallel"),
        name="mla_prep",
    )(proj, proj, proj, qg, kvg, wq, wkv, c, sa, sb)


def _attn_kernel(*refs, n_maps, out_scale):
    q_refs = refs[:n_maps]
    k_ref, vt_ref = refs[n_maps:n_maps + 2]
    if n_maps == 2:
        lam_ref, g_ref = refs[n_maps + 2:n_maps + 4]
        o_ref = refs[n_maps + 4]
        scratch = refs[n_maps + 5:]
    else:
        o_ref = refs[n_maps + 2]
        scratch = refs[n_maps + 3:]
    states = [scratch[3 * a:3 * a + 3] for a in range(n_maps)]
    s_slots = scratch[3 * n_maps:3 * n_maps + 2]
    nq = k_ref.shape[0] // TQ
    sub_per_q = TQ // KS
    i = pl.program_id(2)
    q_blocks = (i, nq - 1 - i)

    for m_ref, l_ref, acc_ref in states:
        m_ref[...] = jnp.full(m_ref.shape, NEG_BIG, F32)
        l_ref[...] = jnp.zeros(l_ref.shape, F32)
        acc_ref[...] = jnp.zeros(acc_ref.shape, F32)

    half = nq // 2
    items = [(0, q_blocks[0], True), (1, q_blocks[1], True)] + [(1, t, False) for t in range(half)]
    for t in range(half - 1):
        first = t < i
        items.append((jnp.where(first, 0, 1), jnp.where(first, t, half + t - i), False))

    def rows(blk, size):
        if isinstance(blk, int):
            return pl.ds(blk * size, size)
        return pl.ds(pl.multiple_of(blk * size, size), size)

    def scores_into(item, s_ref):
        sel, jq, _ = item
        qblk = q_blocks[sel] if isinstance(sel, int) else jnp.where(sel == 0, q_blocks[0], q_blocks[1])
        for sub in range(sub_per_q):
            k = k_ref[rows(jq * sub_per_q + sub, KS), :]
            for a, q_ref in enumerate(q_refs):
                s_ref[a, sub] = lax.dot_general(k, q_ref[rows(qblk, TQ), :], (((1,), (1,)), ((), ())),
                                                preferred_element_type=F32)

    def consume(item, s_ref):
        sel, jq, diagonal = item
        for sub in range(sub_per_q):
            vt = vt_ref[jq * sub_per_q + sub]
            for a, (m_ref, l_ref, acc_ref) in enumerate(states):
                s = s_ref[a, sub]
                if diagonal:
                    key = lax.broadcasted_iota(jnp.int32, s.shape, 0) + sub * KS
                    qry = lax.broadcasted_iota(jnp.int32, s.shape, 1)
                    s = jnp.where(key <= qry, s, NEG_BIG)
                m_prev = m_ref[sel]
                m_new = jnp.maximum(m_prev, jnp.max(s, axis=0, keepdims=True))
                alpha = jnp.exp2(m_prev - m_new)
                p = jnp.exp2(s - m_new)
                l_ref[sel] = alpha * l_ref[sel] + jnp.sum(p.reshape(KS // SUBLANES, SUBLANES, TQ), axis=0)
                acc_ref[sel] = alpha * acc_ref[sel] + jnp.dot(vt, p.astype(BF16), preferred_element_type=F32)
                m_ref[sel] = m_new

    scores_into(items[0], s_slots[0])
    for n, item in enumerate(items):
        if n + 1 < len(items):
            scores_into(items[n + 1], s_slots[(n + 1) % 2])
        consume(item, s_slots[n % 2])

    for sel in range(2):
        outs = []
        for m_ref, l_ref, acc_ref in states:
            inv_l = 1.0 / jnp.sum(l_ref[sel], axis=0, keepdims=True)
            outs.append(acc_ref[sel] * inv_l)
        if n_maps == 2:
            ot = outs[0] - lam_ref[0, 0] * outs[1]
            ot = ot * lax.rsqrt(jnp.mean(ot * ot, axis=0, keepdims=True) + DIFF_SUBLN_EPS)
            o_ref[rows(q_blocks[sel], TQ), :] = (ot.T * (g_ref[...] * out_scale)).astype(ACT)
        else:
            o_ref[rows(q_blocks[sel], TQ), :] = outs[0].T.astype(ACT)


def _attention(qs, k, vt, batch, heads, dqk, dv, extra=(), out_scale=1.0, name="attn"):
    n_maps = len(qs)
    t = k.shape[0]
    seq = t // batch
    nq = seq // TQ
    assert nq % 2 == 0
    seq_spec = lambda width: pl.BlockSpec((seq, width), lambda b, h, i: (b, h))
    in_specs = [seq_spec(dqk)] * (n_maps + 1) + [pl.BlockSpec((seq // KS, dv, KS), lambda b, h, i: (b, h, 0))]
    if n_maps == 2:
        in_specs += [pl.BlockSpec(memory_space=pltpu.SMEM), pl.BlockSpec((1, dv), lambda b, h, i: (0, 0))]
    state = [pltpu.VMEM((2, 1, TQ), F32), pltpu.VMEM((2, SUBLANES, TQ), F32), pltpu.VMEM((2, dv, TQ), F32)]
    s_slot = pltpu.VMEM((n_maps, TQ // KS, KS, TQ), F32)
    return pl.pallas_call(
        functools.partial(_attn_kernel, n_maps=n_maps, out_scale=out_scale),
        grid=(batch, heads, nq // 2),
        in_specs=in_specs,
        out_specs=seq_spec(dv),
        out_shape=jax.ShapeDtypeStruct((t, heads * dv), ACT),
        scratch_shapes=state * n_maps + [s_slot, s_slot],
        compiler_params=_cparams("parallel", "parallel", "arbitrary"),
        name=name,
    )(*qs, k, vt, *extra)


def _diff_prep_kernel(q_ref, k_ref, v_ref, c_ref, sa_ref, sb_ref, q1_ref, q2_ref, ko_ref, vt_ref):
    half = DIFF_ROT // 2
    scale = DIFF_QK ** -0.5 * LOG2E
    lane = lax.broadcasted_iota(jnp.int32, (1, LANES), 1)
    m0 = jnp.where(lane < DIFF_QK, scale, 0.0).astype(F32)
    m1 = scale - m0
    c, sa, sb = c_ref[...], sa_ref[...], sb_ref[...]
    for h in range(DIFF_HEADS):
        sl = slice(LANES * h, LANES * (h + 1))
        q = _rope128(q_ref[:, sl].astype(F32), c, sa, sb, half)
        q1_ref[:, sl] = (q * m0).astype(BF16)
        q2_ref[:, sl] = (q * m1).astype(BF16)
        ko_ref[:, sl] = _rope128(k_ref[:, sl].astype(F32), c, sa, sb, half).astype(BF16)
    _store_vt_tiles(v_ref[...].astype(F32), vt_ref)


def _diff_prep(proj, c, sa, sb):
    t = proj.shape[0]
    tm = TM_PREP
    tab = pl.BlockSpec((tm, LANES), lambda i: (i, 0))
    seg = lambda col: pl.BlockSpec((tm, W_GROUP), lambda i: (i, col // W_GROUP))
    out = pl.BlockSpec((tm, W_GROUP), lambda i: (i, 0))
    shp = jax.ShapeDtypeStruct((t, W_GROUP), BF16)
    return pl.pallas_call(
        _diff_prep_kernel,
        grid=(t // tm,),
        in_specs=[seg(COL_QD), seg(COL_KD), seg(COL_VD), tab, tab, tab],
        out_specs=[out, out, out, pl.BlockSpec((tm // KS, W_GROUP, KS), lambda i: (i, 0, 0))],
        out_shape=[shp, shp, shp, jax.ShapeDtypeStruct((t // KS, W_GROUP, KS), BF16)],
        compiler_params=_cparams("parallel"),
        name="diff_prep",
    )(proj, proj, proj, c, sa, sb)


def _s5_kernel(u_ref, bmat_ref, cre_ref, cim_ref, a1r_ref, a1i_ref, pwr_ref, pwi_ref, pcr_ref, pci_ref, d_ref,
               wglu_ref, bglu_ref, o_ref, hr_ref, hi_ref, sr_ref, si_ref, uf_ref, up_ref, op_ref):
    @pl.when(pl.program_id(1) == 0)
    def _():
        sr_ref[...] = jnp.zeros(sr_ref.shape, F32)
        si_ref[...] = jnp.zeros(si_ref.shape, F32)

    n_slabs = W_GROUP // LANES
    uf = u_ref[...].astype(F32)
    for q in range(n_slabs):
        uf_ref[q] = uf[:, q * LANES:(q + 1) * LANES]
    for t in range(S5_SEG):
        for q in range(n_slabs):
            up_ref[t * SUBLANES:(t + 1) * SUBLANES, q * LANES:(q + 1) * LANES] = (
                uf_ref[q, pl.ds(t, SUBLANES, stride=S5_SEG), :])
    u = up_ref[...]
    ub = u.astype(BF16)
    sw = S5_WIDTH // n_slabs
    for q in range(n_slabs):
        bu = jnp.dot(ub[:, q * LANES:(q + 1) * LANES], bmat_ref[q], preferred_element_type=F32)
        hr_ref[:, q * sw:(q + 1) * sw] = bu[:, :sw]
        hi_ref[:, q * sw:(q + 1) * sw] = bu[:, sw:]

    first_row = lax.broadcasted_iota(jnp.int32, (SUBLANES, S5_LANE_CHUNK), 0) == 0
    for c in range(S5_WIDTH // S5_LANE_CHUNK):
        ls = slice(c * S5_LANE_CHUNK, (c + 1) * S5_LANE_CHUNK)
        ar, ai = a1r_ref[:, ls], a1i_ref[:, ls]

        def tile(t, carry):
            xr, xi = carry
            rows = pl.ds(pl.multiple_of(t * SUBLANES, SUBLANES), SUBLANES)
            xr, xi = hr_ref[rows, ls] + (ar * xr - ai * xi), hi_ref[rows, ls] + (ar * xi + ai * xr)
            hr_ref[rows, ls] = xr
            hi_ref[rows, ls] = xi
            return xr, xi

        zero = jnp.zeros((SUBLANES, S5_LANE_CHUNK), F32)
        er, ei = lax.fori_loop(0, S5_SEG, tile, (zero, zero), unroll=4)

        gr = jnp.where(first_row, sr_ref[:, ls], pltpu.roll(er, 1, axis=0))
        gi = jnp.where(first_row, si_ref[:, ls], pltpu.roll(ei, 1, axis=0))
        for s in range(3):
            pr, pi = pwr_ref[s, :, ls], pwi_ref[s, :, ls]
            rr = pltpu.roll(gr, 1 << s, axis=0)
            ri = pltpu.roll(gi, 1 << s, axis=0)
            gr, gi = gr + (pr * rr - pi * ri), gi + (pr * ri + pi * rr)
        last = slice(SUBLANES - 1, SUBLANES)
        pr, pi = pwr_ref[0, last, ls], pwi_ref[0, last, ls]
        sr_ref[:, ls] = er[last] + (pr * gr[last] - pi * gi[last])
        si_ref[:, ls] = ei[last] + (pr * gi[last] + pi * gr[last])

        for t in range(S5_SEG):
            rows = slice(t * SUBLANES, (t + 1) * SUBLANES)
            pr, pi = pcr_ref[t:t + 1, ls], pci_ref[t:t + 1, ls]
            hr_ref[rows, ls] = hr_ref[rows, ls] + (pr * gr - pi * gi)
            hi_ref[rows, ls] = hi_ref[rows, ls] + (pr * gi + pi * gr)

    y = jnp.concatenate(
        [jnp.dot(hr_ref[:, q * sw:(q + 1) * sw].astype(BF16), cre_ref[q], preferred_element_type=F32)
         + jnp.dot(hi_ref[:, q * sw:(q + 1) * sw].astype(BF16), cim_ref[q], preferred_element_type=F32)
         for q in range(n_slabs)], axis=1)
    y = y + d_ref[...] * u
    g = 0.5 * y * (1.0 + jnp.tanh(math.sqrt(2.0 / math.pi) * (y + 0.044715 * (y * y * y))))
    out = g * _sigmoid(_dot(g, wglu_ref[...]) + bglu_ref[...])
    for q in range(n_slabs):
        op_ref[q] = out[:, q * LANES:(q + 1) * LANES]
    for r in range(SUBLANES):
        o_ref[r * S5_SEG:(r + 1) * S5_SEG, :] = jnp.concatenate(
            [op_ref[q, pl.ds(r, S5_SEG, stride=SUBLANES), :] for q in range(n_slabs)], axis=1).astype(ACT)


def _s5(proj, batch, bmat, cre, cim, a1r, a1i, pwr, pwi, pcr, pci, d, wglu, bglu):
    t = proj.shape[0]
    nb = t // batch // S5_BLOCK
    full2 = lambda a: pl.BlockSpec(a.shape, lambda b, i: (0,) * a.ndim)
    ucol = COL_US5 // W_GROUP
    return pl.pallas_call(
        _s5_kernel,
        grid=(batch, nb),
        in_specs=[pl.BlockSpec((S5_BLOCK, W_GROUP), lambda b, i: (b * nb + i, ucol))]
        + [full2(a) for a in (bmat, cre, cim, a1r, a1i, pwr, pwi, pcr, pci, d, wglu, bglu)],
        out_specs=pl.BlockSpec((S5_BLOCK, W_GROUP), lambda b, i: (b * nb + i, 0)),
        out_shape=jax.ShapeDtypeStruct((t, W_GROUP), ACT),
        scratch_shapes=[pltpu.VMEM((S5_BLOCK, S5_WIDTH), F32), pltpu.VMEM((S5_BLOCK, S5_WIDTH), F32),
                        pltpu.VMEM((1, S5_WIDTH), F32), pltpu.VMEM((1, S5_WIDTH), F32),
                        pltpu.VMEM((W_GROUP // LANES, S5_BLOCK, LANES), F32), pltpu.VMEM((S5_BLOCK, W_GROUP), F32),
                        pltpu.VMEM((W_GROUP // LANES, S5_BLOCK, LANES), F32)],
        compiler_params=_cparams("parallel", "arbitrary"),
        name="s5",
    )(proj, bmat, cre, cim, a1r, a1i, pwr, pwi, pcr, pci, d, wglu, bglu)


def _rwkv_kernel(r_ref, k_ref, v_ref, lo_ref, mu_r_ref, mu_k_ref, mu_v_ref, mu_lo_ref, w0_ref, w2_ref, a0_ref,
                 a2_ref, kk_ref, ka_ref, rk_ref, lng_ref, lnb_ref, ones_ref,
                 o_ref,
                 st_ref, pr_ref, pk_ref, pv_ref, plo_ref,
                 rt_ref, at_ref, bt_ref, kt_ref, bg_ref, kg_ref, vv_ref, ge_ref, oo_ref,
                 qc_ref, ec_ref, mc_ref, dc_ref):
    tb, lc = RWKV_BLOCK, RWKV_CHUNK

    @pl.when(pl.program_id(1) == 0)
    def _():
        st_ref[...] = jnp.zeros(st_ref.shape, F32)
        pr_ref[...] = jnp.zeros(pr_ref.shape, F32)
        pk_ref[...] = jnp.zeros(pk_ref.shape, F32)
        pv_ref[...] = jnp.zeros(pv_ref.shape, F32)
        plo_ref[...] = jnp.zeros(plo_ref.shape, F32)

    def shifted(z_ref, prev_ref, mu_ref):
        z = z_ref[...].astype(F32)
        first = lax.broadcasted_iota(jnp.int32, z.shape, 0) == 0
        z_prev = jnp.where(first, prev_ref[...], pltpu.roll(z, 1, axis=0))
        prev_ref[...] = z[tb - 1:tb, :]
        return z + (z_prev - z) * mu_ref[...]

    r = shifted(r_ref, pr_ref, mu_r_ref)
    k = shifted(k_ref, pk_ref, mu_k_ref)
    v = shifted(v_ref, pv_ref, mu_v_ref)
    lo = shifted(lo_ref, plo_ref, mu_lo_ref)

    ones = ones_ref[...]
    row = lax.broadcasted_iota(jnp.int32, (lc, lc), 0)
    col = lax.broadcasted_iota(jnp.int32, (lc, lc), 1)
    tri = (row >= col).astype(BF16)

    def prepare(chunks, rk_out):
        rg = slice(chunks[0] * lc, (chunks[-1] + 1) * lc)
        r_g, k_g, lo_g = r[rg], k[rg], lo[rg]
        wx = -(w0_ref[...] + _dot(jnp.tanh(lo_g), w2_ref[...]))
        yield
        w_log = -(jnp.maximum(wx, 0.0) + jnp.log(1.0 + jnp.exp(-jnp.abs(wx)))) - 0.5
        lw = -jnp.exp(w_log)
        yield
        a = _sigmoid(a0_ref[...] + _dot(lo_g, a2_ref[...]))
        yield
        kk = k_g * kk_ref[...]
        kk = kk / jnp.maximum(jnp.sqrt(_dot_split(kk * kk, ones)), 1e-12)
        yield
        k2 = k_g * (1.0 + (a - 1.0) * ka_ref[...])
        aa = -kk
        bb = kk * a
        rk_out.append(r_g * k2)
        vv_ref[rg, :] = v[rg]
        yield
        for n, c in enumerate(chunks):
            loc = slice(n * lc, (n + 1) * lc)
            rs = slice(c * lc, (c + 1) * lc)
            lwc = lw[loc]
            cum = _dot_split3(tri, lwc)
            cum_last = cum[lc - 1:lc, :]
            yield
            e_neg = jnp.exp(-cum)
            e_end = jnp.exp(cum_last - cum)
            rt_ref[rs, :] = r_g[loc] * jnp.exp(cum)
            at_ref[rs, :] = aa[loc] * jnp.exp(cum - lwc)
            yield
            bt_ref[rs, :] = bb[loc] * e_neg
            kt_ref[rs, :] = k2[loc] * e_neg
            bg_ref[rs, :] = bb[loc] * e_end
            kg_ref[rs, :] = k2[loc] * e_end
            ge_ref[c] = jnp.broadcast_to(jnp.exp(cum_last), (SUBLANES, W_GROUP))
            yield

    lane = lax.broadcasted_iota(jnp.int32, (1, LANES), 1)
    m0 = (lane < RWKV_HEAD).astype(F32)
    m1 = 1.0 - m0
    r2 = lax.broadcasted_iota(jnp.int32, (2 * lc, 2 * lc), 0)
    c2 = lax.broadcasted_iota(jnp.int32, (2 * lc, 2 * lc), 1)
    same = jnp.right_shift(r2, 6) == jnp.right_shift(c2, 6)
    assert lc == 64
    strict = jnp.where(same & (r2 > c2), 1.0, 0.0).astype(F32)
    incl = jnp.where(same & (r2 >= c2), 1.0, 0.0).astype(F32)
    eye = jnp.where(r2 == c2, 1.0, 0.0).astype(F32)
    n_double = int(math.log2(lc)) - 1

    def stack(x):
        return jnp.concatenate([x * m0, x * m1], axis=0)

    pairs = range(RWKV_HEADS // 2)
    lsl = [slice(LANES * p, LANES * (p + 1)) for p in pairs]
    n_chunks = tb // lc
    dot_nt = lambda x, y: lax.dot_general(x, y, (((1,), (1,)), ((), ())), preferred_element_type=F32)
    dot_tn = lambda x, y: lax.dot_general(x, y, (((0,), (0,)), ((), ())), preferred_element_type=F32)
    dot_nn = lambda x, y: jnp.dot(x, y, preferred_element_type=F32)
    bf = lambda xs: [x.astype(BF16) for x in xs]

    def chunk_terms(chunks):
        chains = [(c, p) for c in chunks for p in range(RWKV_HEADS // 2)]
        pairs = range(len(chains))
        tile = lambda ref: [stack(ref[c * lc:(c + 1) * lc, LANES * p:LANES * (p + 1)]) for c, p in chains]
        a_s = bf(tile(at_ref))
        r_f = tile(rt_ref)
        b_s = bf(tile(bt_ref))
        k_s = bf(tile(kt_ref))
        v_s = bf(tile(vv_ref))
        bg_s = bf(tile(bg_ref))
        kg_s = bf(tile(kg_ref))
        big = [dot_nt(jnp.concatenate([a_s[p], r_f[p].astype(BF16)], axis=0),
                      jnp.concatenate([b_s[p], k_s[p]], axis=0)) for p in pairs]
        nil = [big[p][:2 * lc, :2 * lc] * strict for p in pairs]
        a_ak = bf([big[p][:2 * lc, 2 * lc:] * strict for p in pairs])
        a_rb = bf([big[p][2 * lc:, :2 * lc] * incl for p in pairs])
        a_rk = bf([big[p][2 * lc:, 2 * lc:] * incl for p in pairs])
        yield
        akv = bf([dot_nn(a_ak[p], v_s[p]) for p in pairs])
        inv = [eye + nil[p] for p in pairs]
        nb = bf(nil)
        nil = [dot_nn(nb[p], nb[p]) for p in pairs]
        yield
        for step in range(1, n_double):
            nb = bf(nil)
            prod = [dot_nn(nb[p], jnp.concatenate([nb[p], inv[p].astype(BF16)], axis=1)) for p in pairs]
            nil = [prod[p][:, :2 * lc] for p in pairs]
            inv = [inv[p] + prod[p][:, 2 * lc:] for p in pairs]
            yield
        ib = bf(inv)
        inv = [inv[p] + dot_nn(nil[p].astype(BF16), ib[p]) for p in pairs]
        yield
        ib = bf(inv)
        tatv = bf([dot_nn(ib[p], jnp.concatenate([a_s[p], akv[p]], axis=1)) for p in pairs])
        yield
        qe = [dot_nn(a_rb[p], tatv[p]) for p in pairs]
        qc = [r_f[p] + qe[p][:, :2 * lc] for p in pairs]
        ec = [qe[p][:, 2 * lc:] + dot_nn(a_rk[p], v_s[p]) for p in pairs]
        yield
        md = [dot_tn(tatv[p], bg_s[p]) for p in pairs]
        mc = [md[p][:2 * lc] for p in pairs]
        dc = [md[p][2 * lc:] + dot_tn(v_s[p], kg_s[p]) for p in pairs]
        for j, (c, p) in enumerate(chains):
            idx = c * (RWKV_HEADS // 2) + p
            qc_ref[idx] = qc[j].astype(BF16)
            ec_ref[idx] = ec[j]
            mc_ref[idx] = mc[j].astype(BF16)
            dc_ref[idx] = dc[j]

    def run_alternating(*stage_generators):
        active = list(stage_generators)
        while active:
            for gen in list(active):
                if next(gen, "done") == "done":
                    active.remove(gen)

    state = [st_ref[p] for p in pairs]

    def advance_state(chunks):
        for c in chunks:
            sb = bf(state)
            os_ = [dot_nt(qc_ref[c * len(pairs) + p], sb[p]) + ec_ref[c * len(pairs) + p] for p in pairs]
            state[:] = [state[p] * ge_ref[c, 0:1, lsl[p]] + dot_nn(sb[p], mc_ref[c * len(pairs) + p])
                        + dc_ref[c * len(pairs) + p] for p in pairs]
            for p in pairs:
                oo_ref[c * lc:(c + 1) * lc, lsl[p]] = os_[p][:lc] + os_[p][lc:]
            yield

    groups = [range(c0, c0 + RWKV_CHUNKS_INTERLEAVED) for c0 in range(0, n_chunks, RWKV_CHUNKS_INTERLEAVED)]
    rk_parts = []
    run_alternating(prepare(groups[0], rk_parts))
    for g, group in enumerate(groups):
        stages = [chunk_terms(group)]
        if g + 1 < len(groups):
            stages.append(prepare(groups[g + 1], rk_parts))
        if g > 0:
            stages.append(advance_state(groups[g - 1]))
        run_alternating(*stages)
    run_alternating(advance_state(groups[-1]))
    rk2 = jnp.concatenate(rk_parts, axis=0)
    for p in pairs:
        st_ref[p] = state[p]

    o = oo_ref[...]
    inv_n = 1.0 / RWKV_HEAD
    mean = _dot_split(o, ones) * inv_n
    oc = o - mean
    var = _dot_split(oc * oc, ones) * inv_n
    o = oc * lax.rsqrt(var + RWKV_GN_EPS) * lng_ref[...] + lnb_ref[...]
    bonus = _dot_split(rk2 * rk_ref[...], ones) * v
    o_ref[...] = (o + bonus).astype(ACT)


def _rwkv(proj, batch, mu_r, mu_k, mu_v, mu_lo, w0, w2p, a0, a2p, k_k, k_a, r_k, ln_g, ln_b, ones):
    t = proj.shape[0]
    tb = RWKV_BLOCK
    nb = t // batch // tb
    seg = lambda col: pl.BlockSpec((tb, W_GROUP), lambda b, i: (b * nb + i, col // W_GROUP))
    full2 = lambda a: pl.BlockSpec(a.shape, lambda b, i: (0,) * a.ndim)
    params = (mu_r, mu_k, mu_v, mu_lo, w0, w2p, a0, a2p, k_k, k_a, r_k, ln_g, ln_b, ones)
    buf = pltpu.VMEM((tb, W_GROUP), F32)
    term = (tb // RWKV_CHUNK * (RWKV_HEADS // 2), LANES, LANES)
    return pl.pallas_call(
        _rwkv_kernel,
        grid=(batch, nb),
        in_specs=[seg(COL_R), seg(COL_K), seg(COL_V),
                  pl.BlockSpec((tb, LANES), lambda b, i: (b * nb + i, COL_LORA // LANES))]
        + [full2(a) for a in params],
        out_specs=pl.BlockSpec((tb, W_GROUP), lambda b, i: (b * nb + i, 0)),
        out_shape=jax.ShapeDtypeStruct((t, W_GROUP), ACT),
        scratch_shapes=[pltpu.VMEM((RWKV_HEADS // 2, LANES, LANES), F32),
                        pltpu.VMEM((1, W_GROUP), F32), pltpu.VMEM((1, W_GROUP), F32),
                        pltpu.VMEM((1, W_GROUP), F32), pltpu.VMEM((1, LANES), F32),
                        buf, buf, buf, buf, buf, buf, buf,
                        pltpu.VMEM((tb // RWKV_CHUNK, SUBLANES, W_GROUP), F32), buf,
                        pltpu.VMEM(term, BF16), pltpu.VMEM(term, F32), pltpu.VMEM(term, BF16), pltpu.VMEM(term, F32)],
        compiler_params=_cparams("parallel", "arbitrary"),
        name="rwkv",
    )(proj, proj, proj, proj, *params)


def _rope_tables(positions, rot, period, scale):
    half = rot // 2
    inv = ROPE_THETA ** (-jnp.arange(0, rot, 2, dtype=F32) / rot)
    ang = positions.reshape(-1).astype(F32)[:, None] * inv
    cos, sin = jnp.cos(ang), jnp.sin(ang)
    t = ang.shape[0]
    passthrough = period - rot
    c = jnp.concatenate([cos, cos, jnp.ones((t, passthrough), F32)], axis=1)
    sa = jnp.concatenate([-sin, jnp.zeros((t, half + passthrough), F32)], axis=1)
    sb = jnp.concatenate([jnp.zeros((t, half), F32), sin, jnp.zeros((t, passthrough), F32)], axis=1)
    reps = LANES // period
    return tuple(jnp.tile(a, (1, reps)) * scale for a in (c, sa, sb))


def _s5_tables(a_re, a_im, log_dt, b_re, b_im, c_re, c_im):
    lr = jnp.minimum(a_re, -1e-4)
    li = a_im
    dt = jnp.exp(log_dt)[:, None]
    mag = jnp.exp(dt * lr)
    ab_re, ab_im = mag * jnp.cos(dt * li), mag * jnp.sin(dt * li)
    den = lr * lr + li * li
    nr, ni = ab_re - 1.0, ab_im
    f_re = (nr * lr + ni * li) / den
    f_im = (ni * lr - nr * li) / den
    bb_re = f_re[..., None] * b_re - f_im[..., None] * b_im
    bb_im = f_re[..., None] * b_im + f_im[..., None] * b_re
    n_slabs = W_GROUP // LANES
    gps = S5_GROUPS // n_slabs
    sw = S5_WIDTH // n_slabs
    eye = jnp.eye(gps, dtype=F32)
    slab_in = lambda m: jnp.einsum('sgpc,gh->sgchp', m.reshape(n_slabs, gps, S5_STATE, S5_GROUP),
                                   eye).reshape(n_slabs, LANES, sw)
    slab_out = lambda m: jnp.einsum('sgcp,gh->sgphc', m.reshape(n_slabs, gps, S5_GROUP, S5_STATE),
                                    eye).reshape(n_slabs, sw, LANES)
    bmat = jnp.concatenate([slab_in(bb_re), slab_in(bb_im)], axis=2).astype(BF16)
    cre = slab_out(c_re).astype(BF16)
    cim = slab_out(-c_im).astype(BF16)

    def power(n):
        m = jnp.exp(n * dt * lr)
        return m * jnp.cos(n * dt * li), m * jnp.sin(n * dt * li)

    rows = jnp.arange(SUBLANES, dtype=F32)[:, None, None]
    a1r, a1i = power(jnp.ones((SUBLANES, 1, 1), F32))
    pwr, pwi = [], []
    for s in (1, 2, 4):
        pr, pi = power(jnp.full((1, 1, 1), float(s * S5_SEG), F32))
        keep = (rows >= s).astype(F32)
        pwr.append((keep * pr).reshape(SUBLANES, S5_WIDTH))
        pwi.append((keep * pi).reshape(SUBLANES, S5_WIDTH))
    pcr, pci = power(jnp.arange(1, S5_SEG + 1, dtype=F32)[:, None, None])
    return (bmat, cre, cim, a1r.reshape(SUBLANES, S5_WIDTH), a1i.reshape(SUBLANES, S5_WIDTH),
            jnp.stack(pwr), jnp.stack(pwi), pcr.reshape(S5_SEG, S5_WIDTH), pci.reshape(S5_SEG, S5_WIDTH))


def kernel(x, positions, norm_g, w_in, w_out, mla_q_norm_g, mla_kv_norm_g, mla_w_uq, mla_w_ukv, s5_a_re, s5_a_im, s5_log_dt, s5_b_re, s5_b_im, s5_c_re, s5_c_im, s5_d, s5_w_glu, s5_b_glu, rwkv_mu, rwkv_w0, rwkv_w2, rwkv_a0, rwkv_a2, rwkv_k_k, rwkv_k_a, rwkv_r_k, rwkv_ln_g, rwkv_ln_b, diff_lq1, diff_lk1, diff_lq2, diff_lk2, diff_subln_g, final_norm_g):
    batch, seq, d = x.shape
    depth = w_in.shape[0]
    t = batch * seq
    assert d == D_MODEL and seq % TQ == 0 and t % TM_INPROJ == 0
    assert seq % S5_BLOCK == 0 and seq % RWKV_BLOCK == 0

    ca, saa, sba = _rope_tables(positions, MLA_ROPE, LANES, 1.0)
    zero_hi = (jnp.arange(LANES) < MLA_ROPE).astype(F32)[None, :]
    ca = ca * zero_hi
    cd, sad, sbd = _rope_tables(positions, DIFF_ROT, DIFF_QK, 1.0)

    head_ones = jnp.kron(jnp.eye(RWKV_HEADS, dtype=F32), jnp.ones((RWKV_HEAD, RWKV_HEAD), F32)).astype(BF16)
    row = lambda a: a.reshape(1, -1).astype(F32)

    def layer_params(norm_g, qg, kvg, w_uq, w_ukv, a_re, a_im, log_dt, b_re, b_im, c_re, c_im, s5_d,
                     w_glu, b_glu, mu, w0, w2, a0, a2, k_k, k_a, r_k, ln_g, ln_b, lq1, lk1, lq2, lk2, subln_g):
        wq = w_uq.reshape(MLA_Q_LORA, MLA_HEADS, MLA_NOPE + MLA_ROPE)
        wq = jnp.pad(wq, ((0, 0), (0, 0), (0, 2 * LANES - MLA_NOPE - MLA_ROPE))).reshape(MLA_Q_LORA, -1)
        wkv = w_ukv.reshape(MLA_KV_LORA, MLA_HEADS, MLA_NOPE + MLA_V)
        wkv = jnp.concatenate([wkv[:, :, :MLA_NOPE].reshape(MLA_KV_LORA, -1),
                               wkv[:, :, MLA_NOPE:].reshape(MLA_KV_LORA, -1)], axis=1)
        zpad = jnp.zeros((RWKV_LORA, W_GROUP), F32)
        return dict(
            norm_g=row(norm_g),
            mla=(row(qg), row(kvg), wq.astype(BF16), wkv.astype(BF16)),
            s5=_s5_tables(a_re, a_im, log_dt, b_re, b_im, c_re, c_im) + (row(s5_d), w_glu.astype(BF16), row(b_glu)),
            rwkv=(row(mu[:W_GROUP]), row(mu[W_GROUP:2 * W_GROUP]), row(mu[2 * W_GROUP:3 * W_GROUP]),
                  row(mu[3 * W_GROUP:]), row(w0),
                  jnp.concatenate([w2, zpad], axis=0).astype(BF16),
                  row(a0),
                  jnp.concatenate([zpad, a2], axis=0).astype(BF16),
                  row(k_k), row(k_a), row(r_k), row(ln_g), row(ln_b)),
            lam=jnp.exp(jnp.sum(lq1 * lk1)) - jnp.exp(jnp.sum(lq2 * lk2)),
            subln_g=row(subln_g),
        )

    params = jax.vmap(layer_params)(
        norm_g, mla_q_norm_g, mla_kv_norm_g, mla_w_uq, mla_w_ukv, s5_a_re, s5_a_im, s5_log_dt, s5_b_re,
        s5_b_im, s5_c_re, s5_c_im, s5_d, s5_w_glu, s5_b_glu, rwkv_mu, rwkv_w0, rwkv_w2, rwkv_a0, rwkv_a2, rwkv_k_k,
        rwkv_k_a, rwkv_r_k, rwkv_ln_g, rwkv_ln_b, diff_lq1, diff_lk1, diff_lq2, diff_lk2, diff_subln_g)

    n_in = w_in.shape[2]
    w_in_all = _w_prep(jnp.pad(w_in.astype(BF16), ((0, 0), (0, 0), (0, -n_in % LANES))))
    w_out_all = w_out.astype(BF16)

    xf = x.reshape(t, d)
    for l in range(depth):
        p = jax.tree.map(lambda a: a[l], params)
        proj = _inproj(xf, p["norm_g"], w_in_all, l)

        qa, ka, va = _mla_prep(proj, *p["mla"], ca, saa, sba)
        y_a = _attention([qa], ka, va, batch, MLA_HEADS, 2 * LANES, MLA_V, name="mla_attn")

        y_b = _s5(proj, batch, *p["s5"])

        y_c = _rwkv(proj, batch, *p["rwkv"], head_ones)

        lam_init = 0.8 - 0.6 * math.exp(-0.3 * l)
        q1, q2, kd, vd = _diff_prep(proj, cd, sad, sbd)
        y_d = _attention([q1, q2], kd, vd, batch, DIFF_HEADS, LANES, DIFF_V,
                         extra=((p["lam"] + lam_init).reshape(1, 1).astype(F32), p["subln_g"]),
                         out_scale=1.0 - lam_init, name="diff_attn")

        xf = _outproj(y_a, y_b, y_c, y_d, proj, xf, w_out_all, l, row(final_norm_g), l == depth - 1)
    return xf.reshape(batch, seq, d)
```

```python
import functools
import math

import jax
import jax.numpy as jnp
import numpy as np
from jax import lax
from jax.experimental import pallas as pl
from jax.experimental.pallas import tpu as pltpu

F32 = jnp.float32
BF16 = jnp.bfloat16
ACT = jnp.bfloat16

D_MODEL = 2048
W_GROUP = 512
ROPE_THETA = 500000.0
NORM_EPS = 1e-6
MLA_HEADS, MLA_NOPE, MLA_ROPE, MLA_V = 4, 128, 64, 128
MLA_Q_LORA, MLA_KV_LORA = 512, 256
S5_GROUP, S5_GROUPS, S5_STATE = 16, 32, 64
S5_WIDTH = S5_GROUPS * S5_STATE
RWKV_HEAD, RWKV_HEADS = 64, 8
RWKV_LORA = 64
RWKV_GN_EPS = 64e-5
DIFF_HEADS, DIFF_QK, DIFF_V, DIFF_ROT = 4, 64, 128, 16
DIFF_SUBLN_EPS = 1e-5

LANES = 128
SUBLANES = 8
VMEM_LIMIT_BYTES = 56 * 1024 * 1024

COL_GATE = 0
COL_CQ = 2048
COL_US5 = 2560
COL_QD = 3072
COL_KD = 3584
COL_VD = 4096
COL_R = 4608
COL_K = 5120
COL_V = 5632
COL_CKV = 6144
COL_KROPE = 6400
COL_LORA = 6528
N_PROJ = 6656

TM_INPROJ = 512
TN_INPROJ = 6656
TM_OUTPROJ = 512
TM_PREP = 2048
TQ = 512
KS = 256
S5_BLOCK = 512
S5_SEG = S5_BLOCK // 8
S5_LANE_CHUNK = 2048
RWKV_BLOCK = 1024
RWKV_CHUNK = 64
RWKV_CHUNKS_INTERLEAVED = 4
NEG_BIG = -1e30
LOG2E = math.log2(math.e)


def _cparams(*sem):
    return pltpu.CompilerParams(dimension_semantics=sem, vmem_limit_bytes=VMEM_LIMIT_BYTES)


def _dot(a, b):
    return jnp.dot(a.astype(BF16), b.astype(BF16), preferred_element_type=F32)


def _dot_nt(a, b):
    return lax.dot_general(a.astype(BF16), b.astype(BF16), (((1,), (1,)), ((), ())), preferred_element_type=F32)


def _dot_split(x, w):
    hi = x.astype(BF16)
    lo = (x - hi.astype(F32)).astype(BF16)
    return jnp.dot(hi, w, preferred_element_type=F32) + jnp.dot(lo, w, preferred_element_type=F32)


def _dot_split3(w, x):
    hi = x.astype(BF16)
    r1 = x - hi.astype(F32)
    mid = r1.astype(BF16)
    lo = (r1 - mid.astype(F32)).astype(BF16)
    return (jnp.dot(w, hi, preferred_element_type=F32) + jnp.dot(w, mid, preferred_element_type=F32)
            + jnp.dot(w, lo, preferred_element_type=F32))


def _sigmoid(x):
    return 1.0 / (1.0 + jnp.exp(-x))


_W_IN_SEGMENTS = ((COL_GATE, 4544, D_MODEL), (COL_CQ, 0, W_GROUP), (COL_US5, 832, W_GROUP), (COL_QD, 3008, W_GROUP),
                  (COL_KD, 3520, W_GROUP), (COL_VD, 4032, W_GROUP), (COL_R, 1344, W_GROUP), (COL_K, 1856, W_GROUP),
                  (COL_V, 2368, W_GROUP), (COL_CKV, 512, MLA_KV_LORA), (COL_KROPE, 768, MLA_ROPE),
                  (COL_LORA, 2880, 2 * RWKV_LORA))
W_PREP_SUB = 4


def _w_in_block_table():
    src_blk, shift, zero_hi = [], [], []
    for ob in range(N_PROJ // LANES):
        col = ob * LANES
        out0, src0, width = next(s for s in _W_IN_SEGMENTS if s[0] <= col < s[0] + max(s[2], LANES))
        src = src0 + (col - out0)
        src_blk.append(src // LANES)
        shift.append((src % LANES) // (LANES // 2))
        zero_hi.append(int(width < LANES))
    return (np.asarray(src_blk, np.int32), np.asarray(shift, np.int32), np.asarray(zero_hi, np.int32))


def _w_prep_kernel(blk_ref, shift_ref, zero_ref, *refs):
    in_refs, o_ref = refs[:-1], refs[-1]
    c = pl.program_id(1)
    lower = lax.broadcasted_iota(jnp.int32, (1, LANES), 1) < LANES // 2
    for sub in range(W_PREP_SUB):
        a_ref, b_ref = in_refs[2 * sub], in_refs[2 * sub + 1]
        ob = c * W_PREP_SUB + sub
        cols = slice(sub * LANES, (sub + 1) * LANES)

        @pl.when(shift_ref[ob] == 1)
        def _():
            o_ref[0, :, cols] = jnp.where(lower, pltpu.roll(a_ref[0].astype(F32), LANES // 2, axis=1),
                                          pltpu.roll(b_ref[0].astype(F32), LANES // 2, axis=1)).astype(BF16)

        @pl.when((shift_ref[ob] == 0) & (zero_ref[ob] == 0))
        def _():
            o_ref[0, :, cols] = a_ref[0].astype(BF16)

        @pl.when((shift_ref[ob] == 0) & (zero_ref[ob] == 1))
        def _():
            o_ref[0, :, cols] = jnp.where(lower, a_ref[0].astype(F32), 0.0).astype(BF16)


def _w_prep(w_in):
    depth, d, n_in = w_in.shape
    src_blk, shift, zero_hi = _w_in_block_table()
    last = (n_in - 1) // LANES

    def in_spec(sub, nxt):
        return pl.BlockSpec((1, d, LANES), lambda l, c, blk, sh, zh: (
            l, 0, jnp.minimum(blk[c * W_PREP_SUB + sub] + nxt, last)))

    return pl.pallas_call(
        _w_prep_kernel,
        grid_spec=pltpu.PrefetchScalarGridSpec(
            num_scalar_prefetch=3,
            grid=(depth, N_PROJ // (LANES * W_PREP_SUB)),
            in_specs=[in_spec(sub, nxt) for sub in range(W_PREP_SUB) for nxt in (0, 1)],
            out_specs=pl.BlockSpec((1, d, LANES * W_PREP_SUB), lambda l, c, blk, sh, zh: (l, 0, c)),
        ),
        out_shape=jax.ShapeDtypeStruct((depth, d, N_PROJ), BF16),
        compiler_params=_cparams("parallel", "parallel"),
        name="w_prep",
    )(jnp.asarray(src_blk), jnp.asarray(shift), jnp.asarray(zero_hi), *([w_in] * (2 * W_PREP_SUB)))


def _inproj_kernel(x_ref, g_ref, w_ref, o_ref, h_ref):
    @pl.when(pl.program_id(1) == 0)
    def _():
        x = x_ref[...]
        y = x * lax.rsqrt(jnp.mean(x * x, axis=-1, keepdims=True) + NORM_EPS)
        h_ref[...] = (y * g_ref[...]).astype(BF16)

    o_ref[...] = jnp.dot(h_ref[...], w_ref[0], preferred_element_type=F32).astype(ACT)


def _inproj(x, g, w_all, layer):
    t, d = x.shape
    n = w_all.shape[2]
    return pl.pallas_call(
        _inproj_kernel,
        grid=(t // TM_INPROJ, n // TN_INPROJ),
        in_specs=[pl.BlockSpec((TM_INPROJ, d), lambda i, j: (i, 0)),
                  pl.BlockSpec((1, d), lambda i, j: (0, 0)),
                  pl.BlockSpec((1, d, TN_INPROJ), lambda i, j: (layer, 0, j), pipeline_mode=pl.Buffered(1))],
        out_specs=pl.BlockSpec((TM_INPROJ, TN_INPROJ), lambda i, j: (i, j)),
        out_shape=jax.ShapeDtypeStruct((t, n), ACT),
        scratch_shapes=[pltpu.VMEM((TM_INPROJ, d), BF16)],
        compiler_params=_cparams("parallel", "arbitrary"),
        name="inproj",
    )(x, g, w_all)


def _outproj_kernel(ya_ref, yb_ref, yc_ref, yd_ref, gate_ref, x_ref, w_ref, fg_ref, o_ref, *, final):
    y = jnp.concatenate([ya_ref[...], yb_ref[...], yc_ref[...], yd_ref[...]], axis=-1).astype(F32)
    gate = gate_ref[...].astype(F32)
    mixed = y * (gate * _sigmoid(gate))
    xn = x_ref[...] + _dot(mixed, w_ref[0])
    if final:
        xn = xn * lax.rsqrt(jnp.mean(xn * xn, axis=-1, keepdims=True) + NORM_EPS) * fg_ref[...]
    o_ref[...] = xn


def _outproj(ya, yb, yc, yd, proj, x, w_all, layer, fg, final):
    t, d = x.shape
    tm = TM_OUTPROJ
    yspec = pl.BlockSpec((tm, W_GROUP), lambda i: (i, 0))
    return pl.pallas_call(
        functools.partial(_outproj_kernel, final=final),
        grid=(t // tm,),
        in_specs=[yspec, yspec, yspec, yspec,
                  pl.BlockSpec((tm, d), lambda i: (i, COL_GATE // D_MODEL)),
                  pl.BlockSpec((tm, d), lambda i: (i, 0)),
                  pl.BlockSpec((1, d, d), lambda i: (layer, 0, 0)),
                  pl.BlockSpec((1, d), lambda i: (0, 0))],
        out_specs=pl.BlockSpec((tm, d), lambda i: (i, 0)),
        out_shape=jax.ShapeDtypeStruct((t, d), F32),
        compiler_params=_cparams("parallel"),
        name="outproj_final" if final else "outproj",
    )(ya, yb, yc, yd, proj, x, w_all, fg)


def _rope128(x, c, sa, sb, half):
    return x * c + pltpu.roll(x, LANES - half, axis=1) * sa + pltpu.roll(x, half, axis=1) * sb


def _store_vt_tiles(v, vt_ref):
    for n in range(v.shape[0] // KS):
        vt_ref[n] = v[n * KS:(n + 1) * KS, :].T.astype(BF16)


def _mla_prep_kernel(cq_ref, ckv_ref, kr_ref, qg_ref, kvg_ref, wq_ref, wkv_ref, c_ref, sa_ref, sb_ref,
                     q_ref, k_ref, vt_ref):
    scale = (MLA_NOPE + MLA_ROPE) ** -0.5 * LOG2E
    half = MLA_ROPE // 2
    c, sa, sb = c_ref[...], sa_ref[...], sb_ref[...]
    cq = cq_ref[...].astype(F32)
    hq = cq * lax.rsqrt(jnp.mean(cq * cq, axis=-1, keepdims=True) + NORM_EPS) * qg_ref[...]
    q = _dot(hq, wq_ref[...])
    ckv = ckv_ref[...].astype(F32)
    hkv = ckv * lax.rsqrt(jnp.mean(ckv * ckv, axis=-1, keepdims=True) + NORM_EPS) * kvg_ref[...]
    kv = _dot(hkv, wkv_ref[...])
    kpe = _rope128(kr_ref[...].astype(F32), c, sa, sb, half).astype(BF16)
    for h in range(MLA_HEADS):
        base = 2 * LANES * h
        q_ref[:, base:base + LANES] = (q[:, base:base + LANES] * scale).astype(BF16)
        qpe = _rope128(q[:, base + LANES:base + 2 * LANES], c, sa, sb, half) * scale
        q_ref[:, base + LANES:base + 2 * LANES] = qpe.astype(BF16)
        k_ref[:, base:base + LANES] = kv[:, LANES * h:LANES * (h + 1)].astype(BF16)
        k_ref[:, base + LANES:base + 2 * LANES] = kpe
    _store_vt_tiles(kv[:, MLA_HEADS * MLA_NOPE:], vt_ref)


def _mla_prep(proj, qg, kvg, wq, wkv, c, sa, sb):
    t = proj.shape[0]
    tm = TM_PREP
    tab = pl.BlockSpec((tm, LANES), lambda i: (i, 0))
    full = lambda shape: pl.BlockSpec(shape, lambda i: (0, 0))
    return pl.pallas_call(
        _mla_prep_kernel,
        grid=(t // tm,),
        in_specs=[pl.BlockSpec((tm, MLA_Q_LORA), lambda i: (i, COL_CQ // MLA_Q_LORA)),
                  pl.BlockSpec((tm, MLA_KV_LORA), lambda i: (i, COL_CKV // MLA_KV_LORA)),
                  pl.BlockSpec((tm, LANES), lambda i: (i, COL_KROPE // LANES)),
                  full((1, MLA_Q_LORA)), full((1, MLA_KV_LORA)),
                  full(wq.shape), full(wkv.shape), tab, tab, tab],
        out_specs=[pl.BlockSpec((tm, MLA_HEADS * 2 * LANES), lambda i: (i, 0)),
                   pl.BlockSpec((tm, MLA_HEADS * 2 * LANES), lambda i: (i, 0)),
                   pl.BlockSpec((tm // KS, MLA_HEADS * MLA_V, KS), lambda i: (i, 0, 0))],
        out_shape=[jax.ShapeDtypeStruct((t, MLA_HEADS * 2 * LANES), BF16),
                   jax.ShapeDtypeStruct((t, MLA_HEADS * 2 * LANES), BF16),
                   jax.ShapeDtypeStruct((t // KS, MLA_HEADS * MLA_V, KS), BF16)],
        compiler_params=_cparams("parallel"),
        name="mla_prep",
    )(proj, proj, proj, qg, kvg, wq, wkv, c, sa, sb)


def _attn_kernel(*refs, n_maps, out_scale):
    q_refs = refs[:n_maps]
    k_ref, vt_ref = refs[n_maps:n_maps + 2]
    if n_maps == 2:
        lam_ref, g_ref = refs[n_maps + 2:n_maps + 4]
        o_ref = refs[n_maps + 4]
        scratch = refs[n_maps + 5:]
    else:
        o_ref = refs[n_maps + 2]
        scratch = refs[n_maps + 3:]
    states = [scratch[3 * a:3 * a + 3] for a in range(n_maps)]
    s_slots = scratch[3 * n_maps:3 * n_maps + 2]
    nq = k_ref.shape[0] // TQ
    sub_per_q = TQ // KS
    i = pl.program_id(2)
    q_blocks = (i, nq - 1 - i)

    for m_ref, l_ref, acc_ref in states:
        m_ref[...] = jnp.full(m_ref.shape, NEG_BIG, F32)
        l_ref[...] = jnp.zeros(l_ref.shape, F32)
        acc_ref[...] = jnp.zeros(acc_ref.shape, F32)

    half = nq // 2
    items = [(0, q_blocks[0], True), (1, q_blocks[1], True)] + [(1, t, False) for t in range(half)]
    for t in range(half - 1):
        first = t < i
        items.append((jnp.where(first, 0, 1), jnp.where(first, t, half + t - i), False))

    def rows(blk, size):
        if isinstance(blk, int):
            return pl.ds(blk * size, size)
        return pl.ds(pl.multiple_of(blk * size, size), size)

    def scores_into(item, s_ref):
        sel, jq, _ = item
        qblk = q_blocks[sel] if isinstance(sel, int) else jnp.where(sel == 0, q_blocks[0], q_blocks[1])
        for sub in range(sub_per_q):
            k = k_ref[rows(jq * sub_per_q + sub, KS), :]
            for a, q_ref in enumerate(q_refs):
                s_ref[a, sub] = lax.dot_general(k, q_ref[rows(qblk, TQ), :], (((1,), (1,)), ((), ())),
                                                preferred_element_type=F32)

    def consume(item, s_ref):
        sel, jq, diagonal = item
        for sub in range(sub_per_q):
            vt = vt_ref[jq * sub_per_q + sub]
            for a, (m_ref, l_ref, acc_ref) in enumerate(states):
                s = s_ref[a, sub]
                if diagonal:
                    key = lax.broadcasted_iota(jnp.int32, s.shape, 0) + sub * KS
                    qry = lax.broadcasted_iota(jnp.int32, s.shape, 1)
                    s = jnp.where(key <= qry, s, NEG_BIG)
                m_prev = m_ref[sel]
                m_new = jnp.maximum(m_prev, jnp.max(s, axis=0, keepdims=True))
                alpha = jnp.exp2(m_prev - m_new)
                p = jnp.exp2(s - m_new)
                l_ref[sel] = alpha * l_ref[sel] + jnp.sum(p.reshape(KS // SUBLANES, SUBLANES, TQ), axis=0)
                acc_ref[sel] = alpha * acc_ref[sel] + jnp.dot(vt, p.astype(BF16), preferred_element_type=F32)
                m_ref[sel] = m_new

    scores_into(items[0], s_slots[0])
    for n, item in enumerate(items):
        if n + 1 < len(items):
            scores_into(items[n + 1], s_slots[(n + 1) % 2])
        consume(item, s_slots[n % 2])

    for sel in range(2):
        outs = []
        for m_ref, l_ref, acc_ref in states:
            inv_l = 1.0 / jnp.sum(l_ref[sel], axis=0, keepdims=True)
            outs.append(acc_ref[sel] * inv_l)
        if n_maps == 2:
            ot = outs[0] - lam_ref[0, 0] * outs[1]
            ot = ot * lax.rsqrt(jnp.mean(ot * ot, axis=0, keepdims=True) + DIFF_SUBLN_EPS)
            o_ref[rows(q_blocks[sel], TQ), :] = (ot.T * (g_ref[...] * out_scale)).astype(ACT)
        else:
            o_ref[rows(q_blocks[sel], TQ), :] = outs[0].T.astype(ACT)


def _attention(qs, k, vt, batch, heads, dqk, dv, extra=(), out_scale=1.0, name="attn"):
    n_maps = len(qs)
    t = k.shape[0]
    seq = t // batch
    nq = seq // TQ
    assert nq % 2 == 0
    seq_spec = lambda width: pl.BlockSpec((seq, width), lambda b, h, i: (b, h))
    in_specs = [seq_spec(dqk)] * (n_maps + 1) + [pl.BlockSpec((seq // KS, dv, KS), lambda b, h, i: (b, h, 0))]
    if n_maps == 2:
        in_specs += [pl.BlockSpec(memory_space=pltpu.SMEM), pl.BlockSpec((1, dv), lambda b, h, i: (0, 0))]
    state = [pltpu.VMEM((2, 1, TQ), F32), pltpu.VMEM((2, SUBLANES, TQ), F32), pltpu.VMEM((2, dv, TQ), F32)]
    s_slot = pltpu.VMEM((n_maps, TQ // KS, KS, TQ), F32)
    return pl.pallas_call(
        functools.partial(_attn_kernel, n_maps=n_maps, out_scale=out_scale),
        grid=(batch, heads, nq // 2),
        in_specs=in_specs,
        out_specs=seq_spec(dv),
        out_shape=jax.ShapeDtypeStruct((t, heads * dv), ACT),
        scratch_shapes=state * n_maps + [s_slot, s_slot],
        compiler_params=_cparams("parallel", "parallel", "arbitrary"),
        name=name,
    )(*qs, k, vt, *extra)


def _diff_prep_kernel(q_ref, k_ref, v_ref, c_ref, sa_ref, sb_ref, q1_ref, q2_ref, ko_ref, vt_ref):
    half = DIFF_ROT // 2
    scale = DIFF_QK ** -0.5 * LOG2E
    lane = lax.broadcasted_iota(jnp.int32, (1, LANES), 1)
    m0 = jnp.where(lane < DIFF_QK, scale, 0.0).astype(F32)
    m1 = scale - m0
    c, sa, sb = c_ref[...], sa_ref[...], sb_ref[...]
    for h in range(DIFF_HEADS):
        sl = slice(LANES * h, LANES * (h + 1))
        q = _rope128(q_ref[:, sl].astype(F32), c, sa, sb, half)
        q1_ref[:, sl] = (q * m0).astype(BF16)
        q2_ref[:, sl] = (q * m1).astype(BF16)
        ko_ref[:, sl] = _rope128(k_ref[:, sl].astype(F32), c, sa, sb, half).astype(BF16)
    _store_vt_tiles(v_ref[...].astype(F32), vt_ref)


def _diff_prep(proj, c, sa, sb):
    t = proj.shape[0]
    tm = TM_PREP
    tab = pl.BlockSpec((tm, LANES), lambda i: (i, 0))
    seg = lambda col: pl.BlockSpec((tm, W_GROUP), lambda i: (i, col // W_GROUP))
    out = pl.BlockSpec((tm, W_GROUP), lambda i: (i, 0))
    shp = jax.ShapeDtypeStruct((t, W_GROUP), BF16)
    return pl.pallas_call(
        _diff_prep_kernel,
        grid=(t // tm,),
        in_specs=[seg(COL_QD), seg(COL_KD), seg(COL_VD), tab, tab, tab],
        out_specs=[out, out, out, pl.BlockSpec((tm // KS, W_GROUP, KS), lambda i: (i, 0, 0))],
        out_shape=[shp, shp, shp, jax.ShapeDtypeStruct((t // KS, W_GROUP, KS), BF16)],
        compiler_params=_cparams("parallel"),
        name="diff_prep",
    )(proj, proj, proj, c, sa, sb)


def _s5_kernel(u_ref, bmat_ref, cre_ref, cim_ref, a1r_ref, a1i_ref, pwr_ref, pwi_ref, pcr_ref, pci_ref, d_ref,
               wglu_ref, bglu_ref, o_ref, hr_ref, hi_ref, sr_ref, si_ref, uf_ref, up_ref, op_ref):
    @pl.when(pl.program_id(1) == 0)
    def _():
        sr_ref[...] = jnp.zeros(sr_ref.shape, F32)
        si_ref[...] = jnp.zeros(si_ref.shape, F32)

    n_slabs = W_GROUP // LANES
    uf = u_ref[...].astype(F32)
    for q in range(n_slabs):
        uf_ref[q] = uf[:, q * LANES:(q + 1) * LANES]
    for t in range(S5_SEG):
        for q in range(n_slabs):
            up_ref[t * SUBLANES:(t + 1) * SUBLANES, q * LANES:(q + 1) * LANES] = (
                uf_ref[q, pl.ds(t, SUBLANES, stride=S5_SEG), :])
    u = up_ref[...]
    ub = u.astype(BF16)
    sw = S5_WIDTH // n_slabs
    for q in range(n_slabs):
        bu = jnp.dot(ub[:, q * LANES:(q + 1) * LANES], bmat_ref[q], preferred_element_type=F32)
        hr_ref[:, q * sw:(q + 1) * sw] = bu[:, :sw]
        hi_ref[:, q * sw:(q + 1) * sw] = bu[:, sw:]

    first_row = lax.broadcasted_iota(jnp.int32, (SUBLANES, S5_LANE_CHUNK), 0) == 0
    for c in range(S5_WIDTH // S5_LANE_CHUNK):
        ls = slice(c * S5_LANE_CHUNK, (c + 1) * S5_LANE_CHUNK)
        ar, ai = a1r_ref[:, ls], a1i_ref[:, ls]

        def tile(t, carry):
            xr, xi = carry
            rows = pl.ds(pl.multiple_of(t * SUBLANES, SUBLANES), SUBLANES)
            xr, xi = hr_ref[rows, ls] + (ar * xr - ai * xi), hi_ref[rows, ls] + (ar * xi + ai * xr)
            hr_ref[rows, ls] = xr
            hi_ref[rows, ls] = xi
            return xr, xi

        zero = jnp.zeros((SUBLANES, S5_LANE_CHUNK), F32)
        er, ei = lax.fori_loop(0, S5_SEG, tile, (zero, zero), unroll=4)

        gr = jnp.where(first_row, sr_ref[:, ls], pltpu.roll(er, 1, axis=0))
        gi = jnp.where(first_row, si_ref[:, ls], pltpu.roll(ei, 1, axis=0))
        for s in range(3):
            pr, pi = pwr_ref[s, :, ls], pwi_ref[s, :, ls]
            rr = pltpu.roll(gr, 1 << s, axis=0)
            ri = pltpu.roll(gi, 1 << s, axis=0)
            gr, gi = gr + (pr * rr - pi * ri), gi + (pr * ri + pi * rr)
        last = slice(SUBLANES - 1, SUBLANES)
        pr, pi = pwr_ref[0, last, ls], pwi_ref[0, last, ls]
        sr_ref[:, ls] = er[last] + (pr * gr[last] - pi * gi[last])
        si_ref[:, ls] = ei[last] + (pr * gi[last] + pi * gr[last])

        for t in range(S5_SEG):
            rows = slice(t * SUBLANES, (t + 1) * SUBLANES)
            pr, pi = pcr_ref[t:t + 1, ls], pci_ref[t:t + 1, ls]
            hr_ref[rows, ls] = hr_ref[rows, ls] + (pr * gr - pi * gi)
            hi_ref[rows, ls] = hi_ref[rows, ls] + (pr * gi + pi * gr)

    y = jnp.concatenate(
        [jnp.dot(hr_ref[:, q * sw:(q + 1) * sw].astype(BF16), cre_ref[q], preferred_element_type=F32)
         + jnp.dot(hi_ref[:, q * sw:(q + 1) * sw].astype(BF16), cim_ref[q], preferred_element_type=F32)
         for q in range(n_slabs)], axis=1)
    y = y + d_ref[...] * u
    g = 0.5 * y * (1.0 + jnp.tanh(math.sqrt(2.0 / math.pi) * (y + 0.044715 * (y * y * y))))
    out = g * _sigmoid(_dot(g, wglu_ref[...]) + bglu_ref[...])
    for q in range(n_slabs):
        op_ref[q] = out[:, q * LANES:(q + 1) * LANES]
    for r in range(SUBLANES):
        o_ref[r * S5_SEG:(r + 1) * S5_SEG, :] = jnp.concatenate(
            [op_ref[q, pl.ds(r, S5_SEG, stride=SUBLANES), :] for q in range(n_slabs)], axis=1).astype(ACT)


def _s5(proj, batch, bmat, cre, cim, a1r, a1i, pwr, pwi, pcr, pci, d, wglu, bglu):
    t = proj.shape[0]
    nb = t // batch // S5_BLOCK
    full2 = lambda a: pl.BlockSpec(a.shape, lambda b, i: (0,) * a.ndim)
    ucol = COL_US5 // W_GROUP
    return pl.pallas_call(
        _s5_kernel,
        grid=(batch, nb),
        in_specs=[pl.BlockSpec((S5_BLOCK, W_GROUP), lambda b, i: (b * nb + i, ucol))]
        + [full2(a) for a in (bmat, cre, cim, a1r, a1i, pwr, pwi, pcr, pci, d, wglu, bglu)],
        out_specs=pl.BlockSpec((S5_BLOCK, W_GROUP), lambda b, i: (b * nb + i, 0)),
        out_shape=jax.ShapeDtypeStruct((t, W_GROUP), ACT),
        scratch_shapes=[pltpu.VMEM((S5_BLOCK, S5_WIDTH), F32), pltpu.VMEM((S5_BLOCK, S5_WIDTH), F32),
                        pltpu.VMEM((1, S5_WIDTH), F32), pltpu.VMEM((1, S5_WIDTH), F32),
                        pltpu.VMEM((W_GROUP // LANES, S5_BLOCK, LANES), F32), pltpu.VMEM((S5_BLOCK, W_GROUP), F32),
                        pltpu.VMEM((W_GROUP // LANES, S5_BLOCK, LANES), F32)],
        compiler_params=_cparams("parallel", "arbitrary"),
        name="s5",
    )(proj, bmat, cre, cim, a1r, a1i, pwr, pwi, pcr, pci, d, wglu, bglu)


def _rwkv_kernel(r_ref, k_ref, v_ref, lo_ref, mu_r_ref, mu_k_ref, mu_v_ref, mu_lo_ref, w0_ref, w2_ref, a0_ref,
                 a2_ref, kk_ref, ka_ref, rk_ref, lng_ref, lnb_ref, ones_ref,
                 o_ref,
                 st_ref, pr_ref, pk_ref, pv_ref, plo_ref,
                 rt_ref, at_ref, bt_ref, kt_ref, bg_ref, kg_ref, vv_ref, ge_ref, oo_ref,
                 qc_ref, ec_ref, mc_ref, dc_ref):
    tb, lc = RWKV_BLOCK, RWKV_CHUNK

    @pl.when(pl.program_id(1) == 0)
    def _():
        st_ref[...] = jnp.zeros(st_ref.shape, F32)
        pr_ref[...] = jnp.zeros(pr_ref.shape, F32)
        pk_ref[...] = jnp.zeros(pk_ref.shape, F32)
        pv_ref[...] = jnp.zeros(pv_ref.shape, F32)
        plo_ref[...] = jnp.zeros(plo_ref.shape, F32)

    def shifted(z_ref, prev_ref, mu_ref):
        z = z_ref[...].astype(F32)
        first = lax.broadcasted_iota(jnp.int32, z.shape, 0) == 0
        z_prev = jnp.where(first, prev_ref[...], pltpu.roll(z, 1, axis=0))
        prev_ref[...] = z[tb - 1:tb, :]
        return z + (z_prev - z) * mu_ref[...]

    r = shifted(r_ref, pr_ref, mu_r_ref)
    k = shifted(k_ref, pk_ref, mu_k_ref)
    v = shifted(v_ref, pv_ref, mu_v_ref)
    lo = shifted(lo_ref, plo_ref, mu_lo_ref)

    ones = ones_ref[...]
    row = lax.broadcasted_iota(jnp.int32, (lc, lc), 0)
    col = lax.broadcasted_iota(jnp.int32, (lc, lc), 1)
    tri = (row >= col).astype(BF16)

    def prepare(chunks, rk_out):
        rg = slice(chunks[0] * lc, (chunks[-1] + 1) * lc)
        r_g, k_g, lo_g = r[rg], k[rg], lo[rg]
        wx = -(w0_ref[...] + _dot(jnp.tanh(lo_g), w2_ref[...]))
        yield
        w_log = -(jnp.maximum(wx, 0.0) + jnp.log(1.0 + jnp.exp(-jnp.abs(wx)))) - 0.5
        lw = -jnp.exp(w_log)
        yield
        a = _sigmoid(a0_ref[...] + _dot(lo_g, a2_ref[...]))
        yield
        kk = k_g * kk_ref[...]
        kk = kk / jnp.maximum(jnp.sqrt(_dot_split(kk * kk, ones)), 1e-12)
        yield
        k2 = k_g * (1.0 + (a - 1.0) * ka_ref[...])
        aa = -kk
        bb = kk * a
        rk_out.append(r_g * k2)
        vv_ref[rg, :] = v[rg]
        yield
        for n, c in enumerate(chunks):
            loc = slice(n * lc, (n + 1) * lc)
            rs = slice(c * lc, (c + 1) * lc)
            lwc = lw[loc]
            cum = _dot_split3(tri, lwc)
            cum_last = cum[lc - 1:lc, :]
            yield
            e_neg = jnp.exp(-cum)
            e_end = jnp.exp(cum_last - cum)
            rt_ref[rs, :] = r_g[loc] * jnp.exp(cum)
            at_ref[rs, :] = aa[loc] * jnp.exp(cum - lwc)
            yield
            bt_ref[rs, :] = bb[loc] * e_neg
            kt_ref[rs, :] = k2[loc] * e_neg
            bg_ref[rs, :] = bb[loc] * e_end
            kg_ref[rs, :] = k2[loc] * e_end
            ge_ref[c] = jnp.broadcast_to(jnp.exp(cum_last), (SUBLANES, W_GROUP))
            yield

    lane = lax.broadcasted_iota(jnp.int32, (1, LANES), 1)
    m0 = (lane < RWKV_HEAD).astype(F32)
    m1 = 1.0 - m0
    r2 = lax.broadcasted_iota(jnp.int32, (2 * lc, 2 * lc), 0)
    c2 = lax.broadcasted_iota(jnp.int32, (2 * lc, 2 * lc), 1)
    same = jnp.right_shift(r2, 6) == jnp.right_shift(c2, 6)
    assert lc == 64
    strict = jnp.where(same & (r2 > c2), 1.0, 0.0).astype(F32)
    incl = jnp.where(same & (r2 >= c2), 1.0, 0.0).astype(F32)
    eye = jnp.where(r2 == c2, 1.0, 0.0).astype(F32)
    n_double = int(math.log2(lc)) - 1

    def stack(x):
        return jnp.concatenate([x * m0, x * m1], axis=0)

    pairs = range(RWKV_HEADS // 2)
    lsl = [slice(LANES * p, LANES * (p + 1)) for p in pairs]
    n_chunks = tb // lc
    dot_nt = lambda x, y: lax.dot_general(x, y, (((1,), (1,)), ((), ())), preferred_element_type=F32)
    dot_tn = lambda x, y: lax.dot_general(x, y, (((0,), (0,)), ((), ())), preferred_element_type=F32)
    dot_nn = lambda x, y: jnp.dot(x, y, preferred_element_type=F32)
    bf = lambda xs: [x.astype(BF16) for x in xs]

    def chunk_terms(chunks):
        chains = [(c, p) for c in chunks for p in range(RWKV_HEADS // 2)]
        pairs = range(len(chains))
        tile = lambda ref: [stack(ref[c * lc:(c + 1) * lc, LANES * p:LANES * (p + 1)]) for c, p in chains]
        a_s = bf(tile(at_ref))
        r_f = tile(rt_ref)
        b_s = bf(tile(bt_ref))
        k_s = bf(tile(kt_ref))
        v_s = bf(tile(vv_ref))
        bg_s = bf(tile(bg_ref))
        kg_s = bf(tile(kg_ref))
        big = [dot_nt(jnp.concatenate([a_s[p], r_f[p].astype(BF16)], axis=0),
                      jnp.concatenate([b_s[p], k_s[p]], axis=0)) for p in pairs]
        nil = [big[p][:2 * lc, :2 * lc] * strict for p in pairs]
        a_ak = bf([big[p][:2 * lc, 2 * lc:] * strict for p in pairs])
        a_rb = bf([big[p][2 * lc:, :2 * lc] * incl for p in pairs])
        a_rk = bf([big[p][2 * lc:, 2 * lc:] * incl for p in pairs])
        yield
        akv = bf([dot_nn(a_ak[p], v_s[p]) for p in pairs])
        inv = [eye + nil[p] for p in pairs]
        nb = bf(nil)
        nil = [dot_nn(nb[p], nb[p]) for p in pairs]
        yield
        for step in range(1, n_double):
            nb = bf(nil)
            prod = [dot_nn(nb[p], jnp.concatenate([nb[p], inv[p].astype(BF16)], axis=1)) for p in pairs]
            nil = [prod[p][:, :2 * lc] for p in pairs]
            inv = [inv[p] + prod[p][:, 2 * lc:] for p in pairs]
            yield
        ib = bf(inv)
        inv = [inv[p] + dot_nn(nil[p].astype(BF16), ib[p]) for p in pairs]
        yield
        ib = bf(inv)
        tatv = bf([dot_nn(ib[p], jnp.concatenate([a_s[p], akv[p]], axis=1)) for p in pairs])
        yield
        qe = [dot_nn(a_rb[p], tatv[p]) for p in pairs]
        qc = [r_f[p] + qe[p][:, :2 * lc] for p in pairs]
        ec = [qe[p][:, 2 * lc:] + dot_nn(a_rk[p], v_s[p]) for p in pairs]
        yield
        md = [dot_tn(tatv[p], bg_s[p]) for p in pairs]
        mc = [md[p][:2 * lc] for p in pairs]
        dc = [md[p][2 * lc:] + dot_tn(v_s[p], kg_s[p]) for p in pairs]
        for j, (c, p) in enumerate(chains):
            idx = c * (RWKV_HEADS // 2) + p
            qc_ref[idx] = qc[j].astype(BF16)
            ec_ref[idx] = ec[j]
            mc_ref[idx] = mc[j].astype(BF16)
            dc_ref[idx] = dc[j]

    def run_alternating(*stage_generators):
        active = list(stage_generators)
        while active:
            for gen in list(active):
                if next(gen, "done") == "done":
                    active.remove(gen)

    state = [st_ref[p] for p in pairs]

    def advance_state(chunks):
        for c in chunks:
            sb = bf(state)
            os_ = [dot_nt(qc_ref[c * len(pairs) + p], sb[p]) + ec_ref[c * len(pairs) + p] for p in pairs]
            state[:] = [state[p] * ge_ref[c, 0:1, lsl[p]] + dot_nn(sb[p], mc_ref[c * len(pairs) + p])
                        + dc_ref[c * len(pairs) + p] for p in pairs]
            for p in pairs:
                oo_ref[c * lc:(c + 1) * lc, lsl[p]] = os_[p][:lc] + os_[p][lc:]
            yield

    groups = [range(c0, c0 + RWKV_CHUNKS_INTERLEAVED) for c0 in range(0, n_chunks, RWKV_CHUNKS_INTERLEAVED)]
    rk_parts = []
    run_alternating(prepare(groups[0], rk_parts))
    for g, group in enumerate(groups):
        stages = [chunk_terms(group)]
        if g + 1 < len(groups):
            stages.append(prepare(groups[g + 1], rk_parts))
        if g > 0:
            stages.append(advance_state(groups[g - 1]))
        run_alternating(*stages)
    run_alternating(advance_state(groups[-1]))
    rk2 = jnp.concatenate(rk_parts, axis=0)
    for p in pairs:
        st_ref[p] = state[p]

    o = oo_ref[...]
    inv_n = 1.0 / RWKV_HEAD
    mean = _dot_split(o, ones) * inv_n
    oc = o - mean
    var = _dot_split(oc * oc, ones) * inv_n
    o = oc * lax.rsqrt(var + RWKV_GN_EPS) * lng_ref[...] + lnb_ref[...]
    bonus = _dot_split(rk2 * rk_ref[...], ones) * v
    o_ref[...] = (o + bonus).astype(ACT)


def _rwkv(proj, batch, mu_r, mu_k, mu_v, mu_lo, w0, w2p, a0, a2p, k_k, k_a, r_k, ln_g, ln_b, ones):
    t = proj.shape[0]
    tb = RWKV_BLOCK
    nb = t // batch // tb
    seg = lambda col: pl.BlockSpec((tb, W_GROUP), lambda b, i: (b * nb + i, col // W_GROUP))
    full2 = lambda a: pl.BlockSpec(a.shape, lambda b, i: (0,) * a.ndim)
    params = (mu_r, mu_k, mu_v, mu_lo, w0, w2p, a0, a2p, k_k, k_a, r_k, ln_g, ln_b, ones)
    buf = pltpu.VMEM((tb, W_GROUP), F32)
    term = (tb // RWKV_CHUNK * (RWKV_HEADS // 2), LANES, LANES)
    return pl.pallas_call(
        _rwkv_kernel,
        grid=(batch, nb),
        in_specs=[seg(COL_R), seg(COL_K), seg(COL_V),
                  pl.BlockSpec((tb, LANES), lambda b, i: (b * nb + i, COL_LORA // LANES))]
        + [full2(a) for a in params],
        out_specs=pl.BlockSpec((tb, W_GROUP), lambda b, i: (b * nb + i, 0)),
        out_shape=jax.ShapeDtypeStruct((t, W_GROUP), ACT),
        scratch_shapes=[pltpu.VMEM((RWKV_HEADS // 2, LANES, LANES), F32),
                        pltpu.VMEM((1, W_GROUP), F32), pltpu.VMEM((1, W_GROUP), F32),
                        pltpu.VMEM((1, W_GROUP), F32), pltpu.VMEM((1, LANES), F32),
                        buf, buf, buf, buf, buf, buf, buf,
                        pltpu.VMEM((tb // RWKV_CHUNK, SUBLANES, W_GROUP), F32), buf,
                        pltpu.VMEM(term, BF16), pltpu.VMEM(term, F32), pltpu.VMEM(term, BF16), pltpu.VMEM(term, F32)],
        compiler_params=_cparams("parallel", "arbitrary"),
        name="rwkv",
    )(proj, proj, proj, proj, *params)


def _rope_tables(positions, rot, period, scale):
    half = rot // 2
    inv = ROPE_THETA ** (-jnp.arange(0, rot, 2, dtype=F32) / rot)
    ang = positions.reshape(-1).astype(F32)[:, None] * inv
    cos, sin = jnp.cos(ang), jnp.sin(ang)
    t = ang.shape[0]
    passthrough = period - rot
    c = jnp.concatenate([cos, cos, jnp.ones((t, passthrough), F32)], axis=1)
    sa = jnp.concatenate([-sin, jnp.zeros((t, half + passthrough), F32)], axis=1)
    sb = jnp.concatenate([jnp.zeros((t, half), F32), sin, jnp.zeros((t, passthrough), F32)], axis=1)
    reps = LANES // period
    return tuple(jnp.tile(a, (1, reps)) * scale for a in (c, sa, sb))


def _s5_tables(a_re, a_im, log_dt, b_re, b_im, c_re, c_im):
    lr = jnp.minimum(a_re, -1e-4)
    li = a_im
    dt = jnp.exp(log_dt)[:, None]
    mag = jnp.exp(dt * lr)
    ab_re, ab_im = mag * jnp.cos(dt * li), mag * jnp.sin(dt * li)
    den = lr * lr + li * li
    nr, ni = ab_re - 1.0, ab_im
    f_re = (nr * lr + ni * li) / den
    f_im = (ni * lr - nr * li) / den
    bb_re = f_re[..., None] * b_re - f_im[..., None] * b_im
    bb_im = f_re[..., None] * b_im + f_im[..., None] * b_re
    n_slabs = W_GROUP // LANES
    gps = S5_GROUPS // n_slabs
    sw = S5_WIDTH // n_slabs
    eye = jnp.eye(gps, dtype=F32)
    slab_in = lambda m: jnp.einsum('sgpc,gh->sgchp', m.reshape(n_slabs, gps, S5_STATE, S5_GROUP),
                                   eye).reshape(n_slabs, LANES, sw)
    slab_out = lambda m: jnp.einsum('sgcp,gh->sgphc', m.reshape(n_slabs, gps, S5_GROUP, S5_STATE),
                                    eye).reshape(n_slabs, sw, LANES)
    bmat = jnp.concatenate([slab_in(bb_re), slab_in(bb_im)], axis=2).astype(BF16)
    cre = slab_out(c_re).astype(BF16)
    cim = slab_out(-c_im).astype(BF16)

    def power(n):
        m = jnp.exp(n * dt * lr)
        return m * jnp.cos(n * dt * li), m * jnp.sin(n * dt * li)

    rows = jnp.arange(SUBLANES, dtype=F32)[:, None, None]
    a1r, a1i = power(jnp.ones((SUBLANES, 1, 1), F32))
    pwr, pwi = [], []
    for s in (1, 2, 4):
        pr, pi = power(jnp.full((1, 1, 1), float(s * S5_SEG), F32))
        keep = (rows >= s).astype(F32)
        pwr.append((keep * pr).reshape(SUBLANES, S5_WIDTH))
        pwi.append((keep * pi).reshape(SUBLANES, S5_WIDTH))
    pcr, pci = power(jnp.arange(1, S5_SEG + 1, dtype=F32)[:, None, None])
    return (bmat, cre, cim, a1r.reshape(SUBLANES, S5_WIDTH), a1i.reshape(SUBLANES, S5_WIDTH),
            jnp.stack(pwr), jnp.stack(pwi), pcr.reshape(S5_SEG, S5_WIDTH), pci.reshape(S5_SEG, S5_WIDTH))


def kernel(x, positions, norm_g, w_in, w_out, mla_q_norm_g, mla_kv_norm_g, mla_w_uq, mla_w_ukv, s5_a_re, s5_a_im, s5_log_dt, s5_b_re, s5_b_im, s5_c_re, s5_c_im, s5_d, s5_w_glu, s5_b_glu, rwkv_mu, rwkv_w0, rwkv_w2, rwkv_a0, rwkv_a2, rwkv_k_k, rwkv_k_a, rwkv_r_k, rwkv_ln_g, rwkv_ln_b, diff_lq1, diff_lk1, diff_lq2, diff_lk2, diff_subln_g, final_norm_g):
    batch, seq, d = x.shape
    depth = w_in.shape[0]
    t = batch * seq
    assert d == D_MODEL and seq % TQ == 0 and t % TM_INPROJ == 0
    assert seq % S5_BLOCK == 0 and seq % RWKV_BLOCK == 0

    ca, saa, sba = _rope_tables(positions, MLA_ROPE, LANES, 1.0)
    zero_hi = (jnp.arange(LANES) < MLA_ROPE).astype(F32)[None, :]
    ca = ca * zero_hi
    cd, sad, sbd = _rope_tables(positions, DIFF_ROT, DIFF_QK, 1.0)

    head_ones = jnp.kron(jnp.eye(RWKV_HEADS, dtype=F32), jnp.ones((RWKV_HEAD, RWKV_HEAD), F32)).astype(BF16)
    row = lambda a: a.reshape(1, -1).astype(F32)

    def layer_params(norm_g, qg, kvg, w_uq, w_ukv, a_re, a_im, log_dt, b_re, b_im, c_re, c_im, s5_d,
                     w_glu, b_glu, mu, w0, w2, a0, a2, k_k, k_a, r_k, ln_g, ln_b, lq1, lk1, lq2, lk2, subln_g):
        wq = w_uq.reshape(MLA_Q_LORA, MLA_HEADS, MLA_NOPE + MLA_ROPE)
        wq = jnp.pad(wq, ((0, 0), (0, 0), (0, 2 * LANES - MLA_NOPE - MLA_ROPE))).reshape(MLA_Q_LORA, -1)
        wkv = w_ukv.reshape(MLA_KV_LORA, MLA_HEADS, MLA_NOPE + MLA_V)
        wkv = jnp.concatenate([wkv[:, :, :MLA_NOPE].reshape(MLA_KV_LORA, -1),
                               wkv[:, :, MLA_NOPE:].reshape(MLA_KV_LORA, -1)], axis=1)
        zpad = jnp.zeros((RWKV_LORA, W_GROUP), F32)
        return dict(
            norm_g=row(norm_g),
            mla=(row(qg), row(kvg), wq.astype(BF16), wkv.astype(BF16)),
            s5=_s5_tables(a_re, a_im, log_dt, b_re, b_im, c_re, c_im) + (row(s5_d), w_glu.astype(BF16), row(b_glu)),
            rwkv=(row(mu[:W_GROUP]), row(mu[W_GROUP:2 * W_GROUP]), row(mu[2 * W_GROUP:3 * W_GROUP]),
                  row(mu[3 * W_GROUP:]), row(w0),
                  jnp.concatenate([w2, zpad], axis=0).astype(BF16),
                  row(a0),
                  jnp.concatenate([zpad, a2], axis=0).astype(BF16),
                  row(k_k), row(k_a), row(r_k), row(ln_g), row(ln_b)),
            lam=jnp.exp(jnp.sum(lq1 * lk1)) - jnp.exp(jnp.sum(lq2 * lk2)),
            subln_g=row(subln_g),
        )

    params = jax.vmap(layer_params)(
        norm_g, mla_q_norm_g, mla_kv_norm_g, mla_w_uq, mla_w_ukv, s5_a_re, s5_a_im, s5_log_dt, s5_b_re,
        s5_b_im, s5_c_re, s5_c_im, s5_d, s5_w_glu, s5_b_glu, rwkv_mu, rwkv_w0, rwkv_w2, rwkv_a0, rwkv_a2, rwkv_k_k,
        rwkv_k_a, rwkv_r_k, rwkv_ln_g, rwkv_ln_b, diff_lq1, diff_lk1, diff_lq2, diff_lk2, diff_subln_g)

    n_in = w_in.shape[2]
    w_in_all = _w_prep(jnp.pad(w_in.astype(BF16), ((0, 0), (0, 0), (0, -n_in % LANES))))
    w_out_all = w_out.astype(BF16)

    xf = x.reshape(t, d)
    for l in range(depth):
        p = jax.tree.map(lambda a: a[l], params)
        proj = _inproj(xf, p["norm_g"], w_in_all, l)

        qa, ka, va = _mla_prep(proj, *p["mla"], ca, saa, sba)
        y_a = _attention([qa], ka, va, batch, MLA_HEADS, 2 * LANES, MLA_V, name="mla_attn")

        y_b = _s5(proj, batch, *p["s5"])

        y_c = _rwkv(proj, batch, *p["rwkv"], head_ones)

        lam_init = 0.8 - 0.6 * math.exp(-0.3 * l)
        q1, q2, kd, vd = _diff_prep(proj, cd, sad, sbd)
        y_d = _attention([q1, q2], kd, vd, batch, DIFF_HEADS, LANES, DIFF_V,
                         extra=((p["lam"] + lam_init).reshape(1, 1).astype(F32), p["subln_g"]),
                         out_scale=1.0 - lam_init, name="diff_attn")

        xf = _outproj(y_a, y_b, y_c, y_d, proj, xf, w_out_all, l, row(final_norm_g), l == depth - 1)
    return xf.reshape(batch, seq, d)
```
